```python
import jax, jax.numpy as jnp
from jax import lax
import numpy as np

D_MODEL = 1024
BATCH = 32
SEQ = 2048
DEPTH = 1

CHUNK = 64
N_META = 16
EPS = 1e-6
DN_HEADS = 8
DN_HEAD_DIM = 128
DN_QK = DN_HEADS * DN_HEAD_DIM
DN_V = DN_HEADS * DN_HEAD_DIM
CONV_WIDTH = 4
GLA_HEADS = 4
GLA_KEY_DIM = 128
GLA_VAL_DIM = 256
GLA_QK = GLA_HEADS * GLA_KEY_DIM
GLA_V = GLA_HEADS * GLA_VAL_DIM
GLA_RANK = 16
GLA_TAU = 16.0
N_GROUPS = 8
EXPERTS_PER_GROUP = 8
N_EXPERTS = N_GROUPS * EXPERTS_PER_GROUP
TOP_K = 2
D_EXPERT = 512
MOE_BLOCK = 256
IN_SPLITS = (DN_QK, DN_QK, DN_V, DN_V, DN_HEADS, DN_HEADS,
             GLA_QK, GLA_QK, GLA_V, GLA_V, GLA_RANK, D_MODEL, D_MODEL)
D_IN = sum(IN_SPLITS)

kernel_name = 'streaming_hybrid_deltanet_gla_hmoe'


def rmsnorm(x, g):
    xf = x.astype(jnp.float32)
    y = xf * lax.rsqrt(jnp.mean(jnp.square(xf), axis=-1, keepdims=True) + EPS)
    return (y * g.astype(jnp.float32)).astype(x.dtype)


def l2norm(x):
    return x * lax.rsqrt(jnp.sum(jnp.square(x), axis=-1, keepdims=True) + EPS)


def causal_conv(x, w):
    k_w, L = w.shape[0], x.shape[1]
    xp = jnp.pad(x, ((0, 0), (k_w - 1, 0), (0, 0)))
    return sum(xp[:, j:j + L] * w[j] for j in range(k_w))


def chunkify(x, pad):
    x = jnp.pad(x, ((0, 0), (pad, 0)) + ((0, 0),) * (x.ndim - 2))
    b, lp = x.shape[:2]
    x = x.reshape((b, lp // CHUNK, CHUNK) + x.shape[2:])
    return x.transpose((1, 0, 3, 2) + tuple(range(4, x.ndim)))


def unchunkify(o, pad):
    n, b, h, c, dv = o.shape
    return o.transpose(1, 0, 3, 2, 4).reshape(b, n * c, h, dv)[:, pad:]


def gated_delta_rule(q, k, v, g, beta):
    L = q.shape[1]
    pad = (-L) % CHUNK
    dk, dv = q.shape[-1], v.shape[-1]
    q, k, v = [chunkify(t, pad) for t in (q * dk ** -0.5, k, v)]
    g, beta = chunkify(g, pad), chunkify(beta, pad)
    gc = jnp.cumsum(g, axis=-1)
    causal = jnp.tril(jnp.ones((CHUNK, CHUNK), bool))
    strict = jnp.tril(jnp.ones((CHUNK, CHUNK), bool), -1)
    diff = gc[..., :, None] - gc[..., None, :]
    decay = jnp.where(causal, jnp.exp(jnp.where(causal, diff, 0.0)), 0.0)
    kk = jnp.einsum('nbhid,nbhjd->nbhij', k, k)
    a_mat = jnp.where(strict, beta[..., None] * kk * decay, 0.0) + jnp.eye(CHUNK, dtype=jnp.float32)
    rhs = jnp.concatenate([v * beta[..., None], k * (beta * jnp.exp(gc))[..., None]], axis=-1)
    sol = lax.linalg.triangular_solve(a_mat, rhs, left_side=True, lower=True, unit_diagonal=True)
    u, w = sol[..., :dv], sol[..., dv:]
    qk = jnp.where(causal, jnp.einsum('nbhid,nbhjd->nbhij', q, k) * decay, 0.0)
    q_dec = q * jnp.exp(gc)[..., None]
    g_last = gc[..., -1]
    k_dec = k * jnp.exp(g_last[..., None] - gc)[..., None]

    def step(S, xs):
        qk_c, q_c, k_c, u_c, w_c, gl = xs
        v_new = u_c - jnp.einsum('bhcd,bhde->bhce', w_c, S)
        o = jnp.einsum('bhcd,bhde->bhce', q_c, S) + jnp.einsum('bhij,bhje->bhie', qk_c, v_new)
        S = S * jnp.exp(gl)[..., None, None] + jnp.einsum('bhcd,bhce->bhde', k_c, v_new)
        return S, o

    S0 = jnp.zeros(q.shape[1:3] + (dk, dv), jnp.float32)
    _, o = lax.scan(step, S0, (qk, q_dec, k_dec, u, w, g_last))
    return unchunkify(o, pad)


def gla_attention(q, k, v, log_a):
    L = q.shape[1]
    pad = (-L) % CHUNK
    dk, dv = q.shape[-1], v.shape[-1]
    q, k, v, log_a = [chunkify(t, pad) for t in (q * dk ** -0.5, k, v, log_a)]
    b = jnp.cumsum(log_a, axis=-2)
    b_mid = b[..., CHUNK // 2 - 1:CHUNK // 2, :]
    causal = jnp.tril(jnp.ones((CHUNK, CHUNK), bool))
    att = jnp.einsum('nbhid,nbhjd->nbhij', q * jnp.exp(b - b_mid), k * jnp.exp(b_mid - b))
    o_intra = jnp.einsum('nbhij,nbhje->nbhie', jnp.where(causal, att, 0.0), v)
    q_dec = q * jnp.exp(b)
    b_last = b[..., -1, :]
    k_dec = k * jnp.exp(b_last[..., None, :] - b)

    def step(S, xs):
        q_c, k_c, v_c, bl = xs
        o = jnp.einsum('bhcd,bhde->bhce', q_c, S)
        S = S * jnp.exp(bl)[..., :, None] + jnp.einsum('bhcd,bhce->bhde', k_c, v_c)
        return S, o

    S0 = jnp.zeros(q.shape[1:3] + (dk, dv), jnp.float32)
    _, o_inter = lax.scan(step, S0, (q_dec, k_dec, v, b_last))
    return unchunkify(o_inter + o_intra, pad)


def hybrid_mixer(h, w_in, conv_dn, a_log, dt_bias, norm_head_dn, w_proj_dn,
                 w_alpha, b_alpha, norm_head_gla, w_proj_gla, w_out):
    f32 = jnp.float32
    bsz, L, _ = h.shape
    proj = h @ w_in
    splits = np.cumsum(IN_SPLITS)[:-1].tolist()
    (q_dn, k_dn, v_dn, z_dn, a_dn, b_dn, q_gla, k_gla, v_gla, r_gla, lr_gla,
     gate_dn, gate_gla) = jnp.split(proj, splits, axis=-1)
    qkv = jax.nn.silu(causal_conv(jnp.concatenate([q_dn, k_dn, v_dn], -1), conv_dn)).astype(f32)
    q_dn, k_dn, v_dn = jnp.split(qkv, [DN_QK, 2 * DN_QK], axis=-1)
    q_dn = l2norm(q_dn.reshape(bsz, L, DN_HEADS, DN_HEAD_DIM))
    k_dn = l2norm(k_dn.reshape(bsz, L, DN_HEADS, DN_HEAD_DIM))
    v_dn = v_dn.reshape(bsz, L, DN_HEADS, DN_V // DN_HEADS)
    g = -jnp.exp(a_log.astype(f32)) * jax.nn.softplus(a_dn.astype(f32) + dt_bias.astype(f32))
    beta = jax.nn.sigmoid(b_dn.astype(f32))
    o_dn = gated_delta_rule(q_dn, k_dn, v_dn, g, beta)
    o_dn = rmsnorm(o_dn, norm_head_dn) * jax.nn.silu(z_dn.astype(f32).reshape(bsz, L, DN_HEADS, -1))
    y_dn = o_dn.reshape(bsz, L, DN_V).astype(h.dtype) @ w_proj_dn
    log_alpha = jax.nn.log_sigmoid((lr_gla @ w_alpha).astype(f32) + b_alpha.astype(f32)) / GLA_TAU
    o_gla = gla_attention(q_gla.astype(f32).reshape(bsz, L, GLA_HEADS, GLA_KEY_DIM),
                          k_gla.astype(f32).reshape(bsz, L, GLA_HEADS, GLA_KEY_DIM),
                          v_gla.astype(f32).reshape(bsz, L, GLA_HEADS, GLA_VAL_DIM),
                          log_alpha.reshape(bsz, L, GLA_HEADS, GLA_KEY_DIM))
    o_gla = rmsnorm(o_gla, norm_head_gla) * jax.nn.silu(r_gla.astype(f32).reshape(bsz, L, GLA_HEADS, GLA_VAL_DIM))
    y_gla = o_gla.reshape(bsz, L, GLA_V).astype(h.dtype) @ w_proj_gla
    merged = jax.nn.sigmoid(gate_dn) * y_dn + jax.nn.sigmoid(gate_gla) * y_gla
    return merged @ w_out


def hier_moe(h, w_group, b_group, w_router, b_router, w1, w3, w2):
    f32 = jnp.float32
    bsz, L, d = h.shape
    T = bsz * L
    xf = h.reshape(T, d)
    xr = xf.astype(f32)
    g_logits = xr @ w_group.astype(f32) + b_group.astype(f32)
    g_sel = jnp.argmax(g_logits, axis=-1)
    g_w = jnp.take_along_axis(jax.nn.softmax(g_logits, -1), g_sel[:, None], axis=1)[:, 0]
    e_logits = (xr @ w_router.astype(f32) + b_router.astype(f32)).reshape(T, N_GROUPS, EXPERTS_PER_GROUP)
    e_in = jnp.take_along_axis(e_logits, g_sel[:, None, None], axis=1)[:, 0]
    top_v, top_i = lax.top_k(e_in, TOP_K)
    gate = jax.nn.softmax(top_v, axis=-1) * g_w[:, None]
    expert_id = g_sel[:, None] * EXPERTS_PER_GROUP + top_i
    M = T * TOP_K
    flat_e = expert_id.reshape(M).astype(jnp.int32)
    flat_tok = jnp.repeat(jnp.arange(T, dtype=jnp.int32), TOP_K)
    order = jnp.argsort(flat_e)
    se, stok, sw = flat_e[order], flat_tok[order], gate.reshape(M)[order]
    sizes = jnp.bincount(flat_e, length=N_EXPERTS)
    starts = jnp.cumsum(sizes) - sizes
    padded = (sizes + MOE_BLOCK - 1) // MOE_BLOCK * MOE_BLOCK
    pends = jnp.cumsum(padded)
    pstarts = pends - padded
    slot = pstarts[se] + (jnp.arange(M, dtype=jnp.int32) - starts[se])
    n_blocks = M // MOE_BLOCK + N_EXPERTS
    P = n_blocks * MOE_BLOCK
    slot_tok = jnp.full((P,), T, jnp.int32).at[slot].set(stok)
    x_ext = jnp.concatenate([xf, jnp.zeros((1, d), xf.dtype)], axis=0)
    xb = x_ext[slot_tok].reshape(n_blocks, MOE_BLOCK, d)
    block_expert = jnp.clip(jnp.searchsorted(pends, jnp.arange(n_blocks) * MOE_BLOCK, side='right'),
                            0, N_EXPERTS - 1)

    def expert_block(args):
        xblk, e = args
        hid = jax.nn.silu(xblk @ w1[e]) * (xblk @ w3[e])
        return hid @ w2[e]

    yb = lax.map(expert_block, (xb, block_expert)).reshape(P, d)
    y = jnp.zeros((T, d), yb.dtype).at[stok].add(yb[slot] * sw[:, None].astype(yb.dtype))
    return y.reshape(bsz, L, d).astype(h.dtype)


def setup_inputs(seed: int = 0) -> dict:
    key = jax.random.key(seed)
    ks = jax.random.split(key, 24)

    def nrm(k, shape, scale):
        return jax.random.normal(k, shape, jnp.float32) * scale

    def gain(k, shape):
        return 1.0 + 0.05 * jax.random.normal(k, shape, jnp.float32)

    dt = jnp.exp(jax.random.uniform(ks[5], (DEPTH, DN_HEADS), jnp.float32,
                                    minval=float(np.log(1e-3)), maxval=float(np.log(1e-1))))
    return {
        'x': nrm(ks[0], (BATCH, SEQ, D_MODEL), 1.0),
        'meta_tokens': nrm(ks[1], (N_META, D_MODEL), 1.0),
        'norm_mix': gain(ks[2], (DEPTH, D_MODEL)),
        'w_in': nrm(ks[3], (DEPTH, D_MODEL, D_IN), D_MODEL ** -0.5),
        'conv_dn': nrm(ks[4], (DEPTH, CONV_WIDTH, DN_QK * 2 + DN_V), CONV_WIDTH ** -0.5),
        'a_log': jnp.log(jax.random.uniform(ks[6], (DEPTH, DN_HEADS), jnp.float32, minval=1.0, maxval=16.0)),
        'dt_bias': dt + jnp.log(-jnp.expm1(-dt)),
        'norm_head_dn': gain(ks[7], (DEPTH, DN_HEAD_DIM)),
        'w_proj_dn': nrm(ks[8], (DEPTH, DN_V, D_MODEL), DN_V ** -0.5),
        'w_alpha': nrm(ks[9], (DEPTH, GLA_RANK, GLA_QK), GLA_RANK ** -0.5),
        'b_alpha': nrm(ks[10], (DEPTH, GLA_QK), 0.1),
        'norm_head_gla': gain(ks[11], (DEPTH, GLA_VAL_DIM)),
        'w_proj_gla': nrm(ks[12], (DEPTH, GLA_V, D_MODEL), GLA_V ** -0.5),
        'w_out': nrm(ks[13], (DEPTH, D_MODEL, D_MODEL), D_MODEL ** -0.5),
        'norm_ffn': gain(ks[14], (DEPTH, D_MODEL)),
        'w_group': nrm(ks[15], (DEPTH, D_MODEL, N_GROUPS), D_MODEL ** -0.5),
        'b_group': nrm(ks[16], (DEPTH, N_GROUPS), 0.01),
        'w_router': nrm(ks[17], (DEPTH, D_MODEL, N_EXPERTS), D_MODEL ** -0.5),
        'b_router': nrm(ks[18], (DEPTH, N_EXPERTS), 0.01),
        'w1': nrm(ks[19], (DEPTH, N_EXPERTS, D_MODEL, D_EXPERT), D_MODEL ** -0.5),
        'w3': nrm(ks[20], (DEPTH, N_EXPERTS, D_MODEL, D_EXPERT), D_MODEL ** -0.5),
        'w2': nrm(ks[21], (DEPTH, N_EXPERTS, D_EXPERT, D_MODEL), D_EXPERT ** -0.5),
        'norm_final': gain(ks[22], (D_MODEL,)),
    }


def reference(x, meta_tokens, norm_mix, w_in, conv_dn, a_log, dt_bias, norm_head_dn, w_proj_dn,
              w_alpha, b_alpha, norm_head_gla, w_proj_gla, w_out, norm_ffn, w_group, b_group,
              w_router, b_router, w1, w3, w2, norm_final):
    bsz = x.shape[0]
    meta = jnp.broadcast_to(meta_tokens[None].astype(x.dtype), (bsz, N_META, D_MODEL))
    h = jnp.concatenate([meta, x], axis=1)
    for i in range(DEPTH):
        h = h + hybrid_mixer(rmsnorm(h, norm_mix[i]), w_in[i], conv_dn[i], a_log[i], dt_bias[i],
                             norm_head_dn[i], w_proj_dn[i], w_alpha[i], b_alpha[i],
                             norm_head_gla[i], w_proj_gla[i], w_out[i])
        h = h + hier_moe(rmsnorm(h, norm_ffn[i]), w_group[i], b_group[i], w_router[i], b_router[i],
                         w1[i], w3[i], w2[i])
    return rmsnorm(h, norm_final)[:, N_META:]
```

```python
import functools

import jax
import jax.numpy as jnp
import numpy as np
from jax import lax
from jax.experimental import pallas as pl
from jax.experimental.pallas import tpu as pltpu

F32 = jnp.float32
BF16 = jnp.bfloat16

D_MODEL = 1024
CHUNK = 64
N_META = 16
EPS = 1e-6
DN_HEADS = 8
DN_HEAD_DIM = 128
DN_QK = DN_HEADS * DN_HEAD_DIM
DN_V = DN_HEADS * DN_HEAD_DIM
CONV_WIDTH = 4
GLA_HEADS = 4
GLA_KEY_DIM = 128
GLA_VAL_DIM = 256
GLA_QK = GLA_HEADS * GLA_KEY_DIM
GLA_V = GLA_HEADS * GLA_VAL_DIM
GLA_RANK = 16
GLA_TAU = 16.0
N_GROUPS = 8
EXPERTS_PER_GROUP = 8
N_EXPERTS = N_GROUPS * EXPERTS_PER_GROUP
D_EXPERT = 512
MOE_BLOCK = 256

LANES = 128
SMALL_W = LANES
HIST_ROWS = 8
VMEM_LIMIT = 56 * 1024 * 1024

_NT = (((1,), (1,)), ((), ()))
_TN = (((0,), (0,)), ((), ()))


def _bdot(a, b):
    return jnp.dot(a.astype(BF16), b.astype(BF16), preferred_element_type=F32)


def _bdot_nt(a, b):
    return lax.dot_general(a.astype(BF16), b.astype(BF16), _NT, preferred_element_type=F32)


def _bdot_tn(a, b):
    return lax.dot_general(a.astype(BF16), b.astype(BF16), _TN, preferred_element_type=F32)


def _sigmoid(x):
    return 1.0 / (1.0 + jnp.exp(-x))


def _silu(x):
    return x * _sigmoid(x)


def _softplus(x):
    return jnp.maximum(x, 0.0) + jnp.log(1.0 + jnp.exp(-jnp.abs(x)))


def _rms(x, eps=EPS):
    return x * lax.rsqrt(jnp.mean(x * x, axis=-1, keepdims=True) + eps)


def _tri_incl(n):
    r = lax.broadcasted_iota(jnp.int32, (n, n), 0)
    c = lax.broadcasted_iota(jnp.int32, (n, n), 1)
    return r, c


def _cumsum_rows(x):
    r, c = _tri_incl(x.shape[0])
    tri = (r >= c).astype(BF16)
    hi = x.astype(BF16)
    r1 = x - hi.astype(F32)
    mid = r1.astype(BF16)
    lo = (r1 - mid.astype(F32)).astype(BF16)
    return (jnp.dot(tri, hi, preferred_element_type=F32) + jnp.dot(tri, mid, preferred_element_type=F32)
            + jnp.dot(tri, lo, preferred_element_type=F32))


def _inproj_body(x_ref, g_ref, *refs):
    n = len(refs) // 2
    x = x_ref[...]
    xb = (_rms(x) * g_ref[...]).astype(BF16)
    for w_ref, o_ref in zip(refs[:n], refs[n:]):
        o_ref[...] = jnp.dot(xb, w_ref[...], preferred_element_type=F32).astype(o_ref.dtype)


def _in_proj(x2d, gain, weights, out_dtypes, rows):
    t = x2d.shape[0]
    in_specs = [pl.BlockSpec((rows, D_MODEL), lambda i: (i, 0)), pl.BlockSpec((1, D_MODEL), lambda i: (0, 0))]
    in_specs += [pl.BlockSpec(w.shape, lambda i: (0, 0), pipeline_mode=pl.Buffered(1)) for w in weights]
    out_specs = [pl.BlockSpec((rows, w.shape[1]), lambda i: (i, 0)) for w in weights]
    out_shape = [jax.ShapeDtypeStruct((t, w.shape[1]), dt) for w, dt in zip(weights, out_dtypes)]
    return pl.pallas_call(
        _inproj_body, grid=(t // rows,), in_specs=in_specs, out_specs=out_specs, out_shape=out_shape,
        compiler_params=pltpu.CompilerParams(dimension_semantics=("parallel",), vmem_limit_bytes=VMEM_LIMIT),
        name="in_proj",
    )(x2d, gain, *weights)


def _unit_lower_inverse(a):
    n = a.shape[0]
    r, c = _tri_incl(n)
    eye = (r == c).astype(F32)
    t = eye - jnp.where((r >> 1) == (c >> 1), a, 0.0)
    m = 2
    while m < n:
        sh = m.bit_length() - 1
        am = jnp.where(((r >> (sh + 1)) == (c >> (sh + 1))) & ((r >> sh) != (c >> sh)), a, 0.0)
        t = t - _bdot(_bdot(t, am), t)
        m *= 2
    return t


def _dn_body(qkv_ref, sm_ref, z_ref, cw_ref, hist_ref, s0_ref, alog_ref, dtb_ref, nh_ref,
             o_ref, sfin_ref, buf, s_scr):
    c_id = pl.program_id(1)

    @pl.when(c_id == 0)
    def _():
        buf[0:HIST_ROWS, :] = hist_ref[...]
        s_scr[...] = s0_ref[...]

    buf[HIST_ROWS:HIST_ROWS + CHUNK, :] = qkv_ref[...].astype(F32)

    def conv_silu(col):
        acc = None
        for j in range(CONV_WIDTH):
            lo = HIST_ROWS - (CONV_WIDTH - 1) + j
            term = buf[lo:lo + CHUNK, col:col + DN_HEAD_DIM] * cw_ref[j:j + 1, col:col + DN_HEAD_DIM]
            acc = term if acc is None else acc + term
        return _silu(acc)

    sm = sm_ref[...]
    g_all = -jnp.exp(alog_ref[...]) * _softplus(sm + dtb_ref[...])
    beta_all = _sigmoid(sm)
    gc_all = _cumsum_rows(g_all)
    gc_sq = jnp.concatenate([gc_all, jnp.zeros_like(gc_all)], axis=0)
    gc_t = gc_sq.T

    r, c = _tri_incl(CHUNK)
    causal = r >= c
    strict = r > c
    scale = DN_HEAD_DIM ** -0.5

    for h in range(DN_HEADS):
        col = h * DN_HEAD_DIM
        q = conv_silu(col)
        k = conv_silu(DN_QK + col)
        v = conv_silu(2 * DN_QK + col)
        q = q * lax.rsqrt(jnp.sum(q * q, axis=-1, keepdims=True) + EPS) * scale
        k = k * lax.rsqrt(jnp.sum(k * k, axis=-1, keepdims=True) + EPS)
        gcol = gc_all[:, h:h + 1]
        grow = gc_t[h:h + 1, 0:CHUNK]
        bcol = beta_all[:, DN_HEADS + h:DN_HEADS + h + 1]
        glast = gc_all[CHUNK - 1:CHUNK, h:h + 1]
        decay = jnp.where(causal, jnp.exp(jnp.where(causal, gcol - grow, 0.0)), 0.0)
        eg = jnp.exp(gcol)
        kb = k.astype(BF16)
        kk = lax.dot_general(kb, kb, _NT, preferred_element_type=F32)
        a_mat = jnp.where(strict, bcol * kk * decay, 0.0)
        t_inv = _unit_lower_inverse(a_mat)
        rhs = jnp.concatenate([v * bcol, k * (bcol * eg)], axis=1)
        sol = _bdot(t_inv, rhs)
        u = sol[:, :DN_HEAD_DIM]
        w = sol[:, DN_HEAD_DIM:]
        qk = jnp.where(causal, lax.dot_general(q.astype(BF16), kb, _NT, preferred_element_type=F32) * decay, 0.0)
        s_h = s_scr[h]
        s_b = s_h.astype(BF16)
        v_new = u - _bdot(w, s_b)
        o = _bdot(q * eg, s_b) + _bdot(qk, v_new)
        k_dec = k * jnp.exp(glast - gcol)
        s_scr[h] = s_h * jnp.exp(glast) + _bdot_tn(k_dec, v_new)
        zed = z_ref[:, col:col + DN_HEAD_DIM].astype(F32)
        o_ref[:, col:col + DN_HEAD_DIM] = (_rms(o) * nh_ref[...] * _silu(zed)).astype(o_ref.dtype)

    buf[0:HIST_ROWS, :] = buf[CHUNK:CHUNK + HIST_ROWS, :]

    @pl.when(c_id == pl.num_programs(1) - 1)
    def _():
        sfin_ref[...] = s_scr[...]


def _dn_chunk(qkv, small, z, conv_w, hist, s0, alog, dtb, nh):
    b, l, _ = qkv.shape
    nc = l // CHUNK
    const2 = lambda bi, ci: (0, 0)
    return pl.pallas_call(
        _dn_body, grid=(b, nc),
        in_specs=[
            pl.BlockSpec((None, CHUNK, 3 * DN_QK), lambda bi, ci: (bi, ci, 0)),
            pl.BlockSpec((None, CHUNK, SMALL_W), lambda bi, ci: (bi, ci, 0)),
            pl.BlockSpec((None, CHUNK, DN_V), lambda bi, ci: (bi, ci, 0)),
            pl.BlockSpec(conv_w.shape, const2),
            pl.BlockSpec(hist.shape, const2),
            pl.BlockSpec(s0.shape, lambda bi, ci: (0, 0, 0)),
            pl.BlockSpec(alog.shape, const2),
            pl.BlockSpec(dtb.shape, const2),
            pl.BlockSpec(nh.shape, const2),
        ],
        out_specs=[
            pl.BlockSpec((None, CHUNK, DN_V), lambda bi, ci: (bi, ci, 0)),
            pl.BlockSpec((None, DN_HEADS, DN_HEAD_DIM, DN_HEAD_DIM), lambda bi, ci: (bi, 0, 0, 0)),
        ],
        out_shape=[
            jax.ShapeDtypeStruct((b, l, DN_V), BF16),
            jax.ShapeDtypeStruct((b, DN_HEADS, DN_HEAD_DIM, DN_HEAD_DIM), F32),
        ],
        scratch_shapes=[
            pltpu.VMEM((HIST_ROWS + CHUNK, 3 * DN_QK), F32),
            pltpu.VMEM((DN_HEADS, DN_HEAD_DIM, DN_HEAD_DIM), F32),
        ],
        compiler_params=pltpu.CompilerParams(dimension_semantics=("parallel", "arbitrary"),
                                             vmem_limit_bytes=VMEM_LIMIT),
        name="dn_chunk",
    )(qkv, small, z, conv_w, hist, s0, alog, dtb, nh)


def _gla_body(qk_ref, v_ref, r_ref, sm_ref, wa_ref, ba_ref, nh_ref, s0_ref, o_ref, sfin_ref, s_scr):
    c_id = pl.program_id(1)

    @pl.when(c_id == 0)
    def _():
        s_scr[...] = s0_ref[...]

    la = jnp.dot(sm_ref[...].astype(BF16), wa_ref[...], preferred_element_type=F32) + ba_ref[...]
    log_alpha = (jnp.minimum(la, 0.0) - jnp.log(1.0 + jnp.exp(-jnp.abs(la)))) * (1.0 / GLA_TAU)
    b_all = _cumsum_rows(log_alpha)

    r, c = _tri_incl(CHUNK)
    causal = r >= c
    scale = GLA_KEY_DIM ** -0.5
    mid = CHUNK // 2 - 1

    for h in range(GLA_HEADS):
        kc = h * GLA_KEY_DIM
        vc = h * GLA_VAL_DIM
        q = qk_ref[:, kc:kc + GLA_KEY_DIM].astype(F32) * scale
        k = qk_ref[:, GLA_QK + kc:GLA_QK + kc + GLA_KEY_DIM].astype(F32)
        v = v_ref[:, vc:vc + GLA_VAL_DIM]
        bh = b_all[:, kc:kc + GLA_KEY_DIM]
        bmid = bh[mid:mid + 1, :]
        blast = bh[CHUNK - 1:CHUNK, :]
        att = jnp.where(causal, _bdot_nt(q * jnp.exp(bh - bmid), k * jnp.exp(bmid - bh)), 0.0)
        st = s_scr[h]
        o = _bdot(att, v) + _bdot_nt(q * jnp.exp(bh), st)
        s_scr[h] = st * jnp.exp(blast) + _bdot_tn(v, k * jnp.exp(blast - bh))
        gate = _silu(r_ref[:, vc:vc + GLA_VAL_DIM].astype(F32))
        o_ref[:, vc:vc + GLA_VAL_DIM] = (_rms(o) * nh_ref[...] * gate).astype(o_ref.dtype)

    @pl.when(c_id == pl.num_programs(1) - 1)
    def _():
        sfin_ref[...] = s_scr[...]


def _gla_chunk(qk, v, rr, small, wa, ba, nh, s0):
    b, l, _ = qk.shape
    nc = l // CHUNK
    const2 = lambda bi, ci: (0, 0)
    blk = lambda w: pl.BlockSpec((None, CHUNK, w), lambda bi, ci: (bi, ci, 0))
    return pl.pallas_call(
        _gla_body, grid=(b, nc),
        in_specs=[
            blk(2 * GLA_QK), blk(GLA_V), blk(GLA_V), blk(SMALL_W),
            pl.BlockSpec(wa.shape, const2), pl.BlockSpec(ba.shape, const2), pl.BlockSpec(nh.shape, const2),
            pl.BlockSpec(s0.shape, lambda bi, ci: (0, 0, 0)),
        ],
        out_specs=[
            blk(GLA_V),
            pl.BlockSpec((None, GLA_HEADS, GLA_VAL_DIM, GLA_KEY_DIM), lambda bi, ci: (bi, 0, 0, 0)),
        ],
        out_shape=[
            jax.ShapeDtypeStruct((b, l, GLA_V), BF16),
            jax.ShapeDtypeStruct((b, GLA_HEADS, GLA_VAL_DIM, GLA_KEY_DIM), F32),
        ],
        scratch_shapes=[pltpu.VMEM((GLA_HEADS, GLA_VAL_DIM, GLA_KEY_DIM), F32)],
        compiler_params=pltpu.CompilerParams(dimension_semantics=("parallel", "arbitrary"),
                                             vmem_limit_bytes=VMEM_LIMIT),
        name="gla_chunk",
    )(qk, v, rr, small, wa, ba, nh, s0)


def _outproj_body(odn_ref, ogla_ref, gates_ref, x_ref, wd_ref, wg_ref, wo_ref, gn_ref, wr_ref, br_ref,
                  h2_ref, hn_ref, lg_ref):
    y_dn = jnp.dot(odn_ref[...], wd_ref[...], preferred_element_type=F32)
    y_gla = jnp.dot(ogla_ref[...], wg_ref[...], preferred_element_type=F32)
    gd = _sigmoid(gates_ref[:, 0:D_MODEL].astype(F32))
    gg = _sigmoid(gates_ref[:, D_MODEL:2 * D_MODEL].astype(F32))
    merged = gd * y_dn + gg * y_gla
    h2 = x_ref[...] + _bdot(merged, wo_ref[...])
    h2_ref[...] = h2
    hn = _rms(h2) * gn_ref[...]
    hn_ref[...] = hn
    lg_ref[...] = lax.dot_general(wr_ref[...], hn, _NT, preferred_element_type=F32,
                                  precision=lax.Precision.HIGHEST) + br_ref[...]


def _out_proj(o_dn, o_gla, gates, x2d, wd, wg, wo, gn, wr_t, br, rows):
    t = x2d.shape[0]
    row_blk = lambda w: pl.BlockSpec((rows, w), lambda i: (i, 0))
    const = lambda a: pl.BlockSpec(a.shape, lambda i: (0, 0))
    return pl.pallas_call(
        _outproj_body, grid=(t // rows,),
        in_specs=[row_blk(DN_V), row_blk(GLA_V), row_blk(2 * D_MODEL), row_blk(D_MODEL),
                  const(wd), const(wg), const(wo), const(gn), const(wr_t), const(br)],
        out_specs=[row_blk(D_MODEL), row_blk(D_MODEL), pl.BlockSpec((LANES, rows), lambda i: (0, i))],
        out_shape=[jax.ShapeDtypeStruct((t, D_MODEL), F32), jax.ShapeDtypeStruct((t, D_MODEL), F32),
                   jax.ShapeDtypeStruct((LANES, t), F32)],
        compiler_params=pltpu.CompilerParams(dimension_semantics=("parallel",), vmem_limit_bytes=VMEM_LIMIT),
        name="out_proj",
    )(o_dn, o_gla, gates, x2d, wd, wg, wo, gn, wr_t, br)


ROUTE_SUB = 256


def _route_body(lg_ref, idx_ref, gate_ref, cnt_ref, carry):
    step = pl.program_id(0)

    @pl.when(step == 0)
    def _():
        carry[...] = jnp.zeros_like(carry)

    tt = lg_ref.shape[1]
    gl = lg_ref[0:N_GROUPS, :]
    gmax = jnp.max(gl, axis=0, keepdims=True)
    rid8 = lax.broadcasted_iota(jnp.int32, (N_GROUPS, tt), 0)
    gsel = jnp.min(jnp.where(gl == gmax, rid8, N_GROUPS), axis=0, keepdims=True)
    gw = 1.0 / jnp.sum(jnp.exp(gl - gmax), axis=0, keepdims=True)
    el = lg_ref[N_GROUPS:N_GROUPS + N_EXPERTS, :]
    rid = lax.broadcasted_iota(jnp.int32, (N_EXPERTS, tt), 0)
    neg = jnp.float32(-jnp.inf)
    ein = jnp.where((rid >> 3) == gsel, el, neg)
    t1 = jnp.max(ein, axis=0, keepdims=True)
    i1 = jnp.min(jnp.where(ein == t1, rid, N_EXPERTS), axis=0, keepdims=True)
    ein2 = jnp.where(rid == i1, neg, ein)
    t2 = jnp.max(ein2, axis=0, keepdims=True)
    i2 = jnp.min(jnp.where(ein2 == t2, rid, N_EXPERTS), axis=0, keepdims=True)
    e21 = jnp.exp(t2 - t1)
    den = 1.0 / (1.0 + e21)
    sel1 = rid == i1
    sel2 = rid == i2
    onehot = jnp.where(sel1 | sel2, 1.0, 0.0)

    ur, uc = _tri_incl(ROUTE_SUB)
    upper = (ur <= uc).astype(BF16)
    run = carry[...]
    r1_parts, r2_parts = [], []
    for s in range(tt // ROUTE_SUB):
        sl = slice(s * ROUTE_SUB, (s + 1) * ROUTE_SUB)
        oh = onehot[:, sl]
        incl = jnp.dot(oh.astype(BF16), upper, preferred_element_type=F32) + run
        excl = incl - oh
        r1_parts.append(jnp.sum(jnp.where(sel1[:, sl], excl, 0.0), axis=0, keepdims=True))
        r2_parts.append(jnp.sum(jnp.where(sel2[:, sl], excl, 0.0), axis=0, keepdims=True))
        run = jnp.broadcast_to(incl[:, ROUTE_SUB - 1:ROUTE_SUB], run.shape)
    carry[...] = run
    r1 = jnp.concatenate(r1_parts, axis=1) if len(r1_parts) > 1 else r1_parts[0]
    r2 = jnp.concatenate(r2_parts, axis=1) if len(r2_parts) > 1 else r2_parts[0]

    idx_ref[...] = jnp.zeros_like(idx_ref)
    idx_ref[0:1, :] = i1
    idx_ref[1:2, :] = i2
    idx_ref[2:3, :] = r1.astype(jnp.int32)
    idx_ref[3:4, :] = r2.astype(jnp.int32)
    gate_ref[...] = jnp.zeros_like(gate_ref)
    gate_ref[0:1, :] = den * gw
    gate_ref[1:2, :] = e21 * den * gw
    cnt_ref[...] = run[:, 0:LANES]


def _route(logits_t, lanes):
    t = logits_t.shape[1]
    return pl.pallas_call(
        _route_body, grid=(t // lanes,),
        in_specs=[pl.BlockSpec((LANES, lanes), lambda i: (0, i))],
        out_specs=[pl.BlockSpec((8, lanes), lambda i: (0, i)), pl.BlockSpec((8, lanes), lambda i: (0, i)),
                   pl.BlockSpec((N_EXPERTS, LANES), lambda i: (0, 0))],
        out_shape=[jax.ShapeDtypeStruct((8, t), jnp.int32), jax.ShapeDtypeStruct((8, t), F32),
                   jax.ShapeDtypeStruct((N_EXPERTS, LANES), F32)],
        scratch_shapes=[pltpu.VMEM((N_EXPERTS, ROUTE_SUB), F32)],
        compiler_params=pltpu.CompilerParams(dimension_semantics=("arbitrary",)),
        name="route",
    )(logits_t)


def _row_copy(src_ref, src_row, dst_ref, dst_row, sem):
    return pltpu.make_async_copy(src_ref.at[pl.ds(src_row, 1)], dst_ref.at[pl.ds(dst_row, 1)], sem)


def _dispatch_body(pstart_ref, idx_ref, hn_ref, xs_in_ref, xs_ref, sem):
    del xs_in_ref
    n_tok = hn_ref.shape[0]

    def issue(t, carry):
        for k in range(2):
            slot = pstart_ref[idx_ref[k, t]] + idx_ref[2 + k, t]
            _row_copy(hn_ref, t, xs_ref, slot, sem).start()
        return carry

    lax.fori_loop(0, n_tok, issue, 0)

    def drain(t, carry):
        for k in range(2):
            _row_copy(hn_ref, 0, xs_ref, 0, sem).wait()
        return carry

    lax.fori_loop(0, n_tok, drain, 0)


def _dispatch(pstart, idx, hn, xs_zero, tokens):
    t = hn.shape[0]
    grid_spec = pltpu.PrefetchScalarGridSpec(
        num_scalar_prefetch=1, grid=(t // tokens,),
        in_specs=[pl.BlockSpec((8, tokens), lambda i, ps: (0, i), memory_space=pltpu.SMEM),
                  pl.BlockSpec((tokens, D_MODEL), lambda i, ps: (i, 0)),
                  pl.BlockSpec(memory_space=pl.ANY)],
        out_specs=pl.BlockSpec(memory_space=pl.ANY),
        scratch_shapes=[pltpu.SemaphoreType.DMA(())],
    )
    return pl.pallas_call(
        _dispatch_body, grid_spec=grid_spec,
        out_shape=jax.ShapeDtypeStruct(xs_zero.shape, xs_zero.dtype),
        input_output_aliases={3: 0},
        compiler_params=pltpu.CompilerParams(dimension_semantics=("arbitrary",)),
        name="dispatch",
    )(pstart, idx, hn, xs_zero)


def _experts_body(be_ref, nused_ref, xs_ref, w1_ref, w3_ref, w2_ref, yb_ref):
    used = pl.program_id(0) < nused_ref[0]

    @pl.when(used)
    def _():
        xb = xs_ref[...].astype(BF16)
        h1 = jnp.dot(xb, w1_ref[...], preferred_element_type=F32)
        h3 = jnp.dot(xb, w3_ref[...], preferred_element_type=F32)
        hid = (_silu(h1) * h3).astype(BF16)
        yb_ref[...] = jnp.dot(hid, w2_ref[...], preferred_element_type=F32)

    @pl.when(jnp.logical_not(used))
    def _():
        yb_ref[...] = jnp.zeros_like(yb_ref)


def _experts(block_expert, n_used, xs, w1, w3, w2):
    p = xs.shape[0]
    nb = p // MOE_BLOCK
    row_map = lambda j, be, nu: (jnp.minimum(j, nu[0] - 1), 0)
    w_map = lambda j, be, nu: (be[j], 0, 0)
    grid_spec = pltpu.PrefetchScalarGridSpec(
        num_scalar_prefetch=2, grid=(nb,),
        in_specs=[pl.BlockSpec((MOE_BLOCK, D_MODEL), row_map),
                  pl.BlockSpec((None, D_MODEL, D_EXPERT), w_map),
                  pl.BlockSpec((None, D_MODEL, D_EXPERT), w_map),
                  pl.BlockSpec((None, D_EXPERT, D_MODEL), w_map)],
        out_specs=pl.BlockSpec((MOE_BLOCK, D_MODEL), lambda j, be, nu: (j, 0)),
    )
    return pl.pallas_call(
        _experts_body, grid_spec=grid_spec,
        out_shape=jax.ShapeDtypeStruct((p, D_MODEL), F32),
        compiler_params=pltpu.CompilerParams(dimension_semantics=("arbitrary",), vmem_limit_bytes=VMEM_LIMIT),
        name="experts",
    )(block_expert, n_used, xs, w1, w3, w2)


def _combine_body(pstart_ref, idx_ref, gate_ref, h2_ref, gf_ref, yb_ref, o_ref, ybuf, sem):
    n_tok = h2_ref.shape[0]

    def issue(t, carry):
        for k in range(2):
            slot = pstart_ref[idx_ref[k, t]] + idx_ref[2 + k, t]
            pltpu.make_async_copy(yb_ref.at[pl.ds(slot, 1)], ybuf.at[k, pl.ds(t, 1)], sem).start()
        return carry

    lax.fori_loop(0, n_tok, issue, 0)

    def drain(t, carry):
        for k in range(2):
            pltpu.make_async_copy(yb_ref.at[pl.ds(0, 1)], ybuf.at[k, pl.ds(0, 1)], sem).wait()
        return carry

    lax.fori_loop(0, n_tok, drain, 0)
    y = h2_ref[...] + gate_ref[:, 0:1] * ybuf[0] + gate_ref[:, 1:2] * ybuf[1]
    o_ref[...] = _rms(y) * gf_ref[...]


def _combine(pstart, idx, gate_t, h2, gf, yb, tokens):
    t = h2.shape[0]
    grid_spec = pltpu.PrefetchScalarGridSpec(
        num_scalar_prefetch=1, grid=(t // tokens,),
        in_specs=[pl.BlockSpec((8, tokens), lambda i, ps: (0, i), memory_space=pltpu.SMEM),
                  pl.BlockSpec((tokens, 8), lambda i, ps: (i, 0)),
                  pl.BlockSpec((tokens, D_MODEL), lambda i, ps: (i, 0)),
                  pl.BlockSpec((1, D_MODEL), lambda i, ps: (0, 0)),
                  pl.BlockSpec(memory_space=pl.ANY)],
        out_specs=pl.BlockSpec((tokens, D_MODEL), lambda i, ps: (i, 0)),
        scratch_shapes=[pltpu.VMEM((2, tokens, D_MODEL), F32), pltpu.SemaphoreType.DMA(())],
    )
    return pl.pallas_call(
        _combine_body, grid_spec=grid_spec,
        out_shape=jax.ShapeDtypeStruct((t, D_MODEL), F32),
        compiler_params=pltpu.CompilerParams(dimension_semantics=("arbitrary",)),
        name="combine",
    )(pstart, idx, gate_t, h2, gf, yb)


def _pad_lanes(v, width=LANES):
    v = v.reshape(1, -1).astype(F32)
    return jnp.pad(v, ((0, 0), (0, width - v.shape[1])))


def _largest_tile(n, cap):
    t = cap
    while n % t:
        t //= 2
    return t


def kernel(x, meta_tokens, norm_mix, w_in, conv_dn, a_log, dt_bias, norm_head_dn, w_proj_dn, w_alpha, b_alpha,
           norm_head_gla, w_proj_gla, w_out, norm_ffn, w_group, b_group, w_router, b_router, w1, w3, w2, norm_final):
    assert norm_mix.shape[0] == 1, "single-layer block"
    bsz, seq, d = x.shape
    assert d == D_MODEL and seq % CHUNK == 0
    t = bsz * seq

    wi = w_in[0]
    offs = np.cumsum([0, DN_QK, DN_QK, DN_V, DN_V, DN_HEADS, DN_HEADS, GLA_QK, GLA_QK, GLA_V, GLA_V, GLA_RANK,
                      D_MODEL, D_MODEL]).tolist()
    seg = lambda i, j: wi[:, offs[i]:offs[j]]
    w_small = jnp.concatenate([seg(4, 6), seg(10, 11)], axis=1)
    w_small = jnp.pad(w_small, ((0, 0), (0, SMALL_W - w_small.shape[1])))
    in_weights = [seg(0, 3), seg(3, 4), w_small, seg(6, 8), seg(8, 9), seg(9, 10), seg(11, 13)]
    in_weights = [w.astype(BF16) for w in in_weights]
    in_dtypes = [BF16, BF16, F32, BF16, BF16, BF16, BF16]
    g_mix = norm_mix[0].reshape(1, D_MODEL)

    conv_w = jnp.pad(conv_dn[0], ((0, 8 - CONV_WIDTH), (0, 0)))
    alog = _pad_lanes(a_log[0])
    dtb = _pad_lanes(dt_bias[0])
    nh_dn = norm_head_dn[0].reshape(1, DN_HEAD_DIM)
    lr_off = 2 * DN_HEADS
    wa = jnp.zeros((SMALL_W, GLA_QK), F32).at[lr_off:lr_off + GLA_RANK].set(w_alpha[0]).astype(BF16)
    ba = b_alpha[0].reshape(1, GLA_QK)
    nh_gla = norm_head_gla[0].reshape(1, GLA_VAL_DIM)

    def mixers(tokens2d, nb, hist, s0_dn, s0_gla, rows):
        qkv, z, small, qk_g, v_g, r_g, gates = _in_proj(tokens2d, g_mix, in_weights, in_dtypes, rows)
        r3 = lambda a: a.reshape(nb, -1, a.shape[-1])
        o_dn, s_dn = _dn_chunk(r3(qkv), r3(small), r3(z), conv_w, hist, s0_dn, alog, dtb, nh_dn)
        o_gla, s_gla = _gla_chunk(r3(qk_g), r3(v_g), r3(r_g), r3(small), wa, ba, nh_gla, s0_gla)
        return qkv, o_dn, o_gla, gates, s_dn, s_gla

    meta_rows = jnp.pad(meta_tokens.astype(F32), ((CHUNK - N_META, 0), (0, 0)))
    zero_hist = jnp.zeros((HIST_ROWS, 3 * DN_QK), F32)
    zero_dn = jnp.zeros((DN_HEADS, DN_HEAD_DIM, DN_HEAD_DIM), F32)
    zero_gla = jnp.zeros((GLA_HEADS, GLA_VAL_DIM, GLA_KEY_DIM), F32)
    qkv_m, _, _, _, s_dn_m, s_gla_m = mixers(meta_rows, 1, zero_hist, zero_dn, zero_gla, CHUNK)
    hist = qkv_m[CHUNK - HIST_ROWS:].astype(F32)

    x2d = x.reshape(t, d)
    rows = _largest_tile(t, 512)
    _, o_dn, o_gla, gates, _, _ = mixers(x2d, bsz, hist, s_dn_m[0], s_gla_m[0], rows)

    wr_t = jnp.concatenate([w_group[0], w_router[0]], axis=1).T
    wr_t = jnp.pad(wr_t, ((0, LANES - wr_t.shape[0]), (0, 0)))
    br = jnp.pad(jnp.concatenate([b_group[0], b_router[0]]), (0, LANES - N_GROUPS - N_EXPERTS)).reshape(LANES, 1)
    h2, hn, logits_t = _out_proj(
        o_dn.reshape(t, DN_V), o_gla.reshape(t, GLA_V), gates, x2d,
        w_proj_dn[0].astype(BF16), w_proj_gla[0].astype(BF16), w_out[0].astype(BF16),
        norm_ffn[0].reshape(1, D_MODEL), wr_t, br, rows)

    idx, gate, cnt = _route(logits_t, _largest_tile(t, 512))

    sizes = cnt[:, 0].astype(jnp.int32)
    padded = (sizes + MOE_BLOCK - 1) // MOE_BLOCK * MOE_BLOCK
    pends = jnp.cumsum(padded)
    pstart = (pends - padded).astype(jnp.int32)
    n_blocks = (2 * t) // MOE_BLOCK + N_EXPERTS
    n_used = (pends[-1:] // MOE_BLOCK).astype(jnp.int32)
    block_expert = jnp.clip(jnp.searchsorted(pends, jnp.arange(n_blocks, dtype=jnp.int32) * MOE_BLOCK, side="right"),
                            0, N_EXPERTS - 1).astype(jnp.int32)

    tok_tile = _largest_tile(t, 256)
    xs = _dispatch(pstart, idx, hn, jnp.zeros((n_blocks * MOE_BLOCK, D_MODEL), F32), tok_tile)
    yb = _experts(block_expert, n_used, xs, w1[0].astype(BF16), w3[0].astype(BF16), w2[0].astype(BF16))
    out = _combine(pstart, idx, gate.T, h2, norm_final.reshape(1, D_MODEL), yb, tok_tile)
    return out.reshape(bsz, seq, d)
```

```python
import functools

import jax
import jax.numpy as jnp
import numpy as np
from jax import lax
from jax.experimental import pallas as pl
from jax.experimental.pallas import tpu as pltpu

F32 = jnp.float32
BF16 = jnp.bfloat16

D_MODEL = 1024
CHUNK = 64
N_META = 16
EPS = 1e-6
DN_HEADS = 8
DN_HEAD_DIM = 128
DN_QK = DN_HEADS * DN_HEAD_DIM
DN_V = DN_HEADS * DN_HEAD_DIM
CONV_WIDTH = 4
GLA_HEADS = 4
GLA_KEY_DIM = 128
GLA_VAL_DIM = 256
GLA_QK = GLA_HEADS * GLA_KEY_DIM
GLA_V = GLA_HEADS * GLA_VAL_DIM
GLA_RANK = 16
GLA_TAU = 16.0
N_GROUPS = 8
EXPERTS_PER_GROUP = 8
N_EXPERTS = N_GROUPS * EXPERTS_PER_GROUP
D_EXPERT = 512
MOE_BLOCK = 256

LANES = 128
SMALL_W = LANES
HIST_ROWS = 8
VMEM_LIMIT = 56 * 1024 * 1024

_NT = (((1,), (1,)), ((), ()))
_TN = (((0,), (0,)), ((), ()))


def _bdot(a, b):
    return jnp.dot(a.astype(BF16), b.astype(BF16), preferred_element_type=F32)


def _bdot_nt(a, b):
    return lax.dot_general(a.astype(BF16), b.astype(BF16), _NT, preferred_element_type=F32)


def _bdot_tn(a, b):
    return lax.dot_general(a.astype(BF16), b.astype(BF16), _TN, preferred_element_type=F32)


def _sigmoid(x):
    return 1.0 / (1.0 + jnp.exp(-x))


def _silu(x):
    return x * _sigmoid(x)


def _softplus(x):
    return jnp.maximum(x, 0.0) + jnp.log(1.0 + jnp.exp(-jnp.abs(x)))


def _rms(x, eps=EPS):
    return x * lax.rsqrt(jnp.mean(x * x, axis=-1, keepdims=True) + eps)


def _tri_incl(n):
    r = lax.broadcasted_iota(jnp.int32, (n, n), 0)
    c = lax.broadcasted_iota(jnp.int32, (n, n), 1)
    return r, c


def _cumsum_rows(x):
    r, c = _tri_incl(x.shape[0])
    tri = (r >= c).astype(BF16)
    hi = x.astype(BF16)
    r1 = x - hi.astype(F32)
    mid = r1.astype(BF16)
    lo = (r1 - mid.astype(F32)).astype(BF16)
    return (jnp.dot(tri, hi, preferred_element_type=F32) + jnp.dot(tri, mid, preferred_element_type=F32)
            + jnp.dot(tri, lo, preferred_element_type=F32))


IN_GROUPS = ((3 * DN_QK, BF16), (DN_V, BF16), (2 * GLA_QK, BF16), (GLA_V, BF16), (GLA_V, BF16),
             (2 * D_MODEL, BF16), (SMALL_W, F32))


def _inproj_body(x_ref, g_ref, w_ref, *o_refs):
    x = x_ref[...]
    xb = (_rms(x) * g_ref[...]).astype(BF16)
    col = 0
    for o_ref in o_refs:
        width = o_ref.shape[1]
        o_ref[...] = jnp.dot(xb, w_ref[:, col:col + width], preferred_element_type=F32).astype(o_ref.dtype)
        col += width


def _in_proj(x2d, gain, w_all, rows):
    t = x2d.shape[0]
    in_specs = [pl.BlockSpec((rows, D_MODEL), lambda i: (i, 0)), pl.BlockSpec((1, D_MODEL), lambda i: (0, 0)),
                pl.BlockSpec(w_all.shape, lambda i: (0, 0), pipeline_mode=pl.Buffered(1))]
    out_specs = [pl.BlockSpec((rows, w), lambda i: (i, 0)) for w, _ in IN_GROUPS]
    out_shape = [jax.ShapeDtypeStruct((t, w), dt) for w, dt in IN_GROUPS]
    return pl.pallas_call(
        _inproj_body, grid=(t // rows,), in_specs=in_specs, out_specs=out_specs, out_shape=out_shape,
        compiler_params=pltpu.CompilerParams(dimension_semantics=("parallel",), vmem_limit_bytes=VMEM_LIMIT),
        name="in_proj",
    )(x2d, gain, w_all)


def _unit_lower_inverse(mats):
    n = mats[0].shape[0]
    r, c = _tri_incl(n)
    eye = (r == c).astype(F32)
    pair = (r >> 1) == (c >> 1)
    ts = [eye - jnp.where(pair, a, 0.0) for a in mats]
    m = 2
    while m < n:
        sh = m.bit_length() - 1
        keep = ((r >> (sh + 1)) == (c >> (sh + 1))) & ((r >> sh) != (c >> sh))
        ams = [jnp.where(keep, a, 0.0).astype(BF16) for a in mats]
        tbs = [t.astype(BF16) for t in ts]
        xs = [jnp.dot(tb, am, preferred_element_type=F32) for tb, am in zip(tbs, ams)]
        ys = [jnp.dot(x.astype(BF16), tb, preferred_element_type=F32) for x, tb in zip(xs, tbs)]
        ts = [t - y for t, y in zip(ts, ys)]
        m *= 2
    return ts


def _dn_body(qkv_ref, sm_ref, z_ref, cw_ref, hist_ref, s0_ref, alog_ref, dtb_ref, nh_ref,
             o_ref, sfin_ref, buf, s_scr):
    c_id = pl.program_id(1)

    @pl.when(c_id == 0)
    def _():
        buf[0:HIST_ROWS, :] = hist_ref[...]
        s_scr[...] = s0_ref[...]

    buf[HIST_ROWS:HIST_ROWS + CHUNK, :] = qkv_ref[...].astype(F32)

    def conv_silu(col):
        acc = None
        for j in range(CONV_WIDTH):
            lo = HIST_ROWS - (CONV_WIDTH - 1) + j
            term = buf[lo:lo + CHUNK, col:col + DN_HEAD_DIM] * cw_ref[j:j + 1, col:col + DN_HEAD_DIM]
            acc = term if acc is None else acc + term
        return _silu(acc)

    sm = sm_ref[...]
    g_all = -jnp.exp(alog_ref[...]) * _softplus(sm + dtb_ref[...])
    beta_all = _sigmoid(sm)
    gc_all = _cumsum_rows(g_all)
    gc_sq = jnp.concatenate([gc_all, jnp.zeros_like(gc_all)], axis=0)
    gc_t = gc_sq.T

    r, c = _tri_incl(CHUNK)
    causal = r >= c
    strict = r > c
    scale = DN_HEAD_DIM ** -0.5

    heads = range(DN_HEADS)
    q = [conv_silu(h * DN_HEAD_DIM) for h in heads]
    k = [conv_silu(DN_QK + h * DN_HEAD_DIM) for h in heads]
    v = [conv_silu(2 * DN_QK + h * DN_HEAD_DIM) for h in heads]
    q = [x * (lax.rsqrt(jnp.sum(x * x, axis=-1, keepdims=True) + EPS) * scale) for x in q]
    k = [x * lax.rsqrt(jnp.sum(x * x, axis=-1, keepdims=True) + EPS) for x in k]
    gcol = [gc_all[:, h:h + 1] for h in heads]
    grow = [gc_t[h:h + 1, 0:CHUNK] for h in heads]
    bcol = [beta_all[:, DN_HEADS + h:DN_HEADS + h + 1] for h in heads]
    glast = [gc_all[CHUNK - 1:CHUNK, h:h + 1] for h in heads]
    decay = [jnp.where(causal, jnp.exp(jnp.where(causal, gcol[h] - grow[h], 0.0)), 0.0) for h in heads]
    eg = [jnp.exp(gcol[h]) for h in heads]
    kb = [x.astype(BF16) for x in k]
    qb = [x.astype(BF16) for x in q]
    kk = [lax.dot_general(kb[h], kb[h], _NT, preferred_element_type=F32) for h in heads]
    qk = [lax.dot_general(qb[h], kb[h], _NT, preferred_element_type=F32) for h in heads]
    a_mat = [jnp.where(strict, bcol[h] * kk[h] * decay[h], 0.0) for h in heads]
    qk = [jnp.where(causal, qk[h] * decay[h], 0.0).astype(BF16) for h in heads]
    t_inv = _unit_lower_inverse(a_mat)
    rhs = [jnp.concatenate([v[h] * bcol[h], k[h] * (bcol[h] * eg[h])], axis=1).astype(BF16) for h in heads]
    sol = [jnp.dot(t_inv[h].astype(BF16), rhs[h], preferred_element_type=F32) for h in heads]
    s_old = [s_scr[h] for h in heads]
    s_b = [s.astype(BF16) for s in s_old]
    qe = [(q[h] * eg[h]).astype(BF16) for h in heads]
    k_dec = [(k[h] * jnp.exp(glast[h] - gcol[h])).astype(BF16) for h in heads]
    ws = [jnp.dot(sol[h][:, DN_HEAD_DIM:].astype(BF16), s_b[h], preferred_element_type=F32) for h in heads]
    o_inter = [jnp.dot(qe[h], s_b[h], preferred_element_type=F32) for h in heads]
    v_new = [(sol[h][:, :DN_HEAD_DIM] - ws[h]).astype(BF16) for h in heads]
    o_intra = [jnp.dot(qk[h], v_new[h], preferred_element_type=F32) for h in heads]
    s_add = [lax.dot_general(k_dec[h], v_new[h], _TN, preferred_element_type=F32) for h in heads]
    for h in heads:
        col = h * DN_HEAD_DIM
        s_scr[h] = s_old[h] * jnp.exp(glast[h]) + s_add[h]
        zed = z_ref[:, col:col + DN_HEAD_DIM].astype(F32)
        o = o_inter[h] + o_intra[h]
        o_ref[:, col:col + DN_HEAD_DIM] = (_rms(o) * nh_ref[...] * _silu(zed)).astype(o_ref.dtype)

    buf[0:HIST_ROWS, :] = buf[CHUNK:CHUNK + HIST_ROWS, :]

    @pl.when(c_id == pl.num_programs(1) - 1)
    def _():
        sfin_ref[...] = s_scr[...]


def _dn_chunk(qkv, small, z, conv_w, hist, s0, alog, dtb, nh):
    b, l, _ = qkv.shape
    nc = l // CHUNK
    const2 = lambda bi, ci: (0, 0)
    return pl.pallas_call(
        _dn_body, grid=(b, nc),
        in_specs=[
            pl.BlockSpec((None, CHUNK, 3 * DN_QK), lambda bi, ci: (bi, ci, 0)),
            pl.BlockSpec((None, CHUNK, SMALL_W), lambda bi, ci: (bi, ci, 0)),
            pl.BlockSpec((None, CHUNK, DN_V), lambda bi, ci: (bi, ci, 0)),
            pl.BlockSpec(conv_w.shape, const2),
            pl.BlockSpec(hist.shape, const2),
            pl.BlockSpec(s0.shape, lambda bi, ci: (0, 0, 0)),
            pl.BlockSpec(alog.shape, const2),
            pl.BlockSpec(dtb.shape, const2),
            pl.BlockSpec(nh.shape, const2),
        ],
        out_specs=[
            pl.BlockSpec((None, CHUNK, DN_V), lambda bi, ci: (bi, ci, 0)),
            pl.BlockSpec((None, DN_HEADS, DN_HEAD_DIM, DN_HEAD_DIM), lambda bi, ci: (bi, 0, 0, 0)),
        ],
        out_shape=[
            jax.ShapeDtypeStruct((b, l, DN_V), BF16),
            jax.ShapeDtypeStruct((b, DN_HEADS, DN_HEAD_DIM, DN_HEAD_DIM), F32),
        ],
        scratch_shapes=[
            pltpu.VMEM((HIST_ROWS + CHUNK, 3 * DN_QK), F32),
            pltpu.VMEM((DN_HEADS, DN_HEAD_DIM, DN_HEAD_DIM), F32),
        ],
        compiler_params=pltpu.CompilerParams(dimension_semantics=("parallel", "arbitrary"),
                                             vmem_limit_bytes=VMEM_LIMIT),
        name="dn_chunk",
    )(qkv, small, z, conv_w, hist, s0, alog, dtb, nh)


def _gla_body(qk_ref, v_ref, r_ref, sm_ref, wa_ref, ba_ref, nh_ref, s0_ref, o_ref, sfin_ref, s_scr):
    c_id = pl.program_id(1)

    @pl.when(c_id == 0)
    def _():
        s_scr[...] = s0_ref[...]

    la = jnp.dot(sm_ref[...].astype(BF16), wa_ref[...], preferred_element_type=F32) + ba_ref[...]
    log_alpha = (jnp.minimum(la, 0.0) - jnp.log(1.0 + jnp.exp(-jnp.abs(la)))) * (1.0 / GLA_TAU)
    b_all = _cumsum_rows(log_alpha)

    r, c = _tri_incl(CHUNK)
    causal = r >= c
    scale = GLA_KEY_DIM ** -0.5
    mid = CHUNK // 2 - 1

    heads = range(GLA_HEADS)
    q = [qk_ref[:, h * GLA_KEY_DIM:(h + 1) * GLA_KEY_DIM].astype(F32) * scale for h in heads]
    k = [qk_ref[:, GLA_QK + h * GLA_KEY_DIM:GLA_QK + (h + 1) * GLA_KEY_DIM].astype(F32) for h in heads]
    v = [v_ref[:, h * GLA_VAL_DIM:(h + 1) * GLA_VAL_DIM] for h in heads]
    bh = [b_all[:, h * GLA_KEY_DIM:(h + 1) * GLA_KEY_DIM] for h in heads]
    bmid = [x[mid:mid + 1, :] for x in bh]
    blast = [x[CHUNK - 1:CHUNK, :] for x in bh]
    qs = [(q[h] * jnp.exp(bh[h] - bmid[h])).astype(BF16) for h in heads]
    ks = [(k[h] * jnp.exp(bmid[h] - bh[h])).astype(BF16) for h in heads]
    qd = [(q[h] * jnp.exp(bh[h])).astype(BF16) for h in heads]
    kd = [(k[h] * jnp.exp(blast[h] - bh[h])).astype(BF16) for h in heads]
    st = [s_scr[h] for h in heads]
    att = [lax.dot_general(qs[h], ks[h], _NT, preferred_element_type=F32) for h in heads]
    o_inter = [lax.dot_general(qd[h], st[h].astype(BF16), _NT, preferred_element_type=F32) for h in heads]
    s_add = [lax.dot_general(v[h], kd[h], _TN, preferred_element_type=F32) for h in heads]
    att = [jnp.where(causal, a, 0.0).astype(BF16) for a in att]
    o_intra = [jnp.dot(att[h], v[h], preferred_element_type=F32) for h in heads]
    for h in heads:
        vc = h * GLA_VAL_DIM
        s_scr[h] = st[h] * jnp.exp(blast[h]) + s_add[h]
        gate = _silu(r_ref[:, vc:vc + GLA_VAL_DIM].astype(F32))
        o = o_inter[h] + o_intra[h]
        o_ref[:, vc:vc + GLA_VAL_DIM] = (_rms(o) * nh_ref[...] * gate).astype(o_ref.dtype)

    @pl.when(c_id == pl.num_programs(1) - 1)
    def _():
        sfin_ref[...] = s_scr[...]


def _gla_chunk(qk, v, rr, small, wa, ba, nh, s0):
    b, l, _ = qk.shape
    nc = l // CHUNK
    const2 = lambda bi, ci: (0, 0)
    blk = lambda w: pl.BlockSpec((None, CHUNK, w), lambda bi, ci: (bi, ci, 0))
    return pl.pallas_call(
        _gla_body, grid=(b, nc),
        in_specs=[
            blk(2 * GLA_QK), blk(GLA_V), blk(GLA_V), blk(SMALL_W),
            pl.BlockSpec(wa.shape, const2), pl.BlockSpec(ba.shape, const2), pl.BlockSpec(nh.shape, const2),
            pl.BlockSpec(s0.shape, lambda bi, ci: (0, 0, 0)),
        ],
        out_specs=[
            blk(GLA_V),
            pl.BlockSpec((None, GLA_HEADS, GLA_VAL_DIM, GLA_KEY_DIM), lambda bi, ci: (bi, 0, 0, 0)),
        ],
        out_shape=[
            jax.ShapeDtypeStruct((b, l, GLA_V), BF16),
            jax.ShapeDtypeStruct((b, GLA_HEADS, GLA_VAL_DIM, GLA_KEY_DIM), F32),
        ],
        scratch_shapes=[pltpu.VMEM((GLA_HEADS, GLA_VAL_DIM, GLA_KEY_DIM), F32)],
        compiler_params=pltpu.CompilerParams(dimension_semantics=("parallel", "arbitrary"),
                                             vmem_limit_bytes=VMEM_LIMIT),
        name="gla_chunk",
    )(qk, v, rr, small, wa, ba, nh, s0)


def _outproj_body(odn_ref, ogla_ref, gates_ref, x_ref, wd_ref, wg_ref, wo_ref, gn_ref, wr_ref, br_ref,
                  h2_ref, hn_ref, lg_ref):
    y_dn = jnp.dot(odn_ref[...], wd_ref[...], preferred_element_type=F32)
    y_gla = jnp.dot(ogla_ref[...], wg_ref[...], preferred_element_type=F32)
    gd = _sigmoid(gates_ref[:, 0:D_MODEL].astype(F32))
    gg = _sigmoid(gates_ref[:, D_MODEL:2 * D_MODEL].astype(F32))
    merged = gd * y_dn + gg * y_gla
    h2 = x_ref[...] + _bdot(merged, wo_ref[...])
    h2_ref[...] = h2
    hn = _rms(h2) * gn_ref[...]
    hn_ref[...] = hn
    lg_ref[...] = lax.dot_general(wr_ref[...], hn, _NT, preferred_element_type=F32,
                                  precision=lax.Precision.HIGHEST) + br_ref[...]


def _out_proj(o_dn, o_gla, gates, x2d, wd, wg, wo, gn, wr_t, br, rows):
    t = x2d.shape[0]
    row_blk = lambda w: pl.BlockSpec((rows, w), lambda i: (i, 0))
    const = lambda a: pl.BlockSpec(a.shape, lambda i: (0, 0))
    return pl.pallas_call(
        _outproj_body, grid=(t // rows,),
        in_specs=[row_blk(DN_V), row_blk(GLA_V), row_blk(2 * D_MODEL), row_blk(D_MODEL),
                  const(wd), const(wg), const(wo), const(gn), const(wr_t), const(br)],
        out_specs=[row_blk(D_MODEL), row_blk(D_MODEL), pl.BlockSpec((LANES, rows), lambda i: (0, i))],
        out_shape=[jax.ShapeDtypeStruct((t, D_MODEL), F32), jax.ShapeDtypeStruct((t, D_MODEL), F32),
                   jax.ShapeDtypeStruct((LANES, t), F32)],
        compiler_params=pltpu.CompilerParams(dimension_semantics=("parallel",), vmem_limit_bytes=VMEM_LIMIT),
        name="out_proj",
    )(o_dn, o_gla, gates, x2d, wd, wg, wo, gn, wr_t, br)


ROUTE_SUB = 256


def _route_body(lg_ref, idx_ref, gate_ref, cnt_ref, carry):
    step = pl.program_id(0)

    @pl.when(step == 0)
    def _():
        carry[...] = jnp.zeros_like(carry)

    tt = lg_ref.shape[1]
    gl = lg_ref[0:N_GROUPS, :]
    gmax = jnp.max(gl, axis=0, keepdims=True)
    rid8 = lax.broadcasted_iota(jnp.int32, (N_GROUPS, tt), 0)
    gsel = jnp.min(jnp.where(gl == gmax, rid8, N_GROUPS), axis=0, keepdims=True)
    gw = 1.0 / jnp.sum(jnp.exp(gl - gmax), axis=0, keepdims=True)
    el = lg_ref[N_GROUPS:N_GROUPS + N_EXPERTS, :]
    rid = lax.broadcasted_iota(jnp.int32, (N_EXPERTS, tt), 0)
    neg = jnp.float32(-jnp.inf)
    ein = jnp.where((rid >> 3) == gsel, el, neg)
    t1 = jnp.max(ein, axis=0, keepdims=True)
    i1 = jnp.min(jnp.where(ein == t1, rid, N_EXPERTS), axis=0, keepdims=True)
    ein2 = jnp.where(rid == i1, neg, ein)
    t2 = jnp.max(ein2, axis=0, keepdims=True)
    i2 = jnp.min(jnp.where(ein2 == t2, rid, N_EXPERTS), axis=0, keepdims=True)
    e21 = jnp.exp(t2 - t1)
    den = 1.0 / (1.0 + e21)
    sel1 = rid == i1
    sel2 = rid == i2
    onehot = jnp.where(sel1 | sel2, 1.0, 0.0)

    ur, uc = _tri_incl(ROUTE_SUB)
    upper = (ur <= uc).astype(BF16)
    run = carry[...]
    r1_parts, r2_parts = [], []
    for s in range(tt // ROUTE_SUB):
        sl = slice(s * ROUTE_SUB, (s + 1) * ROUTE_SUB)
        oh = onehot[:, sl]
        incl = jnp.dot(oh.astype(BF16), upper, preferred_element_type=F32) + run
        excl = incl - oh
        r1_parts.append(jnp.sum(jnp.where(sel1[:, sl], excl, 0.0), axis=0, keepdims=True))
        r2_parts.append(jnp.sum(jnp.where(sel2[:, sl], excl, 0.0), axis=0, keepdims=True))
        run = jnp.broadcast_to(incl[:, ROUTE_SUB - 1:ROUTE_SUB], run.shape)
    carry[...] = run
    r1 = jnp.concatenate(r1_parts, axis=1) if len(r1_parts) > 1 else r1_parts[0]
    r2 = jnp.concatenate(r2_parts, axis=1) if len(r2_parts) > 1 else r2_parts[0]

    idx_ref[...] = jnp.zeros_like(idx_ref)
    idx_ref[0:1, :] = i1
    idx_ref[1:2, :] = i2
    idx_ref[2:3, :] = r1.astype(jnp.int32)
    idx_ref[3:4, :] = r2.astype(jnp.int32)
    gate_ref[...] = jnp.zeros_like(gate_ref)
    gate_ref[0:1, :] = den * gw
    gate_ref[1:2, :] = e21 * den * gw
    cnt_ref[...] = run[:, 0:LANES]


def _route(logits_t, lanes):
    t = logits_t.shape[1]
    return pl.pallas_call(
        _route_body, grid=(t // lanes,),
        in_specs=[pl.BlockSpec((LANES, lanes), lambda i: (0, i))],
        out_specs=[pl.BlockSpec((8, lanes), lambda i: (0, i)), pl.BlockSpec((8, lanes), lambda i: (0, i)),
                   pl.BlockSpec((N_EXPERTS, LANES), lambda i: (0, 0))],
        out_shape=[jax.ShapeDtypeStruct((8, t), jnp.int32), jax.ShapeDtypeStruct((8, t), F32),
                   jax.ShapeDtypeStruct((N_EXPERTS, LANES), F32)],
        scratch_shapes=[pltpu.VMEM((N_EXPERTS, ROUTE_SUB), F32)],
        compiler_params=pltpu.CompilerParams(dimension_semantics=("arbitrary",)),
        name="route",
    )(logits_t)


def _row_copy(src_ref, src_row, dst_ref, dst_row, sem):
    return pltpu.make_async_copy(src_ref.at[pl.ds(src_row, 1)], dst_ref.at[pl.ds(dst_row, 1)], sem)


ROW_UNROLL = 8


def _slots_body(pstart_ref, idx_ref, slot_ref):
    idx = idx_ref[...]
    base = jnp.zeros_like(idx)
    for e in range(N_EXPERTS):
        base = jnp.where(idx == e, pstart_ref[e], base)
    slot_ref[...] = jnp.zeros_like(idx)
    slot_ref[0:2, :] = base[0:2, :] + idx[2:4, :]


def _slots(pstart, idx, lanes):
    t = idx.shape[1]
    grid_spec = pltpu.PrefetchScalarGridSpec(
        num_scalar_prefetch=1, grid=(t // lanes,),
        in_specs=[pl.BlockSpec((8, lanes), lambda i, ps: (0, i))],
        out_specs=pl.BlockSpec((8, lanes), lambda i, ps: (0, i)),
    )
    return pl.pallas_call(
        _slots_body, grid_spec=grid_spec, out_shape=jax.ShapeDtypeStruct(idx.shape, jnp.int32),
        compiler_params=pltpu.CompilerParams(dimension_semantics=("parallel",)),
        name="slots",
    )(pstart, idx)


def _dispatch_body(slot_ref, hn_ref, xs_in_ref, xs_ref, sem):
    del xs_in_ref
    n_tok = hn_ref.shape[0]

    def issue(t, carry):
        for k in range(2):
            _row_copy(hn_ref, t, xs_ref, slot_ref[k, t], sem).start()
        return carry

    lax.fori_loop(0, n_tok, issue, 0, unroll=ROW_UNROLL)
    for k in range(2):
        pltpu.make_async_copy(hn_ref, xs_ref.at[pl.ds(0, n_tok)], sem).wait()


def _dispatch(slots, hn, xs_zero, tokens):
    t = hn.shape[0]
    return pl.pallas_call(
        _dispatch_body, grid=(t // tokens,),
        in_specs=[pl.BlockSpec((8, tokens), lambda i: (0, i), memory_space=pltpu.SMEM),
                  pl.BlockSpec((tokens, D_MODEL), lambda i: (i, 0)),
                  pl.BlockSpec(memory_space=pl.ANY)],
        out_specs=pl.BlockSpec(memory_space=pl.ANY),
        scratch_shapes=[pltpu.SemaphoreType.DMA(())],
        out_shape=jax.ShapeDtypeStruct(xs_zero.shape, xs_zero.dtype),
        input_output_aliases={2: 0},
        compiler_params=pltpu.CompilerParams(dimension_semantics=("arbitrary",)),
        name="dispatch",
    )(slots, hn, xs_zero)


def _experts_body(be_ref, nused_ref, xs_ref, w1_ref, w3_ref, w2_ref, yb_ref):
    used = pl.program_id(0) < nused_ref[0]

    @pl.when(used)
    def _():
        xb = xs_ref[...].astype(BF16)
        h1 = jnp.dot(xb, w1_ref[...], preferred_element_type=F32)
        h3 = jnp.dot(xb, w3_ref[...], preferred_element_type=F32)
        hid = (_silu(h1) * h3).astype(BF16)
        yb_ref[...] = jnp.dot(hid, w2_ref[...], preferred_element_type=F32)

    @pl.when(jnp.logical_not(used))
    def _():
        yb_ref[...] = jnp.zeros_like(yb_ref)


def _experts(block_expert, n_used, xs, w1, w3, w2):
    p = xs.shape[0]
    nb = p // MOE_BLOCK
    row_map = lambda j, be, nu: (jnp.minimum(j, nu[0] - 1), 0)
    w_map = lambda j, be, nu: (be[j], 0, 0)
    grid_spec = pltpu.PrefetchScalarGridSpec(
        num_scalar_prefetch=2, grid=(nb,),
        in_specs=[pl.BlockSpec((MOE_BLOCK, D_MODEL), row_map),
                  pl.BlockSpec((None, D_MODEL, D_EXPERT), w_map),
                  pl.BlockSpec((None, D_MODEL, D_EXPERT), w_map),
                  pl.BlockSpec((None, D_EXPERT, D_MODEL), w_map)],
        out_specs=pl.BlockSpec((MOE_BLOCK, D_MODEL), lambda j, be, nu: (j, 0)),
    )
    return pl.pallas_call(
        _experts_body, grid_spec=grid_spec,
        out_shape=jax.ShapeDtypeStruct((p, D_MODEL), F32),
        compiler_params=pltpu.CompilerParams(dimension_semantics=("arbitrary",), vmem_limit_bytes=VMEM_LIMIT),
        name="experts",
    )(block_expert, n_used, xs, w1, w3, w2)


def _combine_body(slot_ref, gate_ref, h2_ref, gf_ref, yb_ref, o_ref, ybuf, sem):
    n_tok = h2_ref.shape[0]

    def issue(t, carry):
        for k in range(2):
            pltpu.make_async_copy(yb_ref.at[pl.ds(slot_ref[k, t], 1)], ybuf.at[k, pl.ds(t, 1)], sem).start()
        return carry

    lax.fori_loop(0, n_tok, issue, 0, unroll=ROW_UNROLL)
    for k in range(2):
        pltpu.make_async_copy(yb_ref.at[pl.ds(0, n_tok)], ybuf.at[k], sem).wait()
    y = h2_ref[...] + gate_ref[:, 0:1] * ybuf[0] + gate_ref[:, 1:2] * ybuf[1]
    o_ref[...] = _rms(y) * gf_ref[...]


def _combine(slots, gate_t, h2, gf, yb, tokens):
    t = h2.shape[0]
    return pl.pallas_call(
        _combine_body, grid=(t // tokens,),
        in_specs=[pl.BlockSpec((8, tokens), lambda i: (0, i), memory_space=pltpu.SMEM),
                  pl.BlockSpec((tokens, 8), lambda i: (i, 0)),
                  pl.BlockSpec((tokens, D_MODEL), lambda i: (i, 0)),
                  pl.BlockSpec((1, D_MODEL), lambda i: (0, 0)),
                  pl.BlockSpec(memory_space=pl.ANY)],
        out_specs=pl.BlockSpec((tokens, D_MODEL), lambda i: (i, 0)),
        scratch_shapes=[pltpu.VMEM((2, tokens, D_MODEL), F32), pltpu.SemaphoreType.DMA(())],
        out_shape=jax.ShapeDtypeStruct((t, D_MODEL), F32),
        compiler_params=pltpu.CompilerParams(dimension_semantics=("arbitrary",)),
        name="combine",
    )(slots, gate_t, h2, gf, yb)


def _pad_lanes(v, width=LANES):
    v = v.reshape(1, -1).astype(F32)
    return jnp.pad(v, ((0, 0), (0, width - v.shape[1])))


def _largest_tile(n, cap):
    t = cap
    while n % t:
        t //= 2
    return t


def kernel(x, meta_tokens, norm_mix, w_in, conv_dn, a_log, dt_bias, norm_head_dn, w_proj_dn, w_alpha, b_alpha,
           norm_head_gla, w_proj_gla, w_out, norm_ffn, w_group, b_group, w_router, b_router, w1, w3, w2, norm_final):
    assert norm_mix.shape[0] == 1, "single-layer block"
    bsz, seq, d = x.shape
    assert d == D_MODEL and seq % CHUNK == 0
    t = bsz * seq

    wi = w_in[0]
    offs = np.cumsum([0, DN_QK, DN_QK, DN_V, DN_V, DN_HEADS, DN_HEADS, GLA_QK, GLA_QK, GLA_V, GLA_V, GLA_RANK,
                      D_MODEL, D_MODEL]).tolist()
    seg = lambda i, j: wi[:, offs[i]:offs[j]]
    n_small = 2 * DN_HEADS + GLA_RANK
    w_all = jnp.concatenate([seg(0, 4), seg(6, 10), seg(11, 13), seg(4, 6), seg(10, 11),
                             jnp.zeros((D_MODEL, SMALL_W - n_small), F32)], axis=1).astype(BF16)
    g_mix = norm_mix[0].reshape(1, D_MODEL)

    conv_w = jnp.pad(conv_dn[0], ((0, 8 - CONV_WIDTH), (0, 0)))
    alog = _pad_lanes(a_log[0])
    dtb = _pad_lanes(dt_bias[0])
    nh_dn = norm_head_dn[0].reshape(1, DN_HEAD_DIM)
    lr_off = 2 * DN_HEADS
    wa = jnp.zeros((SMALL_W, GLA_QK), F32).at[lr_off:lr_off + GLA_RANK].set(w_alpha[0]).astype(BF16)
    ba = b_alpha[0].reshape(1, GLA_QK)
    nh_gla = norm_head_gla[0].reshape(1, GLA_VAL_DIM)

    def mixers(tokens2d, nb, hist, s0_dn, s0_gla, rows):
        qkv, z, qk_g, v_g, r_g, gates, small = _in_proj(tokens2d, g_mix, w_all, rows)
        r3 = lambda a: a.reshape(nb, -1, a.shape[-1])
        o_dn, s_dn = _dn_chunk(r3(qkv), r3(small), r3(z), conv_w, hist, s0_dn, alog, dtb, nh_dn)
        o_gla, s_gla = _gla_chunk(r3(qk_g), r3(v_g), r3(r_g), r3(small), wa, ba, nh_gla, s0_gla)
        return qkv, o_dn, o_gla, gates, s_dn, s_gla

    meta_rows = jnp.pad(meta_tokens.astype(F32), ((CHUNK - N_META, 0), (0, 0)))
    zero_hist = jnp.zeros((HIST_ROWS, 3 * DN_QK), F32)
    zero_dn = jnp.zeros((DN_HEADS, DN_HEAD_DIM, DN_HEAD_DIM), F32)
    zero_gla = jnp.zeros((GLA_HEADS, GLA_VAL_DIM, GLA_KEY_DIM), F32)
    qkv_m, _, _, _, s_dn_m, s_gla_m = mixers(meta_rows, 1, zero_hist, zero_dn, zero_gla, CHUNK)
    hist = qkv_m[CHUNK - HIST_ROWS:].astype(F32)

    x2d = x.reshape(t, d)
    rows = _largest_tile(t, 512)
    _, o_dn, o_gla, gates, _, _ = mixers(x2d, bsz, hist, s_dn_m[0], s_gla_m[0], rows)

    wr_t = jnp.concatenate([w_group[0], w_router[0]], axis=1).T
    wr_t = jnp.pad(wr_t, ((0, LANES - wr_t.shape[0]), (0, 0)))
    br = jnp.pad(jnp.concatenate([b_group[0], b_router[0]]), (0, LANES - N_GROUPS - N_EXPERTS)).reshape(LANES, 1)
    h2, hn, logits_t = _out_proj(
        o_dn.reshape(t, DN_V), o_gla.reshape(t, GLA_V), gates, x2d,
        w_proj_dn[0].astype(BF16), w_proj_gla[0].astype(BF16), w_out[0].astype(BF16),
        norm_ffn[0].reshape(1, D_MODEL), wr_t, br, rows)

    idx, gate, cnt = _route(logits_t, _largest_tile(t, 512))

    sizes = cnt[:, 0].astype(jnp.int32)
    padded = (sizes + MOE_BLOCK - 1) // MOE_BLOCK * MOE_BLOCK
    pends = jnp.cumsum(padded)
    pstart = (pends - padded).astype(jnp.int32)
    n_blocks = (2 * t) // MOE_BLOCK + N_EXPERTS
    n_used = (pends[-1:] // MOE_BLOCK).astype(jnp.int32)
    block_row0 = jnp.arange(n_blocks, dtype=jnp.int32) * MOE_BLOCK
    block_expert = jnp.minimum(jnp.sum(pends[None, :] <= block_row0[:, None], axis=1), N_EXPERTS - 1).astype(jnp.int32)

    tok_tile = _largest_tile(t, 256)
    slots = _slots(pstart, idx, _largest_tile(t, 2048))
    xs = _dispatch(slots, hn, jnp.zeros((n_blocks * MOE_BLOCK, D_MODEL), F32), tok_tile)
    yb = _experts(block_expert, n_used, xs, w1[0].astype(BF16), w3[0].astype(BF16), w2[0].astype(BF16))
    out = _combine(slots, gate.T, h2, norm_final.reshape(1, D_MODEL), yb, tok_tile)
    return out.reshape(bsz, seq, d)
```

```python
import functools

import jax
import jax.numpy as jnp
import numpy as np
from jax import lax
from jax.experimental import pallas as pl
from jax.experimental.pallas import tpu as pltpu

F32 = jnp.float32
BF16 = jnp.bfloat16

D_MODEL = 1024
CHUNK = 64
N_META = 16
EPS = 1e-6
DN_HEADS = 8
DN_HEAD_DIM = 128
DN_QK = DN_HEADS * DN_HEAD_DIM
DN_V = DN_HEADS * DN_HEAD_DIM
CONV_WIDTH = 4
GLA_HEADS = 4
GLA_KEY_DIM = 128
GLA_VAL_DIM = 256
GLA_QK = GLA_HEADS * GLA_KEY_DIM
GLA_V = GLA_HEADS * GLA_VAL_DIM
GLA_RANK = 16
GLA_TAU = 16.0
N_GROUPS = 8
EXPERTS_PER_GROUP = 8
N_EXPERTS = N_GROUPS * EXPERTS_PER_GROUP
D_EXPERT = 512
MOE_BLOCK = 256

LANES = 128
SMALL_W = LANES
HIST_ROWS = 8
VMEM_LIMIT = 56 * 1024 * 1024

_NT = (((1,), (1,)), ((), ()))
_TN = (((0,), (0,)), ((), ()))


def _bdot(a, b):
    return jnp.dot(a.astype(BF16), b.astype(BF16), preferred_element_type=F32)


def _bdot_nt(a, b):
    return lax.dot_general(a.astype(BF16), b.astype(BF16), _NT, preferred_element_type=F32)


def _bdot_tn(a, b):
    return lax.dot_general(a.astype(BF16), b.astype(BF16), _TN, preferred_element_type=F32)


def _sigmoid(x):
    return 1.0 / (1.0 + jnp.exp(-x))


def _silu(x):
    return x * _sigmoid(x)


def _softplus(x):
    return jnp.maximum(x, 0.0) + jnp.log(1.0 + jnp.exp(-jnp.abs(x)))


def _rms(x, eps=EPS):
    return x * lax.rsqrt(jnp.mean(x * x, axis=-1, keepdims=True) + eps)


def _tri_incl(n):
    r = lax.broadcasted_iota(jnp.int32, (n, n), 0)
    c = lax.broadcasted_iota(jnp.int32, (n, n), 1)
    return r, c


def _bf16_part(x):
    bits = pltpu.bitcast(x, jnp.uint32) & jnp.uint32(0xFFFF0000)
    return pltpu.bitcast(bits, F32)


def _cumsum_rows(x):
    r, c = _tri_incl(x.shape[0])
    tri = (r >= c).astype(F32)
    hi = _bf16_part(x)
    r1 = x - hi
    mid = _bf16_part(r1)
    lo = r1 - mid
    return (jnp.dot(tri, hi, preferred_element_type=F32) + jnp.dot(tri, mid, preferred_element_type=F32)
            + jnp.dot(tri, lo, preferred_element_type=F32))


IN_GROUPS = ((3 * DN_QK, BF16), (DN_V, BF16), (2 * GLA_QK, BF16), (GLA_V, BF16), (GLA_V, BF16),
             (2 * D_MODEL, BF16), (SMALL_W, F32))


def _inproj_body(x_ref, g_ref, w_ref, *o_refs):
    x = x_ref[...]
    xb = (_rms(x) * g_ref[...]).astype(BF16)
    col = 0
    for o_ref in o_refs:
        width = o_ref.shape[1]
        o_ref[...] = jnp.dot(xb, w_ref[:, col:col + width], preferred_element_type=F32).astype(o_ref.dtype)
        col += width


def _in_proj(x2d, gain, w_all, rows):
    t = x2d.shape[0]
    in_specs = [pl.BlockSpec((rows, D_MODEL), lambda i: (i, 0)), pl.BlockSpec((1, D_MODEL), lambda i: (0, 0)),
                pl.BlockSpec(w_all.shape, lambda i: (0, 0), pipeline_mode=pl.Buffered(1))]
    out_specs = [pl.BlockSpec((rows, w), lambda i: (i, 0)) for w, _ in IN_GROUPS]
    out_shape = [jax.ShapeDtypeStruct((t, w), dt) for w, dt in IN_GROUPS]
    return pl.pallas_call(
        _inproj_body, grid=(t // rows,), in_specs=in_specs, out_specs=out_specs, out_shape=out_shape,
        compiler_params=pltpu.CompilerParams(dimension_semantics=("parallel",), vmem_limit_bytes=VMEM_LIMIT),
        name="in_proj",
    )(x2d, gain, w_all)


def _unit_lower_inverse(mats):
    n = mats[0].shape[0]
    r, c = _tri_incl(n)
    eye = (r == c).astype(F32)
    pair = (r >> 1) == (c >> 1)
    ts = [eye - jnp.where(pair, a, 0.0) for a in mats]
    m = 2
    while m < n:
        sh = m.bit_length() - 1
        keep = ((r >> (sh + 1)) == (c >> (sh + 1))) & ((r >> sh) != (c >> sh))
        ams = [jnp.where(keep, a, 0.0) for a in mats]
        xs = [jnp.dot(t, am, preferred_element_type=F32) for t, am in zip(ts, ams)]
        ys = [jnp.dot(x, t, preferred_element_type=F32) for x, t in zip(xs, ts)]
        ts = [t - y for t, y in zip(ts, ys)]
        m *= 2
    return ts


def _l2_normalise(xs, ones_bd, scale):
    out = []
    for p in range(0, len(xs), 2):
        sq = jnp.concatenate([xs[p] * xs[p], xs[p + 1] * xs[p + 1]], axis=1)
        ss = jnp.dot(sq, ones_bd, preferred_element_type=F32)
        inv = lax.rsqrt(ss + EPS) * scale
        out.append(xs[p] * inv[:, :DN_HEAD_DIM])
        out.append(xs[p + 1] * inv[:, DN_HEAD_DIM:])
    return out


def _dn_body(qkv_ref, sm_ref, z_ref, cw_ref, hist_ref, s0_ref, alog_ref, dtb_ref, nh_ref, ones_ref,
             o_ref, sfin_ref, buf, s_scr):
    c_id = pl.program_id(1)
    n_rows = qkv_ref.shape[0]
    items = [(b, h) for b in range(n_rows) for h in range(DN_HEADS)]
    n_items = range(len(items))

    @pl.when(c_id == 0)
    def _():
        for b in range(n_rows):
            buf[b, 0:HIST_ROWS, :] = hist_ref[...]
            s_scr[b] = s0_ref[...]

    for b in range(n_rows):
        buf[b, HIST_ROWS:HIST_ROWS + CHUNK, :] = qkv_ref[b].astype(F32)

    def conv_silu(b, col):
        acc = None
        for j in range(CONV_WIDTH):
            lo = HIST_ROWS - (CONV_WIDTH - 1) + j
            term = buf[b, lo:lo + CHUNK, col:col + DN_HEAD_DIM] * cw_ref[j:j + 1, col:col + DN_HEAD_DIM]
            acc = term if acc is None else acc + term
        return _silu(acc)

    gc_all, gc_t, beta_all = [], [], []
    for b in range(n_rows):
        sm = sm_ref[b]
        g_b = -jnp.exp(alog_ref[...]) * _softplus(sm + dtb_ref[...])
        beta_all.append(_sigmoid(sm))
        gc_b = _cumsum_rows(g_b)
        gc_all.append(gc_b)
        gc_t.append(jnp.concatenate([gc_b, jnp.zeros_like(gc_b)], axis=0).T)

    r, c = _tri_incl(CHUNK)
    causal = r >= c
    strict = r > c
    scale = DN_HEAD_DIM ** -0.5

    q = [conv_silu(b, h * DN_HEAD_DIM) for b, h in items]
    k = [conv_silu(b, DN_QK + h * DN_HEAD_DIM) for b, h in items]
    v = [conv_silu(b, 2 * DN_QK + h * DN_HEAD_DIM) for b, h in items]
    ones_bd = ones_ref[...]
    q = _l2_normalise(q, ones_bd, scale)
    k = _l2_normalise(k, ones_bd, 1.0)
    gcol = [gc_all[b][:, h:h + 1] for b, h in items]
    grow = [gc_t[b][h:h + 1, 0:CHUNK] for b, h in items]
    bcol = [beta_all[b][:, DN_HEADS + h:DN_HEADS + h + 1] for b, h in items]
    glast = [gc_all[b][CHUNK - 1:CHUNK, h:h + 1] for b, h in items]
    decay = [jnp.where(causal, jnp.exp(jnp.where(causal, gcol[i] - grow[i], 0.0)), 0.0) for i in n_items]
    eg = [jnp.exp(gcol[i]) for i in n_items]
    kk = [lax.dot_general(k[i], k[i], _NT, preferred_element_type=F32) for i in n_items]
    qk = [lax.dot_general(q[i], k[i], _NT, preferred_element_type=F32) for i in n_items]
    a_mat = [jnp.where(strict, bcol[i] * kk[i] * decay[i], 0.0) for i in n_items]
    qk = [jnp.where(causal, qk[i] * decay[i], 0.0) for i in n_items]
    t_inv = _unit_lower_inverse(a_mat)
    rhs = [jnp.concatenate([v[i] * bcol[i], k[i] * (bcol[i] * eg[i])], axis=1) for i in n_items]
    sol = [jnp.dot(t_inv[i], rhs[i], preferred_element_type=F32) for i in n_items]
    s_old = [s_scr[b, h] for b, h in items]
    qe = [q[i] * eg[i] for i in n_items]
    k_dec = [k[i] * jnp.exp(glast[i] - gcol[i]) for i in n_items]
    ws = [jnp.dot(sol[i][:, DN_HEAD_DIM:], s_old[i], preferred_element_type=F32) for i in n_items]
    o_inter = [jnp.dot(qe[i], s_old[i], preferred_element_type=F32) for i in n_items]
    v_new = [sol[i][:, :DN_HEAD_DIM] - ws[i] for i in n_items]
    o_intra = [jnp.dot(qk[i], v_new[i], preferred_element_type=F32) for i in n_items]
    s_add = [lax.dot_general(k_dec[i], v_new[i], _TN, preferred_element_type=F32) for i in n_items]
    for i, (b, h) in enumerate(items):
        col = h * DN_HEAD_DIM
        s_scr[b, h] = s_old[i] * jnp.exp(glast[i]) + s_add[i]
        zed = z_ref[b, :, col:col + DN_HEAD_DIM].astype(F32)
        o = o_inter[i] + o_intra[i]
        o_ref[b, :, col:col + DN_HEAD_DIM] = (_rms(o) * nh_ref[...] * _silu(zed)).astype(o_ref.dtype)

    for b in range(n_rows):
        buf[b, 0:HIST_ROWS, :] = buf[b, CHUNK:CHUNK + HIST_ROWS, :]

    @pl.when(c_id == pl.num_programs(1) - 1)
    def _():
        sfin_ref[...] = s_scr[...]


def _rows_per_step(b):
    return 2 if b % 2 == 0 else 1


def _dn_chunk(qkv, small, z, conv_w, hist, s0, alog, dtb, nh):
    b, l, _ = qkv.shape
    lane_head = np.arange(2 * DN_HEAD_DIM) // DN_HEAD_DIM
    ones_bd = jnp.asarray(lane_head[:, None] == lane_head[None, :], BF16)
    nc = l // CHUNK
    nr = _rows_per_step(b)
    const2 = lambda bi, ci: (0, 0)
    blk = lambda w: pl.BlockSpec((nr, CHUNK, w), lambda bi, ci: (bi, ci, 0))
    return pl.pallas_call(
        _dn_body, grid=(b // nr, nc),
        in_specs=[
            blk(3 * DN_QK), blk(SMALL_W), blk(DN_V),
            pl.BlockSpec(conv_w.shape, const2),
            pl.BlockSpec(hist.shape, const2),
            pl.BlockSpec(s0.shape, lambda bi, ci: (0, 0, 0)),
            pl.BlockSpec(alog.shape, const2),
            pl.BlockSpec(dtb.shape, const2),
            pl.BlockSpec(nh.shape, const2),
            pl.BlockSpec(ones_bd.shape, const2),
        ],
        out_specs=[
            blk(DN_V),
            pl.BlockSpec((nr, DN_HEADS, DN_HEAD_DIM, DN_HEAD_DIM), lambda bi, ci: (bi, 0, 0, 0)),
        ],
        out_shape=[
            jax.ShapeDtypeStruct((b, l, DN_V), BF16),
            jax.ShapeDtypeStruct((b, DN_HEADS, DN_HEAD_DIM, DN_HEAD_DIM), F32),
        ],
        scratch_shapes=[
            pltpu.VMEM((nr, HIST_ROWS + CHUNK, 3 * DN_QK), F32),
            pltpu.VMEM((nr, DN_HEADS, DN_HEAD_DIM, DN_HEAD_DIM), F32),
        ],
        compiler_params=pltpu.CompilerParams(dimension_semantics=("parallel", "arbitrary"),
                                             vmem_limit_bytes=VMEM_LIMIT),
        name="dn_chunk",
    )(qkv, small, z, conv_w, hist, s0, alog, dtb, nh, ones_bd)


def _gla_body(qk_ref, v_ref, r_ref, sm_ref, wa_ref, ba_ref, nh_ref, s0_ref, o_ref, sfin_ref, s_scr):
    c_id = pl.program_id(1)
    n_rows = qk_ref.shape[0]
    items = [(b, h) for b in range(n_rows) for h in range(GLA_HEADS)]
    n_items = range(len(items))

    @pl.when(c_id == 0)
    def _():
        for b in range(n_rows):
            s_scr[b] = s0_ref[...]

    b_all = []
    for b in range(n_rows):
        la = jnp.dot(sm_ref[b], wa_ref[...], preferred_element_type=F32) + ba_ref[...]
        log_alpha = (jnp.minimum(la, 0.0) - jnp.log(1.0 + jnp.exp(-jnp.abs(la)))) * (1.0 / GLA_TAU)
        b_all.append(_cumsum_rows(log_alpha))

    r, c = _tri_incl(CHUNK)
    causal = r >= c
    scale = GLA_KEY_DIM ** -0.5
    mid = CHUNK // 2 - 1

    q = [qk_ref[b, :, h * GLA_KEY_DIM:(h + 1) * GLA_KEY_DIM].astype(F32) * scale for b, h in items]
    k = [qk_ref[b, :, GLA_QK + h * GLA_KEY_DIM:GLA_QK + (h + 1) * GLA_KEY_DIM].astype(F32) for b, h in items]
    v = [v_ref[b, :, h * GLA_VAL_DIM:(h + 1) * GLA_VAL_DIM].astype(F32) for b, h in items]
    bh = [b_all[b][:, h * GLA_KEY_DIM:(h + 1) * GLA_KEY_DIM] for b, h in items]
    bmid = [x[mid:mid + 1, :] for x in bh]
    blast = [x[CHUNK - 1:CHUNK, :] for x in bh]
    qs = [q[i] * jnp.exp(bh[i] - bmid[i]) for i in n_items]
    ks = [k[i] * jnp.exp(bmid[i] - bh[i]) for i in n_items]
    qd = [q[i] * jnp.exp(bh[i]) for i in n_items]
    kd = [k[i] * jnp.exp(blast[i] - bh[i]) for i in n_items]
    st = [s_scr[b, h] for b, h in items]
    att = [lax.dot_general(qs[i], ks[i], _NT, preferred_element_type=F32) for i in n_items]
    o_inter = [lax.dot_general(qd[i], st[i], _NT, preferred_element_type=F32) for i in n_items]
    s_add = [lax.dot_general(v[i], kd[i], _TN, preferred_element_type=F32) for i in n_items]
    att = [jnp.where(causal, a, 0.0) for a in att]
    o_intra = [jnp.dot(att[i], v[i], preferred_element_type=F32) for i in n_items]
    for i, (b, h) in enumerate(items):
        vc = h * GLA_VAL_DIM
        s_scr[b, h] = st[i] * jnp.exp(blast[i]) + s_add[i]
        gate = _silu(r_ref[b, :, vc:vc + GLA_VAL_DIM].astype(F32))
        o = o_inter[i] + o_intra[i]
        o_ref[b, :, vc:vc + GLA_VAL_DIM] = (_rms(o) * nh_ref[...] * gate).astype(o_ref.dtype)

    @pl.when(c_id == pl.num_programs(1) - 1)
    def _():
        sfin_ref[...] = s_scr[...]


def _gla_chunk(qk, v, rr, small, wa, ba, nh, s0):
    b, l, _ = qk.shape
    nc = l // CHUNK
    nr = _rows_per_step(b)
    const2 = lambda bi, ci: (0, 0)
    blk = lambda w: pl.BlockSpec((nr, CHUNK, w), lambda bi, ci: (bi, ci, 0))
    return pl.pallas_call(
        _gla_body, grid=(b // nr, nc),
        in_specs=[
            blk(2 * GLA_QK), blk(GLA_V), blk(GLA_V), blk(SMALL_W),
            pl.BlockSpec(wa.shape, const2), pl.BlockSpec(ba.shape, const2), pl.BlockSpec(nh.shape, const2),
            pl.BlockSpec(s0.shape, lambda bi, ci: (0, 0, 0)),
        ],
        out_specs=[
            blk(GLA_V),
            pl.BlockSpec((nr, GLA_HEADS, GLA_VAL_DIM, GLA_KEY_DIM), lambda bi, ci: (bi, 0, 0, 0)),
        ],
        out_shape=[
            jax.ShapeDtypeStruct((b, l, GLA_V), BF16),
            jax.ShapeDtypeStruct((b, GLA_HEADS, GLA_VAL_DIM, GLA_KEY_DIM), F32),
        ],
        scratch_shapes=[pltpu.VMEM((nr, GLA_HEADS, GLA_VAL_DIM, GLA_KEY_DIM), F32)],
        compiler_params=pltpu.CompilerParams(dimension_semantics=("parallel", "arbitrary"),
                                             vmem_limit_bytes=VMEM_LIMIT),
        name="gla_chunk",
    )(qk, v, rr, small, wa, ba, nh, s0)


def _outproj_body(odn_ref, ogla_ref, gates_ref, x_ref, wd_ref, wg_ref, wo_ref, gn_ref, wr_ref, br_ref,
                  h2_ref, hn_ref, lg_ref):
    y_dn = jnp.dot(odn_ref[...], wd_ref[...], preferred_element_type=F32)
    y_gla = jnp.dot(ogla_ref[...], wg_ref[...], preferred_element_type=F32)
    gd = _sigmoid(gates_ref[:, 0:D_MODEL].astype(F32))
    gg = _sigmoid(gates_ref[:, D_MODEL:2 * D_MODEL].astype(F32))
    merged = gd * y_dn + gg * y_gla
    h2 = x_ref[...] + _bdot(merged, wo_ref[...])
    h2_ref[...] = h2
    hn = _rms(h2) * gn_ref[...]
    hn_ref[...] = hn
    lg_ref[...] = lax.dot_general(wr_ref[...], hn, _NT, preferred_element_type=F32,
                                  precision=lax.Precision.HIGHEST) + br_ref[...]


def _out_proj(o_dn, o_gla, gates, x2d, wd, wg, wo, gn, wr_t, br, rows):
    t = x2d.shape[0]
    row_blk = lambda w: pl.BlockSpec((rows, w), lambda i: (i, 0))
    const = lambda a: pl.BlockSpec(a.shape, lambda i: (0, 0))
    return pl.pallas_call(
        _outproj_body, grid=(t // rows,),
        in_specs=[row_blk(DN_V), row_blk(GLA_V), row_blk(2 * D_MODEL), row_blk(D_MODEL),
                  const(wd), const(wg), const(wo), const(gn), const(wr_t), const(br)],
        out_specs=[row_blk(D_MODEL), row_blk(D_MODEL), pl.BlockSpec((LANES, rows), lambda i: (0, i))],
        out_shape=[jax.ShapeDtypeStruct((t, D_MODEL), F32), jax.ShapeDtypeStruct((t, D_MODEL), F32),
                   jax.ShapeDtypeStruct((LANES, t), F32)],
        compiler_params=pltpu.CompilerParams(dimension_semantics=("parallel",), vmem_limit_bytes=VMEM_LIMIT),
        name="out_proj",
    )(o_dn, o_gla, gates, x2d, wd, wg, wo, gn, wr_t, br)


ROUTE_SUB = 256


def _route_body(lg_ref, idx_ref, gate_ref, cnt_ref, carry):
    step = pl.program_id(0)

    @pl.when(step == 0)
    def _():
        carry[...] = jnp.zeros_like(carry)

    tt = lg_ref.shape[1]
    gl = lg_ref[0:N_GROUPS, :]
    gmax = jnp.max(gl, axis=0, keepdims=True)
    rid8 = lax.broadcasted_iota(jnp.int32, (N_GROUPS, tt), 0)
    gsel = jnp.min(jnp.where(gl == gmax, rid8, N_GROUPS), axis=0, keepdims=True)
    gw = 1.0 / jnp.sum(jnp.exp(gl - gmax), axis=0, keepdims=True)
    el = lg_ref[N_GROUPS:N_GROUPS + N_EXPERTS, :]
    rid = lax.broadcasted_iota(jnp.int32, (N_EXPERTS, tt), 0)
    neg = jnp.float32(-jnp.inf)
    ein = jnp.where((rid >> 3) == gsel, el, neg)
    t1 = jnp.max(ein, axis=0, keepdims=True)
    i1 = jnp.min(jnp.where(ein == t1, rid, N_EXPERTS), axis=0, keepdims=True)
    ein2 = jnp.where(rid == i1, neg, ein)
    t2 = jnp.max(ein2, axis=0, keepdims=True)
    i2 = jnp.min(jnp.where(ein2 == t2, rid, N_EXPERTS), axis=0, keepdims=True)
    e21 = jnp.exp(t2 - t1)
    den = 1.0 / (1.0 + e21)
    sel1 = rid == i1
    sel2 = rid == i2
    onehot = jnp.where(sel1 | sel2, 1.0, 0.0)

    ur, uc = _tri_incl(ROUTE_SUB)
    upper = (ur <= uc).astype(BF16)
    run = carry[...]
    r1_parts, r2_parts = [], []
    for s in range(tt // ROUTE_SUB):
        sl = slice(s * ROUTE_SUB, (s + 1) * ROUTE_SUB)
        oh = onehot[:, sl]
        incl = jnp.dot(oh.astype(BF16), upper, preferred_element_type=F32) + run
        excl = incl - oh
        r1_parts.append(jnp.sum(jnp.where(sel1[:, sl], excl, 0.0), axis=0, keepdims=True))
        r2_parts.append(jnp.sum(jnp.where(sel2[:, sl], excl, 0.0), axis=0, keepdims=True))
        run = jnp.broadcast_to(incl[:, ROUTE_SUB - 1:ROUTE_SUB], run.shape)
    carry[...] = run
    r1 = jnp.concatenate(r1_parts, axis=1) if len(r1_parts) > 1 else r1_parts[0]
    r2 = jnp.concatenate(r2_parts, axis=1) if len(r2_parts) > 1 else r2_parts[0]

    idx_ref[...] = jnp.zeros_like(idx_ref)
    idx_ref[0:1, :] = i1
    idx_ref[1:2, :] = i2
    idx_ref[2:3, :] = r1.astype(jnp.int32)
    idx_ref[3:4, :] = r2.astype(jnp.int32)
    gate_ref[...] = jnp.zeros_like(gate_ref)
    gate_ref[0:1, :] = den * gw
    gate_ref[1:2, :] = e21 * den * gw
    cnt_ref[...] = run[:, 0:LANES]


def _route(logits_t, lanes):
    t = logits_t.shape[1]
    return pl.pallas_call(
        _route_body, grid=(t // lanes,),
        in_specs=[pl.BlockSpec((LANES, lanes), lambda i: (0, i))],
        out_specs=[pl.BlockSpec((8, lanes), lambda i: (0, i)), pl.BlockSpec((8, lanes), lambda i: (0, i)),
                   pl.BlockSpec((N_EXPERTS, LANES), lambda i: (0, 0))],
        out_shape=[jax.ShapeDtypeStruct((8, t), jnp.int32), jax.ShapeDtypeStruct((8, t), F32),
                   jax.ShapeDtypeStruct((N_EXPERTS, LANES), F32)],
        scratch_shapes=[pltpu.VMEM((N_EXPERTS, ROUTE_SUB), F32)],
        compiler_params=pltpu.CompilerParams(dimension_semantics=("arbitrary",)),
        name="route",
    )(logits_t)


def _row_copy(src_ref, src_row, dst_ref, dst_row, sem):
    return pltpu.make_async_copy(src_ref.at[pl.ds(src_row, 1)], dst_ref.at[pl.ds(dst_row, 1)], sem)


ROW_UNROLL = 8


def _slots_body(pstart_ref, idx_ref, slot_ref):
    idx = idx_ref[...]
    base = jnp.zeros_like(idx)
    for e in range(N_EXPERTS):
        base = jnp.where(idx == e, pstart_ref[e], base)
    slot_ref[...] = jnp.zeros_like(idx)
    slot_ref[0:2, :] = base[0:2, :] + idx[2:4, :]


def _slots(pstart, idx, lanes):
    t = idx.shape[1]
    grid_spec = pltpu.PrefetchScalarGridSpec(
        num_scalar_prefetch=1, grid=(t // lanes,),
        in_specs=[pl.BlockSpec((8, lanes), lambda i, ps: (0, i))],
        out_specs=pl.BlockSpec((8, lanes), lambda i, ps: (0, i)),
    )
    return pl.pallas_call(
        _slots_body, grid_spec=grid_spec, out_shape=jax.ShapeDtypeStruct(idx.shape, jnp.int32),
        compiler_params=pltpu.CompilerParams(dimension_semantics=("parallel",)),
        name="slots",
    )(pstart, idx)


def _dispatch_body(pstart_ref, size_ref, slot_ref, hn_ref, xs_ref, zblk, sem):
    n_tok = hn_ref.shape[0]

    @pl.when(pl.program_id(0) == 0)
    def _():
        zblk[...] = jnp.zeros_like(zblk)

        def per_expert(e, carry):
            n_zero, n_rows = carry
            size = size_ref[e]
            n_pad = (MOE_BLOCK - size % MOE_BLOCK) % MOE_BLOCK
            base = pstart_ref[e] + size

            def issue_zero(r, c):
                _row_copy(zblk, 0, xs_ref, base + r, sem).start()
                return c

            lax.fori_loop(0, n_pad, issue_zero, 0)
            return n_zero + n_pad, n_rows + size + n_pad

        n_zero, n_rows = lax.fori_loop(0, N_EXPERTS, per_expert, (0, 0))

        def drain_zero(r, c):
            _row_copy(zblk, 0, xs_ref, 0, sem).wait()
            return c

        lax.fori_loop(0, n_zero, drain_zero, 0)

        def tail_copy(j):
            return pltpu.make_async_copy(zblk, xs_ref.at[pl.ds(pl.multiple_of(j * MOE_BLOCK, MOE_BLOCK), MOE_BLOCK)],
                                         sem)

        first_unused = n_rows // MOE_BLOCK
        n_blocks = xs_ref.shape[0] // MOE_BLOCK

        def issue_tail(j, c):
            tail_copy(j).start()
            return c

        def drain_tail(j, c):
            tail_copy(j).wait()
            return c

        lax.fori_loop(first_unused, n_blocks, issue_tail, 0)
        lax.fori_loop(first_unused, n_blocks, drain_tail, 0)

    def issue(t, carry):
        for k in range(2):
            _row_copy(hn_ref, t, xs_ref, slot_ref[k, t], sem).start()
        return carry

    lax.fori_loop(0, n_tok, issue, 0, unroll=ROW_UNROLL)
    for k in range(2):
        pltpu.make_async_copy(hn_ref, xs_ref.at[pl.ds(0, n_tok)], sem).wait()


def _dispatch(pstart, sizes, slots, hn, n_rows_out, tokens):
    t = hn.shape[0]
    grid_spec = pltpu.PrefetchScalarGridSpec(
        num_scalar_prefetch=2, grid=(t // tokens,),
        in_specs=[pl.BlockSpec((8, tokens), lambda i, ps, sz: (0, i), memory_space=pltpu.SMEM),
                  pl.BlockSpec((tokens, D_MODEL), lambda i, ps, sz: (i, 0))],
        out_specs=pl.BlockSpec(memory_space=pl.ANY),
        scratch_shapes=[pltpu.VMEM((MOE_BLOCK, D_MODEL), F32), pltpu.SemaphoreType.DMA(())],
    )
    return pl.pallas_call(
        _dispatch_body, grid_spec=grid_spec,
        out_shape=jax.ShapeDtypeStruct((n_rows_out, D_MODEL), F32),
        compiler_params=pltpu.CompilerParams(dimension_semantics=("arbitrary",)),
        name="dispatch",
    )(pstart, sizes, slots, hn)


def _experts_body(be_ref, nused_ref, xs_ref, w1_ref, w3_ref, w2_ref, yb_ref, w1b, w3b, w2b):
    j = pl.program_id(0)
    used = j < nused_ref[0]
    new_expert = jnp.logical_or(j == 0, be_ref[j] != be_ref[jnp.maximum(j - 1, 0)])

    @pl.when(jnp.logical_and(used, new_expert))
    def _():
        w1b[...] = w1_ref[...].astype(BF16)
        w3b[...] = w3_ref[...].astype(BF16)
        w2b[...] = w2_ref[...].astype(BF16)

    @pl.when(used)
    def _():
        xb = xs_ref[...].astype(BF16)
        h1 = jnp.dot(xb, w1b[...], preferred_element_type=F32)
        h3 = jnp.dot(xb, w3b[...], preferred_element_type=F32)
        hid = (_silu(h1) * h3).astype(BF16)
        yb_ref[...] = jnp.dot(hid, w2b[...], preferred_element_type=F32)

    @pl.when(jnp.logical_not(used))
    def _():
        yb_ref[...] = jnp.zeros_like(yb_ref)


def _experts(block_expert, n_used, xs, w1, w3, w2):
    p = xs.shape[0]
    nb = p // MOE_BLOCK
    row_map = lambda j, be, nu: (jnp.minimum(j, nu[0] - 1), 0)
    w_map = lambda j, be, nu: (be[j], 0, 0)
    grid_spec = pltpu.PrefetchScalarGridSpec(
        num_scalar_prefetch=2, grid=(nb,),
        in_specs=[pl.BlockSpec((MOE_BLOCK, D_MODEL), row_map),
                  pl.BlockSpec((None, D_MODEL, D_EXPERT), w_map),
                  pl.BlockSpec((None, D_MODEL, D_EXPERT), w_map),
                  pl.BlockSpec((None, D_EXPERT, D_MODEL), w_map)],
        out_specs=pl.BlockSpec((MOE_BLOCK, D_MODEL), lambda j, be, nu: (j, 0)),
        scratch_shapes=[pltpu.VMEM((D_MODEL, D_EXPERT), BF16), pltpu.VMEM((D_MODEL, D_EXPERT), BF16),
                        pltpu.VMEM((D_EXPERT, D_MODEL), BF16)],
    )
    return pl.pallas_call(
        _experts_body, grid_spec=grid_spec,
        out_shape=jax.ShapeDtypeStruct((p, D_MODEL), F32),
        compiler_params=pltpu.CompilerParams(dimension_semantics=("arbitrary",), vmem_limit_bytes=VMEM_LIMIT),
        name="experts",
    )(block_expert, n_used, xs, w1, w3, w2)


def _combine_body(slot_ref, slot_next_ref, gate_ref, h2_ref, gf_ref, yb_ref, o_ref, ybuf, sems):
    n_tok = h2_ref.shape[0]
    step = pl.program_id(0)
    cur = step % 2

    def gather(slots, buf_id):
        def issue(t, carry):
            for k in range(2):
                pltpu.make_async_copy(yb_ref.at[pl.ds(slots[k, t], 1)], ybuf.at[buf_id, k, pl.ds(t, 1)],
                                      sems.at[buf_id]).start()
            return carry

        lax.fori_loop(0, n_tok, issue, 0, unroll=ROW_UNROLL)

    @pl.when(step == 0)
    def _():
        gather(slot_ref, 0)

    @pl.when(step + 1 < pl.num_programs(0))
    def _():
        gather(slot_next_ref, 1 - cur)

    for k in range(2):
        pltpu.make_async_copy(yb_ref.at[pl.ds(0, n_tok)], ybuf.at[cur, k], sems.at[cur]).wait()
    y = h2_ref[...] + gate_ref[:, 0:1] * ybuf[cur, 0] + gate_ref[:, 1:2] * ybuf[cur, 1]
    o_ref[...] = _rms(y) * gf_ref[...]


def _combine(slots, gate_t, h2, gf, yb, tokens):
    t = h2.shape[0]
    last = t // tokens - 1
    return pl.pallas_call(
        _combine_body, grid=(t // tokens,),
        in_specs=[pl.BlockSpec((8, tokens), lambda i: (0, i), memory_space=pltpu.SMEM),
                  pl.BlockSpec((8, tokens), lambda i: (0, jnp.minimum(i + 1, last)), memory_space=pltpu.SMEM),
                  pl.BlockSpec((tokens, 8), lambda i: (i, 0)),
                  pl.BlockSpec((tokens, D_MODEL), lambda i: (i, 0)),
                  pl.BlockSpec((1, D_MODEL), lambda i: (0, 0)),
                  pl.BlockSpec(memory_space=pl.ANY)],
        out_specs=pl.BlockSpec((tokens, D_MODEL), lambda i: (i, 0)),
        scratch_shapes=[pltpu.VMEM((2, 2, tokens, D_MODEL), F32), pltpu.SemaphoreType.DMA((2,))],
        out_shape=jax.ShapeDtypeStruct((t, D_MODEL), F32),
        compiler_params=pltpu.CompilerParams(dimension_semantics=("arbitrary",)),
        name="combine",
    )(slots, slots, gate_t, h2, gf, yb)


def _pad_lanes(v, width=LANES):
    v = v.reshape(1, -1).astype(F32)
    return jnp.pad(v, ((0, 0), (0, width - v.shape[1])))


def _largest_tile(n, cap):
    t = cap
    while n % t:
        t //= 2
    return t


def kernel(x, meta_tokens, norm_mix, w_in, conv_dn, a_log, dt_bias, norm_head_dn, w_proj_dn, w_alpha, b_alpha,
           norm_head_gla, w_proj_gla, w_out, norm_ffn, w_group, b_group, w_router, b_router, w1, w3, w2, norm_final):
    assert norm_mix.shape[0] == 1, "single-layer block"
    bsz, seq, d = x.shape
    assert d == D_MODEL and seq % CHUNK == 0
    t = bsz * seq

    wi = w_in[0]
    offs = np.cumsum([0, DN_QK, DN_QK, DN_V, DN_V, DN_HEADS, DN_HEADS, GLA_QK, GLA_QK, GLA_V, GLA_V, GLA_RANK,
                      D_MODEL, D_MODEL]).tolist()
    seg = lambda i, j: wi[:, offs[i]:offs[j]]
    n_small = 2 * DN_HEADS + GLA_RANK
    w_all = jnp.concatenate([seg(0, 4), seg(6, 10), seg(11, 13), seg(4, 6), seg(10, 11),
                             jnp.zeros((D_MODEL, SMALL_W - n_small), F32)], axis=1).astype(BF16)
    g_mix = norm_mix[0].reshape(1, D_MODEL)

    conv_w = jnp.pad(conv_dn[0], ((0, 8 - CONV_WIDTH), (0, 0)))
    alog = _pad_lanes(a_log[0])
    dtb = _pad_lanes(dt_bias[0])
    nh_dn = norm_head_dn[0].reshape(1, DN_HEAD_DIM)
    lr_off = 2 * DN_HEADS
    wa = jnp.zeros((SMALL_W, GLA_QK), F32).at[lr_off:lr_off + GLA_RANK].set(w_alpha[0])
    ba = b_alpha[0].reshape(1, GLA_QK)
    nh_gla = norm_head_gla[0].reshape(1, GLA_VAL_DIM)

    def mixers(tokens2d, nb, hist, s0_dn, s0_gla, rows):
        qkv, z, qk_g, v_g, r_g, gates, small = _in_proj(tokens2d, g_mix, w_all, rows)
        r3 = lambda a: a.reshape(nb, -1, a.shape[-1])
        o_dn, s_dn = _dn_chunk(r3(qkv), r3(small), r3(z), conv_w, hist, s0_dn, alog, dtb, nh_dn)
        o_gla, s_gla = _gla_chunk(r3(qk_g), r3(v_g), r3(r_g), r3(small), wa, ba, nh_gla, s0_gla)
        return qkv, o_dn, o_gla, gates, s_dn, s_gla

    meta_rows = jnp.pad(meta_tokens.astype(F32), ((CHUNK - N_META, 0), (0, 0)))
    zero_hist = jnp.zeros((HIST_ROWS, 3 * DN_QK), F32)
    zero_dn = jnp.zeros((DN_HEADS, DN_HEAD_DIM, DN_HEAD_DIM), F32)
    zero_gla = jnp.zeros((GLA_HEADS, GLA_VAL_DIM, GLA_KEY_DIM), F32)
    qkv_m, _, _, _, s_dn_m, s_gla_m = mixers(meta_rows, 1, zero_hist, zero_dn, zero_gla, CHUNK)
    hist = qkv_m[CHUNK - HIST_ROWS:].astype(F32)

    x2d = x.reshape(t, d)
    rows = _largest_tile(t, 512)
    _, o_dn, o_gla, gates, _, _ = mixers(x2d, bsz, hist, s_dn_m[0], s_gla_m[0], rows)

    wr_t = jnp.concatenate([w_group[0], w_router[0]], axis=1).T
    wr_t = jnp.pad(wr_t, ((0, LANES - wr_t.shape[0]), (0, 0)))
    br = jnp.pad(jnp.concatenate([b_group[0], b_router[0]]), (0, LANES - N_GROUPS - N_EXPERTS)).reshape(LANES, 1)
    h2, hn, logits_t = _out_proj(
        o_dn.reshape(t, DN_V), o_gla.reshape(t, GLA_V), gates, x2d,
        w_proj_dn[0].astype(BF16), w_proj_gla[0].astype(BF16), w_out[0].astype(BF16),
        norm_ffn[0].reshape(1, D_MODEL), wr_t, br, rows)

    idx, gate, cnt = _route(logits_t, _largest_tile(t, 512))

    sizes = cnt[:, 0].astype(jnp.int32)
    padded = (sizes + MOE_BLOCK - 1) // MOE_BLOCK * MOE_BLOCK
    pends = jnp.cumsum(padded)
    pstart = (pends - padded).astype(jnp.int32)
    n_blocks = (2 * t) // MOE_BLOCK + N_EXPERTS
    n_used = (pends[-1:] // MOE_BLOCK).astype(jnp.int32)
    block_row0 = jnp.arange(n_blocks, dtype=jnp.int32) * MOE_BLOCK
    block_expert = jnp.minimum(jnp.sum(pends[None, :] <= block_row0[:, None], axis=1), N_EXPERTS - 1).astype(jnp.int32)

    tok_tile = _largest_tile(t, 256)
    slots = _slots(pstart, idx, _largest_tile(t, 2048))
    xs = _dispatch(pstart, sizes, slots, hn, n_blocks * MOE_BLOCK, tok_tile)
    yb = _experts(block_expert, n_used, xs, w1[0], w3[0], w2[0])
    out = _combine(slots, gate.T, h2, norm_final.reshape(1, D_MODEL), yb, tok_tile)
    return out.reshape(bsz, seq, d)
```

```python
import functools

import jax
import jax.numpy as jnp
import numpy as np
from jax import lax
from jax.experimental import pallas as pl
from jax.experimental.pallas import tpu as pltpu

F32 = jnp.float32
BF16 = jnp.bfloat16

D_MODEL = 1024
CHUNK = 64
N_META = 16
EPS = 1e-6
DN_HEADS = 8
DN_HEAD_DIM = 128
DN_QK = DN_HEADS * DN_HEAD_DIM
DN_V = DN_HEADS * DN_HEAD_DIM
CONV_WIDTH = 4
GLA_HEADS = 4
GLA_KEY_DIM = 128
GLA_VAL_DIM = 256
GLA_QK = GLA_HEADS * GLA_KEY_DIM
GLA_V = GLA_HEADS * GLA_VAL_DIM
GLA_RANK = 16
GLA_TAU = 16.0
N_GROUPS = 8
EXPERTS_PER_GROUP = 8
N_EXPERTS = N_GROUPS * EXPERTS_PER_GROUP
D_EXPERT = 512
MOE_BLOCK = 512
ZERO_ROWS = 256

LANES = 128
SMALL_W = LANES
HIST_ROWS = 8
VMEM_LIMIT = 56 * 1024 * 1024

_NT = (((1,), (1,)), ((), ()))
_TN = (((0,), (0,)), ((), ()))


def _bdot(a, b):
    return jnp.dot(a.astype(BF16), b.astype(BF16), preferred_element_type=F32)


def _bdot_nt(a, b):
    return lax.dot_general(a.astype(BF16), b.astype(BF16), _NT, preferred_element_type=F32)


def _bdot_tn(a, b):
    return lax.dot_general(a.astype(BF16), b.astype(BF16), _TN, preferred_element_type=F32)


def _sigmoid(x):
    return 1.0 / (1.0 + jnp.exp(-x))


def _silu(x):
    half = 0.5 * x
    return half + half * jnp.tanh(half)


def _softplus(x):
    return jnp.maximum(x, 0.0) + jnp.log(1.0 + jnp.exp(-jnp.abs(x)))


def _rms(x, eps=EPS):
    return x * lax.rsqrt(jnp.mean(x * x, axis=-1, keepdims=True) + eps)


def _tri_incl(n):
    r = lax.broadcasted_iota(jnp.int32, (n, n), 0)
    c = lax.broadcasted_iota(jnp.int32, (n, n), 1)
    return r, c


def _bf16_part(x):
    bits = pltpu.bitcast(x, jnp.uint32) & jnp.uint32(0xFFFF0000)
    return pltpu.bitcast(bits, F32)


def _cumsum_rows(x):
    r, c = _tri_incl(x.shape[0])
    tri = (r >= c).astype(F32)
    hi = _bf16_part(x)
    r1 = x - hi
    mid = _bf16_part(r1)
    lo = r1 - mid
    return (jnp.dot(tri, hi, preferred_element_type=F32) + jnp.dot(tri, mid, preferred_element_type=F32)
            + jnp.dot(tri, lo, preferred_element_type=F32))


IN_GROUPS = ((3 * DN_QK, BF16), (DN_V, BF16), (2 * GLA_QK, BF16), (GLA_V, BF16), (GLA_V, BF16),
             (2 * D_MODEL, BF16), (SMALL_W, F32))


def _inproj_body(x_ref, g_ref, w_ref, *o_refs):
    x = x_ref[...]
    xb = (_rms(x) * g_ref[...]).astype(BF16)
    col = 0
    for o_ref in o_refs:
        width = o_ref.shape[1]
        o_ref[...] = jnp.dot(xb, w_ref[:, col:col + width], preferred_element_type=F32).astype(o_ref.dtype)
        col += width


def _in_proj(x2d, gain, w_all, rows):
    t = x2d.shape[0]
    in_specs = [pl.BlockSpec((rows, D_MODEL), lambda i: (i, 0)), pl.BlockSpec((1, D_MODEL), lambda i: (0, 0)),
                pl.BlockSpec(w_all.shape, lambda i: (0, 0), pipeline_mode=pl.Buffered(1))]
    out_specs = [pl.BlockSpec((rows, w), lambda i: (i, 0)) for w, _ in IN_GROUPS]
    out_shape = [jax.ShapeDtypeStruct((t, w), dt) for w, dt in IN_GROUPS]
    return pl.pallas_call(
        _inproj_body, grid=(t // rows,), in_specs=in_specs, out_specs=out_specs, out_shape=out_shape,
        compiler_params=pltpu.CompilerParams(dimension_semantics=("parallel",), vmem_limit_bytes=VMEM_LIMIT),
        name="in_proj",
    )(x2d, gain, w_all)


def _unit_lower_inverse(mats):
    n = mats[0].shape[0]
    r, c = _tri_incl(n)
    eye = (r == c).astype(F32)
    pair = (r >> 1) == (c >> 1)
    ts = [eye - jnp.where(pair, a, 0.0) for a in mats]
    m = 2
    while m < n:
        sh = m.bit_length() - 1
        keep = ((r >> (sh + 1)) == (c >> (sh + 1))) & ((r >> sh) != (c >> sh))
        ams = [jnp.where(keep, a, 0.0) for a in mats]
        xs = [jnp.dot(t, am, preferred_element_type=F32) for t, am in zip(ts, ams)]
        ys = [jnp.dot(x, t, preferred_element_type=F32) for x, t in zip(xs, ts)]
        ts = [t - y for t, y in zip(ts, ys)]
        m *= 2
    return ts


def _l2_normalise(xs, ones_bd, scale):
    out = []
    for p in range(0, len(xs), 2):
        sq = jnp.concatenate([xs[p] * xs[p], xs[p + 1] * xs[p + 1]], axis=1)
        ss = jnp.dot(sq, ones_bd, preferred_element_type=F32)
        inv = lax.rsqrt(ss + EPS) * scale
        out.append(xs[p] * inv[:, :DN_HEAD_DIM])
        out.append(xs[p + 1] * inv[:, DN_HEAD_DIM:])
    return out


def _dn_body(qkv_ref, sm_ref, z_ref, cw_ref, hist_ref, s0_ref, alog_ref, dtb_ref, nh_ref, ones_ref,
             o_ref, sfin_ref, buf, s_scr):
    c_id = pl.program_id(1)
    n_rows = qkv_ref.shape[0]
    items = [(b, h) for b in range(n_rows) for h in range(DN_HEADS)]
    n_items = range(len(items))

    @pl.when(c_id == 0)
    def _():
        for b in range(n_rows):
            buf[b, 0:HIST_ROWS, :] = hist_ref[...]
            s_scr[b] = s0_ref[...]

    for b in range(n_rows):
        buf[b, HIST_ROWS:HIST_ROWS + CHUNK, :] = qkv_ref[b].astype(F32)

    def conv_silu(b, col):
        acc = None
        for j in range(CONV_WIDTH):
            lo = HIST_ROWS - (CONV_WIDTH - 1) + j
            term = buf[b, lo:lo + CHUNK, col:col + DN_HEAD_DIM] * cw_ref[j:j + 1, col:col + DN_HEAD_DIM]
            acc = term if acc is None else acc + term
        return _silu(acc)

    gc_all, gc_t, beta_all = [], [], []
    for b in range(n_rows):
        sm = sm_ref[b]
        g_b = -jnp.exp(alog_ref[...]) * _softplus(sm + dtb_ref[...])
        beta_all.append(_sigmoid(sm))
        gc_b = _cumsum_rows(g_b)
        gc_all.append(gc_b)
        gc_t.append(jnp.concatenate([gc_b, jnp.zeros_like(gc_b)], axis=0).T)

    r, c = _tri_incl(CHUNK)
    causal = r >= c
    strict = r > c
    scale = DN_HEAD_DIM ** -0.5

    q = [conv_silu(b, h * DN_HEAD_DIM) for b, h in items]
    k = [conv_silu(b, DN_QK + h * DN_HEAD_DIM) for b, h in items]
    v = [conv_silu(b, 2 * DN_QK + h * DN_HEAD_DIM) for b, h in items]
    ones_bd = ones_ref[...]
    q = _l2_normalise(q, ones_bd, scale)
    k = _l2_normalise(k, ones_bd, 1.0)
    gcol = [gc_all[b][:, h:h + 1] for b, h in items]
    grow = [gc_t[b][h:h + 1, 0:CHUNK] for b, h in items]
    bcol = [beta_all[b][:, DN_HEADS + h:DN_HEADS + h + 1] for b, h in items]
    glast = [gc_all[b][CHUNK - 1:CHUNK, h:h + 1] for b, h in items]
    decay = [jnp.where(causal, jnp.exp(jnp.where(causal, gcol[i] - grow[i], 0.0)), 0.0) for i in n_items]
    eg = [jnp.exp(gcol[i]) for i in n_items]
    kk = [lax.dot_general(k[i], k[i], _NT, preferred_element_type=F32) for i in n_items]
    qk = [lax.dot_general(q[i], k[i], _NT, preferred_element_type=F32) for i in n_items]
    a_mat = [jnp.where(strict, bcol[i] * kk[i] * decay[i], 0.0) for i in n_items]
    qk = [jnp.where(causal, qk[i] * decay[i], 0.0) for i in n_items]
    t_inv = _unit_lower_inverse(a_mat)
    rhs = [jnp.concatenate([v[i] * bcol[i], k[i] * (bcol[i] * eg[i])], axis=1) for i in n_items]
    sol = [jnp.dot(t_inv[i], rhs[i], preferred_element_type=F32) for i in n_items]
    s_old = [s_scr[b, h] for b, h in items]
    qe = [q[i] * eg[i] for i in n_items]
    k_dec = [k[i] * jnp.exp(glast[i] - gcol[i]) for i in n_items]
    ws = [jnp.dot(sol[i][:, DN_HEAD_DIM:], s_old[i], preferred_element_type=F32) for i in n_items]
    o_inter = [jnp.dot(qe[i], s_old[i], preferred_element_type=F32) for i in n_items]
    v_new = [sol[i][:, :DN_HEAD_DIM] - ws[i] for i in n_items]
    o_intra = [jnp.dot(qk[i], v_new[i], preferred_element_type=F32) for i in n_items]
    s_add = [lax.dot_general(k_dec[i], v_new[i], _TN, preferred_element_type=F32) for i in n_items]
    for i, (b, h) in enumerate(items):
        col = h * DN_HEAD_DIM
        s_scr[b, h] = s_old[i] * jnp.exp(glast[i]) + s_add[i]
        zed = z_ref[b, :, col:col + DN_HEAD_DIM].astype(F32)
        o = o_inter[i] + o_intra[i]
        o_ref[b, :, col:col + DN_HEAD_DIM] = (_rms(o) * nh_ref[...] * _silu(zed)).astype(o_ref.dtype)

    for b in range(n_rows):
        buf[b, 0:HIST_ROWS, :] = buf[b, CHUNK:CHUNK + HIST_ROWS, :]

    @pl.when(c_id == pl.num_programs(1) - 1)
    def _():
        sfin_ref[...] = s_scr[...]


def _rows_per_step(b, want):
    while b % want:
        want //= 2
    return want


def _dn_chunk(qkv, small, z, conv_w, hist, s0, alog, dtb, nh):
    b, l, _ = qkv.shape
    lane_head = np.arange(2 * DN_HEAD_DIM) // DN_HEAD_DIM
    ones_bd = jnp.asarray(lane_head[:, None] == lane_head[None, :], F32)
    nc = l // CHUNK
    nr = _rows_per_step(b, 2)
    const2 = lambda bi, ci: (0, 0)
    blk = lambda w: pl.BlockSpec((nr, CHUNK, w), lambda bi, ci: (bi, ci, 0))
    return pl.pallas_call(
        _dn_body, grid=(b // nr, nc),
        in_specs=[
            blk(3 * DN_QK), blk(SMALL_W), blk(DN_V),
            pl.BlockSpec(conv_w.shape, const2),
            pl.BlockSpec(hist.shape, const2),
            pl.BlockSpec(s0.shape, lambda bi, ci: (0, 0, 0)),
            pl.BlockSpec(alog.shape, const2),
            pl.BlockSpec(dtb.shape, const2),
            pl.BlockSpec(nh.shape, const2),
            pl.BlockSpec(ones_bd.shape, const2),
        ],
        out_specs=[
            blk(DN_V),
            pl.BlockSpec((nr, DN_HEADS, DN_HEAD_DIM, DN_HEAD_DIM), lambda bi, ci: (bi, 0, 0, 0)),
        ],
        out_shape=[
            jax.ShapeDtypeStruct((b, l, DN_V), BF16),
            jax.ShapeDtypeStruct((b, DN_HEADS, DN_HEAD_DIM, DN_HEAD_DIM), F32),
        ],
        scratch_shapes=[
            pltpu.VMEM((nr, HIST_ROWS + CHUNK, 3 * DN_QK), F32),
            pltpu.VMEM((nr, DN_HEADS, DN_HEAD_DIM, DN_HEAD_DIM), F32),
        ],
        compiler_params=pltpu.CompilerParams(dimension_semantics=("parallel", "arbitrary"),
                                             vmem_limit_bytes=VMEM_LIMIT),
        name="dn_chunk",
    )(qkv, small, z, conv_w, hist, s0, alog, dtb, nh, ones_bd)


def _gla_body(qk_ref, v_ref, r_ref, sm_ref, wa_ref, ba_ref, nh_ref, s0_ref, o_ref, sfin_ref, s_scr):
    c_id = pl.program_id(1)
    n_rows = qk_ref.shape[0]
    items = [(b, h) for b in range(n_rows) for h in range(GLA_HEADS)]
    n_items = range(len(items))

    @pl.when(c_id == 0)
    def _():
        for b in range(n_rows):
            s_scr[b] = s0_ref[...]

    b_all = []
    for b in range(n_rows):
        la = jnp.dot(sm_ref[b], wa_ref[...], preferred_element_type=F32) + ba_ref[...]
        log_alpha = (jnp.minimum(la, 0.0) - jnp.log(1.0 + jnp.exp(-jnp.abs(la)))) * (1.0 / GLA_TAU)
        b_all.append(_cumsum_rows(log_alpha))

    r, c = _tri_incl(CHUNK)
    causal = r >= c
    scale = GLA_KEY_DIM ** -0.5
    mid = CHUNK // 2 - 1

    q = [qk_ref[b, :, h * GLA_KEY_DIM:(h + 1) * GLA_KEY_DIM].astype(F32) * scale for b, h in items]
    k = [qk_ref[b, :, GLA_QK + h * GLA_KEY_DIM:GLA_QK + (h + 1) * GLA_KEY_DIM].astype(F32) for b, h in items]
    v = [v_ref[b, :, h * GLA_VAL_DIM:(h + 1) * GLA_VAL_DIM].astype(F32) for b, h in items]
    bh = [b_all[b][:, h * GLA_KEY_DIM:(h + 1) * GLA_KEY_DIM] for b, h in items]
    bmid = [x[mid:mid + 1, :] for x in bh]
    blast = [x[CHUNK - 1:CHUNK, :] for x in bh]
    qs = [q[i] * jnp.exp(bh[i] - bmid[i]) for i in n_items]
    ks = [k[i] * jnp.exp(bmid[i] - bh[i]) for i in n_items]
    qd = [q[i] * jnp.exp(bh[i]) for i in n_items]
    kd = [k[i] * jnp.exp(blast[i] - bh[i]) for i in n_items]
    st = [s_scr[b, h] for b, h in items]
    att = [lax.dot_general(qs[i], ks[i], _NT, preferred_element_type=F32) for i in n_items]
    o_inter = [lax.dot_general(qd[i], st[i], _NT, preferred_element_type=F32) for i in n_items]
    s_add = [lax.dot_general(v[i], kd[i], _TN, preferred_element_type=F32) for i in n_items]
    att = [jnp.where(causal, a, 0.0) for a in att]
    o_intra = [jnp.dot(att[i], v[i], preferred_element_type=F32) for i in n_items]
    for i, (b, h) in enumerate(items):
        vc = h * GLA_VAL_DIM
        s_scr[b, h] = st[i] * jnp.exp(blast[i]) + s_add[i]
        gate = _silu(r_ref[b, :, vc:vc + GLA_VAL_DIM].astype(F32))
        o = o_inter[i] + o_intra[i]
        o_ref[b, :, vc:vc + GLA_VAL_DIM] = (_rms(o) * nh_ref[...] * gate).astype(o_ref.dtype)

    @pl.when(c_id == pl.num_programs(1) - 1)
    def _():
        sfin_ref[...] = s_scr[...]


def _gla_chunk(qk, v, rr, small, wa, ba, nh, s0):
    b, l, _ = qk.shape
    nc = l // CHUNK
    nr = _rows_per_step(b, 4)
    const2 = lambda bi, ci: (0, 0)
    blk = lambda w: pl.BlockSpec((nr, CHUNK, w), lambda bi, ci: (bi, ci, 0))
    return pl.pallas_call(
        _gla_body, grid=(b // nr, nc),
        in_specs=[
            blk(2 * GLA_QK), blk(GLA_V), blk(GLA_V), blk(SMALL_W),
            pl.BlockSpec(wa.shape, const2), pl.BlockSpec(ba.shape, const2), pl.BlockSpec(nh.shape, const2),
            pl.BlockSpec(s0.shape, lambda bi, ci: (0, 0, 0)),
        ],
        out_specs=[
            blk(GLA_V),
            pl.BlockSpec((nr, GLA_HEADS, GLA_VAL_DIM, GLA_KEY_DIM), lambda bi, ci: (bi, 0, 0, 0)),
        ],
        out_shape=[
            jax.ShapeDtypeStruct((b, l, GLA_V), BF16),
            jax.ShapeDtypeStruct((b, GLA_HEADS, GLA_VAL_DIM, GLA_KEY_DIM), F32),
        ],
        scratch_shapes=[pltpu.VMEM((nr, GLA_HEADS, GLA_VAL_DIM, GLA_KEY_DIM), F32)],
        compiler_params=pltpu.CompilerParams(dimension_semantics=("parallel", "arbitrary"),
                                             vmem_limit_bytes=VMEM_LIMIT),
        name="gla_chunk",
    )(qk, v, rr, small, wa, ba, nh, s0)


def _outproj_body(odn_ref, ogla_ref, gates_ref, x_ref, wd_ref, wg_ref, wo_ref, gn_ref, wr_ref, br_ref,
                  h2_ref, hn_ref, lg_ref):
    y_dn = jnp.dot(odn_ref[...], wd_ref[...], preferred_element_type=F32)
    y_gla = jnp.dot(ogla_ref[...], wg_ref[...], preferred_element_type=F32)
    gd = _sigmoid(gates_ref[:, 0:D_MODEL].astype(F32))
    gg = _sigmoid(gates_ref[:, D_MODEL:2 * D_MODEL].astype(F32))
    merged = gd * y_dn + gg * y_gla
    h2 = x_ref[...] + _bdot(merged, wo_ref[...])
    h2_ref[...] = h2
    hn = _rms(h2) * gn_ref[...]
    hn_ref[...] = hn
    lg_ref[...] = lax.dot_general(wr_ref[...], hn, _NT, preferred_element_type=F32,
                                  precision=lax.Precision.HIGHEST) + br_ref[...]


def _out_proj(o_dn, o_gla, gates, x2d, wd, wg, wo, gn, wr_t, br, rows):
    t = x2d.shape[0]
    row_blk = lambda w: pl.BlockSpec((rows, w), lambda i: (i, 0))
    const = lambda a: pl.BlockSpec(a.shape, lambda i: (0, 0))
    return pl.pallas_call(
        _outproj_body, grid=(t // rows,),
        in_specs=[row_blk(DN_V), row_blk(GLA_V), row_blk(2 * D_MODEL), row_blk(D_MODEL),
                  const(wd), const(wg), const(wo), const(gn), const(wr_t), const(br)],
        out_specs=[row_blk(D_MODEL), row_blk(D_MODEL), pl.BlockSpec((LANES, rows), lambda i: (0, i))],
        out_shape=[jax.ShapeDtypeStruct((t, D_MODEL), F32), jax.ShapeDtypeStruct((t, D_MODEL), F32),
                   jax.ShapeDtypeStruct((LANES, t), F32)],
        compiler_params=pltpu.CompilerParams(dimension_semantics=("parallel",), vmem_limit_bytes=VMEM_LIMIT),
        name="out_proj",
    )(o_dn, o_gla, gates, x2d, wd, wg, wo, gn, wr_t, br)


ROUTE_SUB = 256


def _route_body(lg_ref, idx_ref, gate_ref, cnt_ref, carry):
    step = pl.program_id(0)

    @pl.when(step == 0)
    def _():
        carry[...] = jnp.zeros_like(carry)

    tt = lg_ref.shape[1]
    gl = lg_ref[0:N_GROUPS, :]
    gmax = jnp.max(gl, axis=0, keepdims=True)
    rid8 = lax.broadcasted_iota(jnp.int32, (N_GROUPS, tt), 0)
    gsel = jnp.min(jnp.where(gl == gmax, rid8, N_GROUPS), axis=0, keepdims=True)
    gw = 1.0 / jnp.sum(jnp.exp(gl - gmax), axis=0, keepdims=True)
    el = lg_ref[N_GROUPS:N_GROUPS + N_EXPERTS, :]
    rid = lax.broadcasted_iota(jnp.int32, (N_EXPERTS, tt), 0)
    neg = jnp.float32(-jnp.inf)
    ein = jnp.where((rid >> 3) == gsel, el, neg)
    t1 = jnp.max(ein, axis=0, keepdims=True)
    i1 = jnp.min(jnp.where(ein == t1, rid, N_EXPERTS), axis=0, keepdims=True)
    ein2 = jnp.where(rid == i1, neg, ein)
    t2 = jnp.max(ein2, axis=0, keepdims=True)
    i2 = jnp.min(jnp.where(ein2 == t2, rid, N_EXPERTS), axis=0, keepdims=True)
    e21 = jnp.exp(t2 - t1)
    den = 1.0 / (1.0 + e21)
    sel1 = rid == i1
    sel2 = rid == i2
    onehot = jnp.where(sel1 | sel2, 1.0, 0.0)

    ur, uc = _tri_incl(ROUTE_SUB)
    upper = (ur <= uc).astype(BF16)
    run = carry[...]
    r1_parts, r2_parts = [], []
    for s in range(tt // ROUTE_SUB):
        sl = slice(s * ROUTE_SUB, (s + 1) * ROUTE_SUB)
        oh = onehot[:, sl]
        incl = jnp.dot(oh.astype(BF16), upper, preferred_element_type=F32) + run
        excl = incl - oh
        r1_parts.append(jnp.sum(jnp.where(sel1[:, sl], excl, 0.0), axis=0, keepdims=True))
        r2_parts.append(jnp.sum(jnp.where(sel2[:, sl], excl, 0.0), axis=0, keepdims=True))
        run = jnp.broadcast_to(incl[:, ROUTE_SUB - 1:ROUTE_SUB], run.shape)
    carry[...] = run
    r1 = jnp.concatenate(r1_parts, axis=1) if len(r1_parts) > 1 else r1_parts[0]
    r2 = jnp.concatenate(r2_parts, axis=1) if len(r2_parts) > 1 else r2_parts[0]

    idx_ref[...] = jnp.zeros_like(idx_ref)
    idx_ref[0:1, :] = i1
    idx_ref[1:2, :] = i2
    idx_ref[2:3, :] = r1.astype(jnp.int32)
    idx_ref[3:4, :] = r2.astype(jnp.int32)
    gate_ref[...] = jnp.zeros_like(gate_ref)
    gate_ref[0:1, :] = den * gw
    gate_ref[1:2, :] = e21 * den * gw
    cnt_ref[...] = run[:, 0:LANES]


def _route(logits_t, lanes):
    t = logits_t.shape[1]
    return pl.pallas_call(
        _route_body, grid=(t // lanes,),
        in_specs=[pl.BlockSpec((LANES, lanes), lambda i: (0, i))],
        out_specs=[pl.BlockSpec((8, lanes), lambda i: (0, i)), pl.BlockSpec((8, lanes), lambda i: (0, i)),
                   pl.BlockSpec((N_EXPERTS, LANES), lambda i: (0, 0))],
        out_shape=[jax.ShapeDtypeStruct((8, t), jnp.int32), jax.ShapeDtypeStruct((8, t), F32),
                   jax.ShapeDtypeStruct((N_EXPERTS, LANES), F32)],
        scratch_shapes=[pltpu.VMEM((N_EXPERTS, ROUTE_SUB), F32)],
        compiler_params=pltpu.CompilerParams(dimension_semantics=("arbitrary",)),
        name="route",
    )(logits_t)


def _row_copy(src_ref, src_row, dst_ref, dst_row, sem):
    return pltpu.make_async_copy(src_ref.at[pl.ds(src_row, 1)], dst_ref.at[pl.ds(dst_row, 1)], sem)


ROW_UNROLL = 8


def _slots_body(pstart_ref, idx_ref, slot_ref):
    idx = idx_ref[...]
    base = jnp.zeros_like(idx)
    for e in range(N_EXPERTS):
        base = jnp.where(idx == e, pstart_ref[e], base)
    slot_ref[...] = jnp.zeros_like(idx)
    slot_ref[0:2, :] = base[0:2, :] + idx[2:4, :]


def _slots(pstart, idx, lanes):
    t = idx.shape[1]
    grid_spec = pltpu.PrefetchScalarGridSpec(
        num_scalar_prefetch=1, grid=(t // lanes,),
        in_specs=[pl.BlockSpec((8, lanes), lambda i, ps: (0, i))],
        out_specs=pl.BlockSpec((8, lanes), lambda i, ps: (0, i)),
    )
    return pl.pallas_call(
        _slots_body, grid_spec=grid_spec, out_shape=jax.ShapeDtypeStruct(idx.shape, jnp.int32),
        compiler_params=pltpu.CompilerParams(dimension_semantics=("parallel",)),
        name="slots",
    )(pstart, idx)


def _dispatch_body(pstart_ref, size_ref, slot_ref, hn_ref, xs_ref, zblk, sem):
    n_tok = hn_ref.shape[0]

    @pl.when(pl.program_id(0) == 0)
    def _():
        zblk[...] = jnp.zeros_like(zblk)
        parts = MOE_BLOCK // ZERO_ROWS

        def clear_copy(row0, part):
            dst = xs_ref.at[pl.ds(pl.multiple_of(row0 + part * ZERO_ROWS, ZERO_ROWS), ZERO_ROWS)]
            return pltpu.make_async_copy(zblk, dst, sem)

        def per_expert(e, n_rows):
            size = size_ref[e]
            full = size // MOE_BLOCK * MOE_BLOCK

            @pl.when(size != full)
            def _():
                for part in range(parts):
                    clear_copy(pstart_ref[e] + full, part).start()
                for part in range(parts):
                    clear_copy(pstart_ref[e] + full, part).wait()

            return n_rows + (size + MOE_BLOCK - 1) // MOE_BLOCK * MOE_BLOCK

        n_rows = lax.fori_loop(0, N_EXPERTS, per_expert, 0)

        def clear_unused(j, c):
            for part in range(parts):
                clear_copy(j * MOE_BLOCK, part).start()
            for part in range(parts):
                clear_copy(j * MOE_BLOCK, part).wait()
            return c

        lax.fori_loop(n_rows // MOE_BLOCK, xs_ref.shape[0] // MOE_BLOCK, clear_unused, 0)

    def issue(t, carry):
        for k in range(2):
            _row_copy(hn_ref, t, xs_ref, slot_ref[k, t], sem).start()
        return carry

    lax.fori_loop(0, n_tok, issue, 0, unroll=ROW_UNROLL)
    for k in range(2):
        pltpu.make_async_copy(hn_ref, xs_ref.at[pl.ds(0, n_tok)], sem).wait()


def _dispatch(pstart, sizes, slots, hn, n_rows_out, tokens):
    t = hn.shape[0]
    grid_spec = pltpu.PrefetchScalarGridSpec(
        num_scalar_prefetch=2, grid=(t // tokens,),
        in_specs=[pl.BlockSpec((8, tokens), lambda i, ps, sz: (0, i), memory_space=pltpu.SMEM),
                  pl.BlockSpec((tokens, D_MODEL), lambda i, ps, sz: (i, 0))],
        out_specs=pl.BlockSpec(memory_space=pl.ANY),
        scratch_shapes=[pltpu.VMEM((ZERO_ROWS, D_MODEL), F32), pltpu.SemaphoreType.DMA(())],
    )
    return pl.pallas_call(
        _dispatch_body, grid_spec=grid_spec,
        out_shape=jax.ShapeDtypeStruct((n_rows_out, D_MODEL), F32),
        compiler_params=pltpu.CompilerParams(dimension_semantics=("arbitrary",)),
        name="dispatch",
    )(pstart, sizes, slots, hn)


def _experts_body(be_ref, nused_ref, xs_ref, w1_ref, w3_ref, w2_ref, yb_ref, w1b, w3b, w2b):
    j = pl.program_id(0)
    used = j < nused_ref[0]
    new_expert = jnp.logical_or(j == 0, be_ref[j] != be_ref[jnp.maximum(j - 1, 0)])

    @pl.when(jnp.logical_and(used, new_expert))
    def _():
        w1b[...] = w1_ref[...].astype(BF16)
        w3b[...] = w3_ref[...].astype(BF16)
        w2b[...] = w2_ref[...].astype(BF16)

    @pl.when(used)
    def _():
        xb = xs_ref[...].astype(BF16)
        h1 = jnp.dot(xb, w1b[...], preferred_element_type=F32)
        h3 = jnp.dot(xb, w3b[...], preferred_element_type=F32)
        hid = (_silu(h1) * h3).astype(BF16)
        yb_ref[...] = jnp.dot(hid, w2b[...], preferred_element_type=F32)

    @pl.when(jnp.logical_not(used))
    def _():
        yb_ref[...] = jnp.zeros_like(yb_ref)


def _experts(block_expert, n_used, xs, w1, w3, w2):
    p = xs.shape[0]
    nb = p // MOE_BLOCK
    row_map = lambda j, be, nu: (jnp.minimum(j, nu[0] - 1), 0)
    w_map = lambda j, be, nu: (be[j], 0, 0)
    grid_spec = pltpu.PrefetchScalarGridSpec(
        num_scalar_prefetch=2, grid=(nb,),
        in_specs=[pl.BlockSpec((MOE_BLOCK, D_MODEL), row_map),
                  pl.BlockSpec((None, D_MODEL, D_EXPERT), w_map),
                  pl.BlockSpec((None, D_MODEL, D_EXPERT), w_map),
                  pl.BlockSpec((None, D_EXPERT, D_MODEL), w_map)],
        out_specs=pl.BlockSpec((MOE_BLOCK, D_MODEL), lambda j, be, nu: (j, 0)),
        scratch_shapes=[pltpu.VMEM((D_MODEL, D_EXPERT), BF16), pltpu.VMEM((D_MODEL, D_EXPERT), BF16),
                        pltpu.VMEM((D_EXPERT, D_MODEL), BF16)],
    )
    return pl.pallas_call(
        _experts_body, grid_spec=grid_spec,
        out_shape=jax.ShapeDtypeStruct((p, D_MODEL), F32),
        compiler_params=pltpu.CompilerParams(dimension_semantics=("arbitrary",), vmem_limit_bytes=VMEM_LIMIT),
        name="experts",
    )(block_expert, n_used, xs, w1, w3, w2)


def _combine_body(slot_ref, slot_next_ref, gate_ref, h2_ref, gf_ref, yb_ref, o_ref, ybuf, sems):
    n_tok = h2_ref.shape[0]
    step = pl.program_id(0)
    cur = step % 2

    def gather(slots, buf_id):
        def issue(t, carry):
            for k in range(2):
                pltpu.make_async_copy(yb_ref.at[pl.ds(slots[k, t], 1)], ybuf.at[buf_id, k, pl.ds(t, 1)],
                                      sems.at[buf_id]).start()
            return carry

        lax.fori_loop(0, n_tok, issue, 0, unroll=ROW_UNROLL)

    @pl.when(step == 0)
    def _():
        gather(slot_ref, 0)

    @pl.when(step + 1 < pl.num_programs(0))
    def _():
        gather(slot_next_ref, 1 - cur)

    for k in range(2):
        pltpu.make_async_copy(yb_ref.at[pl.ds(0, n_tok)], ybuf.at[cur, k], sems.at[cur]).wait()
    y = h2_ref[...] + gate_ref[:, 0:1] * ybuf[cur, 0] + gate_ref[:, 1:2] * ybuf[cur, 1]
    o_ref[...] = _rms(y) * gf_ref[...]


def _combine(slots, gate_t, h2, gf, yb, tokens):
    t = h2.shape[0]
    last = t // tokens - 1
    return pl.pallas_call(
        _combine_body, grid=(t // tokens,),
        in_specs=[pl.BlockSpec((8, tokens), lambda i: (0, i), memory_space=pltpu.SMEM),
                  pl.BlockSpec((8, tokens), lambda i: (0, jnp.minimum(i + 1, last)), memory_space=pltpu.SMEM),
                  pl.BlockSpec((tokens, 8), lambda i: (i, 0)),
                  pl.BlockSpec((tokens, D_MODEL), lambda i: (i, 0)),
                  pl.BlockSpec((1, D_MODEL), lambda i: (0, 0)),
                  pl.BlockSpec(memory_space=pl.ANY)],
        out_specs=pl.BlockSpec((tokens, D_MODEL), lambda i: (i, 0)),
        scratch_shapes=[pltpu.VMEM((2, 2, tokens, D_MODEL), F32), pltpu.SemaphoreType.DMA((2,))],
        out_shape=jax.ShapeDtypeStruct((t, D_MODEL), F32),
        compiler_params=pltpu.CompilerParams(dimension_semantics=("arbitrary",)),
        name="combine",
    )(slots, slots, gate_t, h2, gf, yb)


def _pad_lanes(v, width=LANES):
    v = v.reshape(1, -1).astype(F32)
    return jnp.pad(v, ((0, 0), (0, width - v.shape[1])))


def _largest_tile(n, cap):
    t = cap
    while n % t:
        t //= 2
    return t


def kernel(x, meta_tokens, norm_mix, w_in, conv_dn, a_log, dt_bias, norm_head_dn, w_proj_dn, w_alpha, b_alpha,
           norm_head_gla, w_proj_gla, w_out, norm_ffn, w_group, b_group, w_router, b_router, w1, w3, w2, norm_final):
    assert norm_mix.shape[0] == 1, "single-layer block"
    bsz, seq, d = x.shape
    assert d == D_MODEL and seq % CHUNK == 0
    t = bsz * seq

    wi = w_in[0]
    offs = np.cumsum([0, DN_QK, DN_QK, DN_V, DN_V, DN_HEADS, DN_HEADS, GLA_QK, GLA_QK, GLA_V, GLA_V, GLA_RANK,
                      D_MODEL, D_MODEL]).tolist()
    seg = lambda i, j: wi[:, offs[i]:offs[j]]
    n_small = 2 * DN_HEADS + GLA_RANK
    w_all = jnp.concatenate([seg(0, 4), seg(6, 10), seg(11, 13), seg(4, 6), seg(10, 11),
                             jnp.zeros((D_MODEL, SMALL_W - n_small), F32)], axis=1).astype(BF16)
    g_mix = norm_mix[0].reshape(1, D_MODEL)

    conv_w = jnp.pad(conv_dn[0], ((0, 8 - CONV_WIDTH), (0, 0)))
    alog = _pad_lanes(a_log[0])
    dtb = _pad_lanes(dt_bias[0])
    nh_dn = norm_head_dn[0].reshape(1, DN_HEAD_DIM)
    lr_off = 2 * DN_HEADS
    wa = jnp.zeros((SMALL_W, GLA_QK), F32).at[lr_off:lr_off + GLA_RANK].set(w_alpha[0])
    ba = b_alpha[0].reshape(1, GLA_QK)
    nh_gla = norm_head_gla[0].reshape(1, GLA_VAL_DIM)

    def mixers(tokens2d, nb, hist, s0_dn, s0_gla, rows):
        qkv, z, qk_g, v_g, r_g, gates, small = _in_proj(tokens2d, g_mix, w_all, rows)
        r3 = lambda a: a.reshape(nb, -1, a.shape[-1])
        o_dn, s_dn = _dn_chunk(r3(qkv), r3(small), r3(z), conv_w, hist, s0_dn, alog, dtb, nh_dn)
        o_gla, s_gla = _gla_chunk(r3(qk_g), r3(v_g), r3(r_g), r3(small), wa, ba, nh_gla, s0_gla)
        return qkv, o_dn, o_gla, gates, s_dn, s_gla

    meta_rows = jnp.pad(meta_tokens.astype(F32), ((CHUNK - N_META, 0), (0, 0)))
    zero_hist = jnp.zeros((HIST_ROWS, 3 * DN_QK), F32)
    zero_dn = jnp.zeros((DN_HEADS, DN_HEAD_DIM, DN_HEAD_DIM), F32)
    zero_gla = jnp.zeros((GLA_HEADS, GLA_VAL_DIM, GLA_KEY_DIM), F32)
    qkv_m, _, _, _, s_dn_m, s_gla_m = mixers(meta_rows, 1, zero_hist, zero_dn, zero_gla, CHUNK)
    hist = qkv_m[CHUNK - HIST_ROWS:].astype(F32)

    x2d = x.reshape(t, d)
    rows = _largest_tile(t, 512)
    _, o_dn, o_gla, gates, _, _ = mixers(x2d, bsz, hist, s_dn_m[0], s_gla_m[0], rows)

    wr_t = jnp.concatenate([w_group[0], w_router[0]], axis=1).T
    wr_t = jnp.pad(wr_t, ((0, LANES - wr_t.shape[0]), (0, 0)))
    br = jnp.pad(jnp.concatenate([b_group[0], b_router[0]]), (0, LANES - N_GROUPS - N_EXPERTS)).reshape(LANES, 1)
    h2, hn, logits_t = _out_proj(
        o_dn.reshape(t, DN_V), o_gla.reshape(t, GLA_V), gates, x2d,
        w_proj_dn[0].astype(BF16), w_proj_gla[0].astype(BF16), w_out[0].astype(BF16),
        norm_ffn[0].reshape(1, D_MODEL), wr_t, br, rows)

    idx, gate, cnt = _route(logits_t, _largest_tile(t, 512))

    sizes = cnt[:, 0].astype(jnp.int32)
    padded = (sizes + MOE_BLOCK - 1) // MOE_BLOCK * MOE_BLOCK
    pends = jnp.cumsum(padded)
    pstart = (pends - padded).astype(jnp.int32)
    n_blocks = (2 * t) // MOE_BLOCK + N_EXPERTS
    n_used = (pends[-1:] // MOE_BLOCK).astype(jnp.int32)
    block_row0 = jnp.arange(n_blocks, dtype=jnp.int32) * MOE_BLOCK
    block_expert = jnp.minimum(jnp.sum(pends[None, :] <= block_row0[:, None], axis=1), N_EXPERTS - 1).astype(jnp.int32)

    tok_tile = _largest_tile(t, 256)
    slots = _slots(pstart, idx, _largest_tile(t, 2048))
    xs = _dispatch(pstart, sizes, slots, hn, n_blocks * MOE_BLOCK, tok_tile)
    yb = _experts(block_expert, n_used, xs, w1[0], w3[0], w2[0])
    out = _combine(slots, gate.T, h2, norm_final.reshape(1, D_MODEL), yb, tok_tile)
    return out.reshape(bsz, seq, d)
```

```python
import functools

import jax
import jax.numpy as jnp
import numpy as np
from jax import lax
from jax.experimental import pallas as pl
from jax.experimental.pallas import tpu as pltpu

F32 = jnp.float32
BF16 = jnp.bfloat16

D_MODEL = 1024
CHUNK = 64
N_META = 16
EPS = 1e-6
DN_HEADS = 8
DN_HEAD_DIM = 128
DN_QK = DN_HEADS * DN_HEAD_DIM
DN_V = DN_HEADS * DN_HEAD_DIM
CONV_WIDTH = 4
GLA_HEADS = 4
GLA_KEY_DIM = 128
GLA_VAL_DIM = 256
GLA_QK = GLA_HEADS * GLA_KEY_DIM
GLA_V = GLA_HEADS * GLA_VAL_DIM
GLA_RANK = 16
GLA_TAU = 16.0
N_GROUPS = 8
EXPERTS_PER_GROUP = 8
N_EXPERTS = N_GROUPS * EXPERTS_PER_GROUP
D_EXPERT = 512
MOE_BLOCK = 512
ZERO_ROWS = 256

LANES = 128
SUBLANES = 8
SMALL_W = LANES
HIST_ROWS = 8
VMEM_LIMIT = 56 * 1024 * 1024

_NT = (((1,), (1,)), ((), ()))
_TN = (((0,), (0,)), ((), ()))


def _bdot(a, b):
    return jnp.dot(a.astype(BF16), b.astype(BF16), preferred_element_type=F32)


def _bdot_nt(a, b):
    return lax.dot_general(a.astype(BF16), b.astype(BF16), _NT, preferred_element_type=F32)


def _bdot_tn(a, b):
    return lax.dot_general(a.astype(BF16), b.astype(BF16), _TN, preferred_element_type=F32)


def _sigmoid(x):
    return 1.0 / (1.0 + jnp.exp(-x))


def _silu(x):
    half = 0.5 * x
    return half + half * jnp.tanh(half)


def _softplus(x):
    return jnp.maximum(x, 0.0) + jnp.log(1.0 + jnp.exp(-jnp.abs(x)))


def _rms(x, eps=EPS):
    return x * lax.rsqrt(jnp.mean(x * x, axis=-1, keepdims=True) + eps)


def _tri_incl(n):
    r = lax.broadcasted_iota(jnp.int32, (n, n), 0)
    c = lax.broadcasted_iota(jnp.int32, (n, n), 1)
    return r, c


def _bf16_part(x):
    bits = pltpu.bitcast(x, jnp.uint32) & jnp.uint32(0xFFFF0000)
    return pltpu.bitcast(bits, F32)


def _cumsum_rows(x):
    r, c = _tri_incl(x.shape[0])
    tri = (r >= c).astype(F32)
    hi = _bf16_part(x)
    r1 = x - hi
    mid = _bf16_part(r1)
    lo = r1 - mid
    return (jnp.dot(tri, hi, preferred_element_type=F32) + jnp.dot(tri, mid, preferred_element_type=F32)
            + jnp.dot(tri, lo, preferred_element_type=F32))


IN_GROUPS = ((3 * DN_QK, BF16), (DN_V, BF16), (2 * GLA_QK, BF16), (GLA_V, BF16), (GLA_V, BF16),
             (2 * D_MODEL, BF16), (SMALL_W, F32))


def _inproj_body(x_ref, g_ref, w_ref, *o_refs):
    x = x_ref[...]
    xb = (_rms(x) * g_ref[...]).astype(BF16)
    col = 0
    for o_ref in o_refs:
        width = o_ref.shape[1]
        o_ref[...] = jnp.dot(xb, w_ref[:, col:col + width], preferred_element_type=F32).astype(o_ref.dtype)
        col += width


def _in_proj(x2d, gain, w_all, rows):
    t = x2d.shape[0]
    in_specs = [pl.BlockSpec((rows, D_MODEL), lambda i: (i, 0)), pl.BlockSpec((1, D_MODEL), lambda i: (0, 0)),
                pl.BlockSpec(w_all.shape, lambda i: (0, 0), pipeline_mode=pl.Buffered(1))]
    out_specs = [pl.BlockSpec((rows, w), lambda i: (i, 0)) for w, _ in IN_GROUPS]
    out_shape = [jax.ShapeDtypeStruct((t, w), dt) for w, dt in IN_GROUPS]
    return pl.pallas_call(
        _inproj_body, grid=(t // rows,), in_specs=in_specs, out_specs=out_specs, out_shape=out_shape,
        compiler_params=pltpu.CompilerParams(dimension_semantics=("parallel",), vmem_limit_bytes=VMEM_LIMIT),
        name="in_proj",
    )(x2d, gain, w_all)


def _unit_lower_inverse(mats):
    n = mats[0].shape[0]
    r, c = _tri_incl(n)
    eye = (r == c).astype(F32)
    pair = (r >> 1) == (c >> 1)
    ts = [eye - jnp.where(pair, a, 0.0) for a in mats]
    m = 2
    while m < n:
        sh = m.bit_length() - 1
        keep = ((r >> (sh + 1)) == (c >> (sh + 1))) & ((r >> sh) != (c >> sh))
        ams = [jnp.where(keep, a, 0.0) for a in mats]
        xs = [jnp.dot(t, am, preferred_element_type=F32) for t, am in zip(ts, ams)]
        ys = [jnp.dot(x, t, preferred_element_type=F32) for x, t in zip(xs, ts)]
        ts = [t - y for t, y in zip(ts, ys)]
        m *= 2
    return ts


def _l2_normalise(xs, ones_bd, scale):
    out = []
    for p in range(0, len(xs), 2):
        sq = jnp.concatenate([xs[p] * xs[p], xs[p + 1] * xs[p + 1]], axis=1)
        ss = jnp.dot(sq, ones_bd, preferred_element_type=F32)
        inv = lax.rsqrt(ss + EPS) * scale
        out.append(xs[p] * inv[:, :DN_HEAD_DIM])
        out.append(xs[p + 1] * inv[:, DN_HEAD_DIM:])
    return out


def _dn_body(qkv_ref, sm_ref, z_ref, cw_ref, hist_ref, s0_ref, alog_ref, dtb_ref, nh_ref, ones_ref,
             o_ref, sfin_ref, buf, s_scr):
    c_id = pl.program_id(1)
    n_rows = qkv_ref.shape[0]
    items = [(b, h) for b in range(n_rows) for h in range(DN_HEADS)]
    n_items = range(len(items))

    @pl.when(c_id == 0)
    def _():
        for b in range(n_rows):
            buf[b, 0:HIST_ROWS, :] = hist_ref[...]
            s_scr[b] = s0_ref[...]

    for b in range(n_rows):
        buf[b, HIST_ROWS:HIST_ROWS + CHUNK, :] = qkv_ref[b].astype(F32)

    def conv_silu(b, col):
        acc = None
        for j in range(CONV_WIDTH):
            lo = HIST_ROWS - (CONV_WIDTH - 1) + j
            term = buf[b, lo:lo + CHUNK, col:col + DN_HEAD_DIM] * cw_ref[j:j + 1, col:col + DN_HEAD_DIM]
            acc = term if acc is None else acc + term
        return _silu(acc)

    gc_all, gc_t, beta_all = [], [], []
    for b in range(n_rows):
        sm = sm_ref[b]
        g_b = -jnp.exp(alog_ref[...]) * _softplus(sm + dtb_ref[...])
        beta_all.append(_sigmoid(sm))
        gc_b = _cumsum_rows(g_b)
        gc_all.append(gc_b)
        gc_t.append(jnp.concatenate([gc_b, jnp.zeros_like(gc_b)], axis=0).T)

    r, c = _tri_incl(CHUNK)
    causal = r >= c
    strict = r > c
    scale = DN_HEAD_DIM ** -0.5

    q = [conv_silu(b, h * DN_HEAD_DIM) for b, h in items]
    k = [conv_silu(b, DN_QK + h * DN_HEAD_DIM) for b, h in items]
    v = [conv_silu(b, 2 * DN_QK + h * DN_HEAD_DIM) for b, h in items]
    ones_bd = ones_ref[...]
    q = _l2_normalise(q, ones_bd, scale)
    k = _l2_normalise(k, ones_bd, 1.0)
    gcol = [gc_all[b][:, h:h + 1] for b, h in items]
    grow = [gc_t[b][h:h + 1, 0:CHUNK] for b, h in items]
    bcol = [beta_all[b][:, DN_HEADS + h:DN_HEADS + h + 1] for b, h in items]
    glast = [gc_all[b][CHUNK - 1:CHUNK, h:h + 1] for b, h in items]
    decay = [jnp.where(causal, jnp.exp(jnp.where(causal, gcol[i] - grow[i], 0.0)), 0.0) for i in n_items]
    eg = [jnp.exp(gcol[i]) for i in n_items]
    kk = [lax.dot_general(k[i], k[i], _NT, preferred_element_type=F32) for i in n_items]
    qk = [lax.dot_general(q[i], k[i], _NT, preferred_element_type=F32) for i in n_items]
    a_mat = [jnp.where(strict, bcol[i] * kk[i] * decay[i], 0.0) for i in n_items]
    qk = [jnp.where(causal, qk[i] * decay[i], 0.0) for i in n_items]
    t_inv = _unit_lower_inverse(a_mat)
    rhs = [jnp.concatenate([v[i] * bcol[i], k[i] * (bcol[i] * eg[i])], axis=1) for i in n_items]
    sol = [jnp.dot(t_inv[i], rhs[i], preferred_element_type=F32) for i in n_items]
    s_old = [s_scr[b, h] for b, h in items]
    qe = [q[i] * eg[i] for i in n_items]
    k_dec = [k[i] * jnp.exp(glast[i] - gcol[i]) for i in n_items]
    ws = [jnp.dot(sol[i][:, DN_HEAD_DIM:], s_old[i], preferred_element_type=F32) for i in n_items]
    o_inter = [jnp.dot(qe[i], s_old[i], preferred_element_type=F32) for i in n_items]
    v_new = [sol[i][:, :DN_HEAD_DIM] - ws[i] for i in n_items]
    o_intra = [jnp.dot(qk[i], v_new[i], preferred_element_type=F32) for i in n_items]
    s_add = [lax.dot_general(k_dec[i], v_new[i], _TN, preferred_element_type=F32) for i in n_items]
    for i, (b, h) in enumerate(items):
        col = h * DN_HEAD_DIM
        s_scr[b, h] = s_old[i] * jnp.exp(glast[i]) + s_add[i]
        zed = z_ref[b, :, col:col + DN_HEAD_DIM].astype(F32)
        o = o_inter[i] + o_intra[i]
        o_ref[b, :, col:col + DN_HEAD_DIM] = (_rms(o) * nh_ref[...] * _silu(zed)).astype(o_ref.dtype)

    for b in range(n_rows):
        buf[b, 0:HIST_ROWS, :] = buf[b, CHUNK:CHUNK + HIST_ROWS, :]

    @pl.when(c_id == pl.num_programs(1) - 1)
    def _():
        sfin_ref[...] = s_scr[...]


def _rows_per_step(b, want):
    while b % want:
        want //= 2
    return want


def _dn_chunk(qkv, small, z, conv_w, hist, s0, alog, dtb, nh):
    b, l, _ = qkv.shape
    lane_head = np.arange(2 * DN_HEAD_DIM) // DN_HEAD_DIM
    ones_bd = jnp.asarray(lane_head[:, None] == lane_head[None, :], F32)
    nc = l // CHUNK
    nr = _rows_per_step(b, 2)
    const2 = lambda bi, ci: (0, 0)
    blk = lambda w: pl.BlockSpec((nr, CHUNK, w), lambda bi, ci: (bi, ci, 0))
    return pl.pallas_call(
        _dn_body, grid=(b // nr, nc),
        in_specs=[
            blk(3 * DN_QK), blk(SMALL_W), blk(DN_V),
            pl.BlockSpec(conv_w.shape, const2),
            pl.BlockSpec(hist.shape, const2),
            pl.BlockSpec(s0.shape, lambda bi, ci: (0, 0, 0)),
            pl.BlockSpec(alog.shape, const2),
            pl.BlockSpec(dtb.shape, const2),
            pl.BlockSpec(nh.shape, const2),
            pl.BlockSpec(ones_bd.shape, const2),
        ],
        out_specs=[
            blk(DN_V),
            pl.BlockSpec((nr, DN_HEADS, DN_HEAD_DIM, DN_HEAD_DIM), lambda bi, ci: (bi, 0, 0, 0)),
        ],
        out_shape=[
            jax.ShapeDtypeStruct((b, l, DN_V), BF16),
            jax.ShapeDtypeStruct((b, DN_HEADS, DN_HEAD_DIM, DN_HEAD_DIM), F32),
        ],
        scratch_shapes=[
            pltpu.VMEM((nr, HIST_ROWS + CHUNK, 3 * DN_QK), F32),
            pltpu.VMEM((nr, DN_HEADS, DN_HEAD_DIM, DN_HEAD_DIM), F32),
        ],
        compiler_params=pltpu.CompilerParams(dimension_semantics=("parallel", "arbitrary"),
                                             vmem_limit_bytes=VMEM_LIMIT),
        name="dn_chunk",
    )(qkv, small, z, conv_w, hist, s0, alog, dtb, nh, ones_bd)


def _gla_body(qk_ref, v_ref, r_ref, sm_ref, wa_ref, ba_ref, nh_ref, s0_ref, o_ref, sfin_ref, s_scr):
    c_id = pl.program_id(1)
    n_rows = qk_ref.shape[0]
    items = [(b, h) for b in range(n_rows) for h in range(GLA_HEADS)]
    n_items = range(len(items))

    @pl.when(c_id == 0)
    def _():
        for b in range(n_rows):
            s_scr[b] = s0_ref[...]

    b_all = []
    for b in range(n_rows):
        la = jnp.dot(sm_ref[b], wa_ref[...], preferred_element_type=F32) + ba_ref[...]
        log_alpha = (jnp.minimum(la, 0.0) - jnp.log(1.0 + jnp.exp(-jnp.abs(la)))) * (1.0 / GLA_TAU)
        b_all.append(_cumsum_rows(log_alpha))

    r, c = _tri_incl(CHUNK)
    causal = r >= c
    scale = GLA_KEY_DIM ** -0.5
    mid = CHUNK // 2 - 1

    q = [qk_ref[b, :, h * GLA_KEY_DIM:(h + 1) * GLA_KEY_DIM].astype(F32) * scale for b, h in items]
    k = [qk_ref[b, :, GLA_QK + h * GLA_KEY_DIM:GLA_QK + (h + 1) * GLA_KEY_DIM].astype(F32) for b, h in items]
    v = [v_ref[b, :, h * GLA_VAL_DIM:(h + 1) * GLA_VAL_DIM].astype(F32) for b, h in items]
    bh = [b_all[b][:, h * GLA_KEY_DIM:(h + 1) * GLA_KEY_DIM] for b, h in items]
    bmid = [x[mid:mid + 1, :] for x in bh]
    blast = [x[CHUNK - 1:CHUNK, :] for x in bh]
    qs = [q[i] * jnp.exp(bh[i] - bmid[i]) for i in n_items]
    ks = [k[i] * jnp.exp(bmid[i] - bh[i]) for i in n_items]
    qd = [q[i] * jnp.exp(bh[i]) for i in n_items]
    kd = [k[i] * jnp.exp(blast[i] - bh[i]) for i in n_items]
    st = [s_scr[b, h] for b, h in items]
    att = [lax.dot_general(qs[i], ks[i], _NT, preferred_element_type=F32) for i in n_items]
    o_inter = [lax.dot_general(qd[i], st[i], _NT, preferred_element_type=F32) for i in n_items]
    s_add = [lax.dot_general(v[i], kd[i], _TN, preferred_element_type=F32) for i in n_items]
    att = [jnp.where(causal, a, 0.0) for a in att]
    o_intra = [jnp.dot(att[i], v[i], preferred_element_type=F32) for i in n_items]
    for i, (b, h) in enumerate(items):
        vc = h * GLA_VAL_DIM
        s_scr[b, h] = st[i] * jnp.exp(blast[i]) + s_add[i]
        gate = _silu(r_ref[b, :, vc:vc + GLA_VAL_DIM].astype(F32))
        o = o_inter[i] + o_intra[i]
        o_ref[b, :, vc:vc + GLA_VAL_DIM] = (_rms(o) * nh_ref[...] * gate).astype(o_ref.dtype)

    @pl.when(c_id == pl.num_programs(1) - 1)
    def _():
        sfin_ref[...] = s_scr[...]


def _gla_chunk(qk, v, rr, small, wa, ba, nh, s0):
    b, l, _ = qk.shape
    nc = l // CHUNK
    nr = _rows_per_step(b, 4)
    const2 = lambda bi, ci: (0, 0)
    blk = lambda w: pl.BlockSpec((nr, CHUNK, w), lambda bi, ci: (bi, ci, 0))
    return pl.pallas_call(
        _gla_body, grid=(b // nr, nc),
        in_specs=[
            blk(2 * GLA_QK), blk(GLA_V), blk(GLA_V), blk(SMALL_W),
            pl.BlockSpec(wa.shape, const2), pl.BlockSpec(ba.shape, const2), pl.BlockSpec(nh.shape, const2),
            pl.BlockSpec(s0.shape, lambda bi, ci: (0, 0, 0)),
        ],
        out_specs=[
            blk(GLA_V),
            pl.BlockSpec((nr, GLA_HEADS, GLA_VAL_DIM, GLA_KEY_DIM), lambda bi, ci: (bi, 0, 0, 0)),
        ],
        out_shape=[
            jax.ShapeDtypeStruct((b, l, GLA_V), BF16),
            jax.ShapeDtypeStruct((b, GLA_HEADS, GLA_VAL_DIM, GLA_KEY_DIM), F32),
        ],
        scratch_shapes=[pltpu.VMEM((nr, GLA_HEADS, GLA_VAL_DIM, GLA_KEY_DIM), F32)],
        compiler_params=pltpu.CompilerParams(dimension_semantics=("parallel", "arbitrary"),
                                             vmem_limit_bytes=VMEM_LIMIT),
        name="gla_chunk",
    )(qk, v, rr, small, wa, ba, nh, s0)


TOK_SUB = D_MODEL // LANES
assert TOK_SUB == SUBLANES


def _tiles_to_rows(ref, n):
    return jnp.concatenate([ref[pl.ds(s, n, stride=TOK_SUB), :] for s in range(TOK_SUB)], axis=1)


def _rows_to_tiles(ref, value):
    n = value.shape[0]
    for s in range(TOK_SUB):
        ref[pl.ds(s, n, stride=TOK_SUB), :] = value[:, s * LANES:(s + 1) * LANES]


def _tile_rows(tok, count=1):
    return pl.ds(pl.multiple_of(tok * TOK_SUB, TOK_SUB), count * TOK_SUB)


def _outproj_body(odn_ref, ogla_ref, gates_ref, x_ref, wd_ref, wg_ref, wo_ref, gn_ref, wr_ref, br_ref,
                  h2_ref, hn_ref, lg_ref):
    y_dn = jnp.dot(odn_ref[...], wd_ref[...], preferred_element_type=F32)
    y_gla = jnp.dot(ogla_ref[...], wg_ref[...], preferred_element_type=F32)
    gd = _sigmoid(gates_ref[:, 0:D_MODEL].astype(F32))
    gg = _sigmoid(gates_ref[:, D_MODEL:2 * D_MODEL].astype(F32))
    merged = gd * y_dn + gg * y_gla
    h2 = x_ref[...] + _bdot(merged, wo_ref[...])
    h2_ref[...] = h2
    hn = _rms(h2) * gn_ref[...]
    _rows_to_tiles(hn_ref, hn)
    lg_ref[...] = lax.dot_general(wr_ref[...], hn, _NT, preferred_element_type=F32,
                                  precision=lax.Precision.HIGHEST) + br_ref[...]


def _out_proj(o_dn, o_gla, gates, x2d, wd, wg, wo, gn, wr_t, br, rows):
    t = x2d.shape[0]
    row_blk = lambda w: pl.BlockSpec((rows, w), lambda i: (i, 0))
    const = lambda a: pl.BlockSpec(a.shape, lambda i: (0, 0))
    return pl.pallas_call(
        _outproj_body, grid=(t // rows,),
        in_specs=[row_blk(DN_V), row_blk(GLA_V), row_blk(2 * D_MODEL), row_blk(D_MODEL),
                  const(wd), const(wg), const(wo), const(gn), const(wr_t), const(br)],
        out_specs=[row_blk(D_MODEL), pl.BlockSpec((rows * TOK_SUB, LANES), lambda i: (i, 0)),
                   pl.BlockSpec((LANES, rows), lambda i: (0, i))],
        out_shape=[jax.ShapeDtypeStruct((t, D_MODEL), F32), jax.ShapeDtypeStruct((t * TOK_SUB, LANES), F32),
                   jax.ShapeDtypeStruct((LANES, t), F32)],
        compiler_params=pltpu.CompilerParams(dimension_semantics=("parallel",), vmem_limit_bytes=VMEM_LIMIT),
        name="out_proj",
    )(o_dn, o_gla, gates, x2d, wd, wg, wo, gn, wr_t, br)


ROUTE_SUB = 256


def _route_body(lg_ref, idx_ref, gate_ref, cnt_ref, carry):
    step = pl.program_id(0)

    @pl.when(step == 0)
    def _():
        carry[...] = jnp.zeros_like(carry)

    tt = lg_ref.shape[1]
    gl = lg_ref[0:N_GROUPS, :]
    gmax = jnp.max(gl, axis=0, keepdims=True)
    rid8 = lax.broadcasted_iota(jnp.int32, (N_GROUPS, tt), 0)
    gsel = jnp.min(jnp.where(gl == gmax, rid8, N_GROUPS), axis=0, keepdims=True)
    gw = 1.0 / jnp.sum(jnp.exp(gl - gmax), axis=0, keepdims=True)
    el = lg_ref[N_GROUPS:N_GROUPS + N_EXPERTS, :]
    rid = lax.broadcasted_iota(jnp.int32, (N_EXPERTS, tt), 0)
    neg = jnp.float32(-jnp.inf)
    ein = jnp.where((rid >> 3) == gsel, el, neg)
    t1 = jnp.max(ein, axis=0, keepdims=True)
    i1 = jnp.min(jnp.where(ein == t1, rid, N_EXPERTS), axis=0, keepdims=True)
    ein2 = jnp.where(rid == i1, neg, ein)
    t2 = jnp.max(ein2, axis=0, keepdims=True)
    i2 = jnp.min(jnp.where(ein2 == t2, rid, N_EXPERTS), axis=0, keepdims=True)
    e21 = jnp.exp(t2 - t1)
    den = 1.0 / (1.0 + e21)
    sel1 = rid == i1
    sel2 = rid == i2
    onehot = jnp.where(sel1 | sel2, 1.0, 0.0)

    ur, uc = _tri_incl(ROUTE_SUB)
    upper = (ur <= uc).astype(BF16)
    run = carry[...]
    r1_parts, r2_parts = [], []
    for s in range(tt // ROUTE_SUB):
        sl = slice(s * ROUTE_SUB, (s + 1) * ROUTE_SUB)
        oh = onehot[:, sl]
        incl = jnp.dot(oh.astype(BF16), upper, preferred_element_type=F32) + run
        excl = incl - oh
        r1_parts.append(jnp.sum(jnp.where(sel1[:, sl], excl, 0.0), axis=0, keepdims=True))
        r2_parts.append(jnp.sum(jnp.where(sel2[:, sl], excl, 0.0), axis=0, keepdims=True))
        run = jnp.broadcast_to(incl[:, ROUTE_SUB - 1:ROUTE_SUB], run.shape)
    carry[...] = run
    r1 = jnp.concatenate(r1_parts, axis=1) if len(r1_parts) > 1 else r1_parts[0]
    r2 = jnp.concatenate(r2_parts, axis=1) if len(r2_parts) > 1 else r2_parts[0]

    idx_ref[...] = jnp.zeros_like(idx_ref)
    idx_ref[0:1, :] = i1
    idx_ref[1:2, :] = i2
    idx_ref[2:3, :] = r1.astype(jnp.int32)
    idx_ref[3:4, :] = r2.astype(jnp.int32)
    gate_ref[...] = jnp.zeros_like(gate_ref)
    gate_ref[0:1, :] = den * gw
    gate_ref[1:2, :] = e21 * den * gw
    cnt_ref[...] = run[:, 0:LANES]


def _route(logits_t, lanes):
    t = logits_t.shape[1]
    return pl.pallas_call(
        _route_body, grid=(t // lanes,),
        in_specs=[pl.BlockSpec((LANES, lanes), lambda i: (0, i))],
        out_specs=[pl.BlockSpec((8, lanes), lambda i: (0, i)), pl.BlockSpec((8, lanes), lambda i: (0, i)),
                   pl.BlockSpec((N_EXPERTS, LANES), lambda i: (0, 0))],
        out_shape=[jax.ShapeDtypeStruct((8, t), jnp.int32), jax.ShapeDtypeStruct((8, t), F32),
                   jax.ShapeDtypeStruct((N_EXPERTS, LANES), F32)],
        scratch_shapes=[pltpu.VMEM((N_EXPERTS, ROUTE_SUB), F32)],
        compiler_params=pltpu.CompilerParams(dimension_semantics=("arbitrary",)),
        name="route",
    )(logits_t)


def _row_copy(src_ref, src_tok, dst_ref, dst_tok, sem):
    return pltpu.make_async_copy(src_ref.at[_tile_rows(src_tok)], dst_ref.at[_tile_rows(dst_tok)], sem)


ROW_UNROLL = 8


def _slots_body(pstart_ref, idx_ref, slot_ref):
    idx = idx_ref[...]
    base = jnp.zeros_like(idx)
    for e in range(N_EXPERTS):
        base = jnp.where(idx == e, pstart_ref[e], base)
    slot_ref[...] = jnp.zeros_like(idx)
    slot_ref[0:2, :] = base[0:2, :] + idx[2:4, :]


def _slots(pstart, idx, lanes):
    t = idx.shape[1]
    grid_spec = pltpu.PrefetchScalarGridSpec(
        num_scalar_prefetch=1, grid=(t // lanes,),
        in_specs=[pl.BlockSpec((8, lanes), lambda i, ps: (0, i))],
        out_specs=pl.BlockSpec((8, lanes), lambda i, ps: (0, i)),
    )
    return pl.pallas_call(
        _slots_body, grid_spec=grid_spec, out_shape=jax.ShapeDtypeStruct(idx.shape, jnp.int32),
        compiler_params=pltpu.CompilerParams(dimension_semantics=("parallel",)),
        name="slots",
    )(pstart, idx)


def _dispatch_body(pstart_ref, size_ref, slot_ref, hn_ref, xs_ref, zblk, sem):
    n_tok = hn_ref.shape[0] // TOK_SUB

    @pl.when(pl.program_id(0) == 0)
    def _():
        zblk[...] = jnp.zeros_like(zblk)
        parts = MOE_BLOCK // ZERO_ROWS

        def clear_copy(row0, part):
            return pltpu.make_async_copy(zblk, xs_ref.at[_tile_rows(row0 + part * ZERO_ROWS, ZERO_ROWS)], sem)

        def per_expert(e, n_rows):
            size = size_ref[e]
            full = size // MOE_BLOCK * MOE_BLOCK

            @pl.when(size != full)
            def _():
                for part in range(parts):
                    clear_copy(pstart_ref[e] + full, part).start()
                for part in range(parts):
                    clear_copy(pstart_ref[e] + full, part).wait()

            return n_rows + (size + MOE_BLOCK - 1) // MOE_BLOCK * MOE_BLOCK

        n_rows = lax.fori_loop(0, N_EXPERTS, per_expert, 0)

        def clear_unused(j, c):
            for part in range(parts):
                clear_copy(j * MOE_BLOCK, part).start()
            for part in range(parts):
                clear_copy(j * MOE_BLOCK, part).wait()
            return c

        lax.fori_loop(n_rows // MOE_BLOCK, xs_ref.shape[0] // (MOE_BLOCK * TOK_SUB), clear_unused, 0)

    def issue(t, carry):
        for k in range(2):
            _row_copy(hn_ref, t, xs_ref, slot_ref[k, t], sem).start()
        return carry

    lax.fori_loop(0, n_tok, issue, 0, unroll=ROW_UNROLL)
    for k in range(2):
        pltpu.make_async_copy(hn_ref, xs_ref.at[_tile_rows(0, n_tok)], sem).wait()


def _dispatch(pstart, sizes, slots, hn, n_rows_out, tokens):
    t = hn.shape[0] // TOK_SUB
    grid_spec = pltpu.PrefetchScalarGridSpec(
        num_scalar_prefetch=2, grid=(t // tokens,),
        in_specs=[pl.BlockSpec((8, tokens), lambda i, ps, sz: (0, i), memory_space=pltpu.SMEM),
                  pl.BlockSpec((tokens * TOK_SUB, LANES), lambda i, ps, sz: (i, 0))],
        out_specs=pl.BlockSpec(memory_space=pl.ANY),
        scratch_shapes=[pltpu.VMEM((ZERO_ROWS * TOK_SUB, LANES), F32), pltpu.SemaphoreType.DMA(())],
    )
    return pl.pallas_call(
        _dispatch_body, grid_spec=grid_spec,
        out_shape=jax.ShapeDtypeStruct((n_rows_out * TOK_SUB, LANES), F32),
        compiler_params=pltpu.CompilerParams(dimension_semantics=("arbitrary",)),
        name="dispatch",
    )(pstart, sizes, slots, hn)


def _experts_body(be_ref, nused_ref, xs_ref, w1_ref, w3_ref, w2_ref, yb_ref, w1b, w3b, w2b):
    j = pl.program_id(0)
    used = j < nused_ref[0]
    new_expert = jnp.logical_or(j == 0, be_ref[j] != be_ref[jnp.maximum(j - 1, 0)])

    @pl.when(jnp.logical_and(used, new_expert))
    def _():
        w1b[...] = w1_ref[...].astype(BF16)
        w3b[...] = w3_ref[...].astype(BF16)
        w2b[...] = w2_ref[...].astype(BF16)

    @pl.when(used)
    def _():
        xb = _tiles_to_rows(xs_ref, MOE_BLOCK).astype(BF16)
        h1 = jnp.dot(xb, w1b[...], preferred_element_type=F32)
        h3 = jnp.dot(xb, w3b[...], preferred_element_type=F32)
        hid = (_silu(h1) * h3).astype(BF16)
        _rows_to_tiles(yb_ref, jnp.dot(hid, w2b[...], preferred_element_type=F32))

    @pl.when(jnp.logical_not(used))
    def _():
        yb_ref[...] = jnp.zeros_like(yb_ref)


def _experts(block_expert, n_used, xs, w1, w3, w2):
    p = xs.shape[0] // TOK_SUB
    nb = p // MOE_BLOCK
    row_map = lambda j, be, nu: (jnp.minimum(j, nu[0] - 1), 0)
    w_map = lambda j, be, nu: (be[j], 0, 0)
    grid_spec = pltpu.PrefetchScalarGridSpec(
        num_scalar_prefetch=2, grid=(nb,),
        in_specs=[pl.BlockSpec((MOE_BLOCK * TOK_SUB, LANES), row_map),
                  pl.BlockSpec((None, D_MODEL, D_EXPERT), w_map),
                  pl.BlockSpec((None, D_MODEL, D_EXPERT), w_map),
                  pl.BlockSpec((None, D_EXPERT, D_MODEL), w_map)],
        out_specs=pl.BlockSpec((MOE_BLOCK * TOK_SUB, LANES), lambda j, be, nu: (j, 0)),
        scratch_shapes=[pltpu.VMEM((D_MODEL, D_EXPERT), BF16), pltpu.VMEM((D_MODEL, D_EXPERT), BF16),
                        pltpu.VMEM((D_EXPERT, D_MODEL), BF16)],
    )
    return pl.pallas_call(
        _experts_body, grid_spec=grid_spec,
        out_shape=jax.ShapeDtypeStruct((p * TOK_SUB, LANES), F32),
        compiler_params=pltpu.CompilerParams(dimension_semantics=("arbitrary",), vmem_limit_bytes=VMEM_LIMIT),
        name="experts",
    )(block_expert, n_used, xs, w1, w3, w2)


def _combine_body(slot_ref, slot_next_ref, gate_ref, h2_ref, gf_ref, yb_ref, o_ref, ybuf, sems):
    n_tok = h2_ref.shape[0]
    step = pl.program_id(0)
    cur = step % 2

    def gather(slots, buf_id):
        def issue(t, carry):
            for k in range(2):
                pltpu.make_async_copy(yb_ref.at[_tile_rows(slots[k, t])], ybuf.at[buf_id, k, _tile_rows(t)],
                                      sems.at[buf_id]).start()
            return carry

        lax.fori_loop(0, n_tok, issue, 0, unroll=ROW_UNROLL)

    @pl.when(step == 0)
    def _():
        gather(slot_ref, 0)

    @pl.when(step + 1 < pl.num_programs(0))
    def _():
        gather(slot_next_ref, 1 - cur)

    for k in range(2):
        pltpu.make_async_copy(yb_ref.at[_tile_rows(0, n_tok)], ybuf.at[cur, k], sems.at[cur]).wait()
    y0 = _tiles_to_rows(ybuf.at[cur, 0], n_tok)
    y1 = _tiles_to_rows(ybuf.at[cur, 1], n_tok)
    y = h2_ref[...] + gate_ref[:, 0:1] * y0 + gate_ref[:, 1:2] * y1
    o_ref[...] = _rms(y) * gf_ref[...]


def _combine(slots, gate_t, h2, gf, yb, tokens):
    t = h2.shape[0]
    last = t // tokens - 1
    return pl.pallas_call(
        _combine_body, grid=(t // tokens,),
        in_specs=[pl.BlockSpec((8, tokens), lambda i: (0, i), memory_space=pltpu.SMEM),
                  pl.BlockSpec((8, tokens), lambda i: (0, jnp.minimum(i + 1, last)), memory_space=pltpu.SMEM),
                  pl.BlockSpec((tokens, 8), lambda i: (i, 0)),
                  pl.BlockSpec((tokens, D_MODEL), lambda i: (i, 0)),
                  pl.BlockSpec((1, D_MODEL), lambda i: (0, 0)),
                  pl.BlockSpec(memory_space=pl.ANY)],
        out_specs=pl.BlockSpec((tokens, D_MODEL), lambda i: (i, 0)),
        scratch_shapes=[pltpu.VMEM((2, 2, tokens * TOK_SUB, LANES), F32), pltpu.SemaphoreType.DMA((2,))],
        out_shape=jax.ShapeDtypeStruct((t, D_MODEL), F32),
        compiler_params=pltpu.CompilerParams(dimension_semantics=("arbitrary",)),
        name="combine",
    )(slots, slots, gate_t, h2, gf, yb)


def _pad_lanes(v, width=LANES):
    v = v.reshape(1, -1).astype(F32)
    return jnp.pad(v, ((0, 0), (0, width - v.shape[1])))


def _largest_tile(n, cap):
    t = cap
    while n % t:
        t //= 2
    return t


def kernel(x, meta_tokens, norm_mix, w_in, conv_dn, a_log, dt_bias, norm_head_dn, w_proj_dn, w_alpha, b_alpha,
           norm_head_gla, w_proj_gla, w_out, norm_ffn, w_group, b_group, w_router, b_router, w1, w3, w2, norm_final):
    assert norm_mix.shape[0] == 1, "single-layer block"
    bsz, seq, d = x.shape
    assert d == D_MODEL and seq % CHUNK == 0
    t = bsz * seq

    wi = w_in[0]
    offs = np.cumsum([0, DN_QK, DN_QK, DN_V, DN_V, DN_HEADS, DN_HEADS, GLA_QK, GLA_QK, GLA_V, GLA_V, GLA_RANK,
                      D_MODEL, D_MODEL]).tolist()
    seg = lambda i, j: wi[:, offs[i]:offs[j]]
    n_small = 2 * DN_HEADS + GLA_RANK
    w_all = jnp.concatenate([seg(0, 4), seg(6, 10), seg(11, 13), seg(4, 6), seg(10, 11),
                             jnp.zeros((D_MODEL, SMALL_W - n_small), F32)], axis=1).astype(BF16)
    g_mix = norm_mix[0].reshape(1, D_MODEL)

    conv_w = jnp.pad(conv_dn[0], ((0, 8 - CONV_WIDTH), (0, 0)))
    alog = _pad_lanes(a_log[0])
    dtb = _pad_lanes(dt_bias[0])
    nh_dn = norm_head_dn[0].reshape(1, DN_HEAD_DIM)
    lr_off = 2 * DN_HEADS
    wa = jnp.zeros((SMALL_W, GLA_QK), F32).at[lr_off:lr_off + GLA_RANK].set(w_alpha[0])
    ba = b_alpha[0].reshape(1, GLA_QK)
    nh_gla = norm_head_gla[0].reshape(1, GLA_VAL_DIM)

    def mixers(tokens2d, nb, hist, s0_dn, s0_gla, rows):
        qkv, z, qk_g, v_g, r_g, gates, small = _in_proj(tokens2d, g_mix, w_all, rows)
        r3 = lambda a: a.reshape(nb, -1, a.shape[-1])
        o_dn, s_dn = _dn_chunk(r3(qkv), r3(small), r3(z), conv_w, hist, s0_dn, alog, dtb, nh_dn)
        o_gla, s_gla = _gla_chunk(r3(qk_g), r3(v_g), r3(r_g), r3(small), wa, ba, nh_gla, s0_gla)
        return qkv, o_dn, o_gla, gates, s_dn, s_gla

    meta_rows = jnp.pad(meta_tokens.astype(F32), ((CHUNK - N_META, 0), (0, 0)))
    zero_hist = jnp.zeros((HIST_ROWS, 3 * DN_QK), F32)
    zero_dn = jnp.zeros((DN_HEADS, DN_HEAD_DIM, DN_HEAD_DIM), F32)
    zero_gla = jnp.zeros((GLA_HEADS, GLA_VAL_DIM, GLA_KEY_DIM), F32)
    qkv_m, _, _, _, s_dn_m, s_gla_m = mixers(meta_rows, 1, zero_hist, zero_dn, zero_gla, CHUNK)
    hist = qkv_m[CHUNK - HIST_ROWS:].astype(F32)

    x2d = x.reshape(t, d)
    rows = _largest_tile(t, 512)
    _, o_dn, o_gla, gates, _, _ = mixers(x2d, bsz, hist, s_dn_m[0], s_gla_m[0], rows)

    wr_t = jnp.concatenate([w_group[0], w_router[0]], axis=1).T
    wr_t = jnp.pad(wr_t, ((0, LANES - wr_t.shape[0]), (0, 0)))
    br = jnp.pad(jnp.concatenate([b_group[0], b_router[0]]), (0, LANES - N_GROUPS - N_EXPERTS)).reshape(LANES, 1)
    h2, hn, logits_t = _out_proj(
        o_dn.reshape(t, DN_V), o_gla.reshape(t, GLA_V), gates, x2d,
        w_proj_dn[0].astype(BF16), w_proj_gla[0].astype(BF16), w_out[0].astype(BF16),
        norm_ffn[0].reshape(1, D_MODEL), wr_t, br, rows)

    idx, gate, cnt = _route(logits_t, _largest_tile(t, 512))

    sizes = cnt[:, 0].astype(jnp.int32)
    padded = (sizes + MOE_BLOCK - 1) // MOE_BLOCK * MOE_BLOCK
    pends = jnp.cumsum(padded)
    pstart = (pends - padded).astype(jnp.int32)
    n_blocks = (2 * t) // MOE_BLOCK + N_EXPERTS
    n_used = (pends[-1:] // MOE_BLOCK).astype(jnp.int32)
    block_row0 = jnp.arange(n_blocks, dtype=jnp.int32) * MOE_BLOCK
    block_expert = jnp.minimum(jnp.sum(pends[None, :] <= block_row0[:, None], axis=1), N_EXPERTS - 1).astype(jnp.int32)

    tok_tile = _largest_tile(t, 256)
    slots = _slots(pstart, idx, _largest_tile(t, 2048))
    xs = _dispatch(pstart, sizes, slots, hn, n_blocks * MOE_BLOCK, tok_tile)
    yb = _experts(block_expert, n_used, xs, w1[0], w3[0], w2[0])
    out = _combine(slots, gate.T, h2, norm_final.reshape(1, D_MODEL), yb, tok_tile)
    return out.reshape(bsz, seq, d)
```

```python
import functools

import jax
import jax.numpy as jnp
import numpy as np
from jax import lax
from jax.experimental import pallas as pl
from jax.experimental.pallas import tpu as pltpu

F32 = jnp.float32
BF16 = jnp.bfloat16

D_MODEL = 1024
CHUNK = 64
N_META = 16
EPS = 1e-6
DN_HEADS = 8
DN_HEAD_DIM = 128
DN_QK = DN_HEADS * DN_HEAD_DIM
DN_V = DN_HEADS * DN_HEAD_DIM
CONV_WIDTH = 4
GLA_HEADS = 4
GLA_KEY_DIM = 128
GLA_VAL_DIM = 256
GLA_QK = GLA_HEADS * GLA_KEY_DIM
GLA_V = GLA_HEADS * GLA_VAL_DIM
GLA_RANK = 16
GLA_TAU = 16.0
N_GROUPS = 8
EXPERTS_PER_GROUP = 8
N_EXPERTS = N_GROUPS * EXPERTS_PER_GROUP
D_EXPERT = 512
MOE_BLOCK = 512
ZERO_ROWS = 256

LANES = 128
SUBLANES = 8
SMALL_W = LANES
HIST_ROWS = 8
VMEM_LIMIT = 56 * 1024 * 1024

_NT = (((1,), (1,)), ((), ()))
_TN = (((0,), (0,)), ((), ()))


def _bdot(a, b):
    return jnp.dot(a.astype(BF16), b.astype(BF16), preferred_element_type=F32)


def _bdot_nt(a, b):
    return lax.dot_general(a.astype(BF16), b.astype(BF16), _NT, preferred_element_type=F32)


def _bdot_tn(a, b):
    return lax.dot_general(a.astype(BF16), b.astype(BF16), _TN, preferred_element_type=F32)


def _sigmoid(x):
    return 1.0 / (1.0 + jnp.exp(-x))


def _silu(x):
    half = 0.5 * x
    return half + half * jnp.tanh(half)


def _softplus(x):
    return jnp.maximum(x, 0.0) + jnp.log(1.0 + jnp.exp(-jnp.abs(x)))


def _rms(x, eps=EPS):
    return x * lax.rsqrt(jnp.mean(x * x, axis=-1, keepdims=True) + eps)


def _tri_incl(n):
    r = lax.broadcasted_iota(jnp.int32, (n, n), 0)
    c = lax.broadcasted_iota(jnp.int32, (n, n), 1)
    return r, c


def _bf16_part(x):
    bits = pltpu.bitcast(x, jnp.uint32) & jnp.uint32(0xFFFF0000)
    return pltpu.bitcast(bits, F32)


def _cumsum_rows(x):
    r, c = _tri_incl(x.shape[0])
    tri = (r >= c).astype(F32)
    hi = _bf16_part(x)
    r1 = x - hi
    mid = _bf16_part(r1)
    lo = r1 - mid
    return (jnp.dot(tri, hi, preferred_element_type=F32) + jnp.dot(tri, mid, preferred_element_type=F32)
            + jnp.dot(tri, lo, preferred_element_type=F32))


IN_GROUPS = ((3 * DN_QK, BF16), (DN_V, BF16), (2 * GLA_QK, BF16), (GLA_V, BF16), (GLA_V, BF16),
             (2 * D_MODEL, BF16), (SMALL_W, F32))


CONV_COLS = 256


def _inproj_body(tiles_per_seq, x_ref, g_ref, w_ref, cw_ref, hist_ref, *refs):
    o_refs, tail_ref, pre, carry = refs[:-3], refs[-3], refs[-2], refs[-1]
    rows = x_ref.shape[0]

    @pl.when(pl.program_id(0) % tiles_per_seq == 0)
    def _():
        carry[...] = hist_ref[...]

    x = x_ref[...]
    xb = (_rms(x) * g_ref[...]).astype(BF16)

    qkv_ref = o_refs[0]
    for c0 in range(0, qkv_ref.shape[1], CONV_COLS):
        cols = slice(c0, c0 + CONV_COLS)
        pre[0:HIST_ROWS, :] = carry[:, cols]
        pre[HIST_ROWS:HIST_ROWS + rows, :] = jnp.dot(xb, w_ref[:, cols], preferred_element_type=F32)
        carry[:, cols] = pre[rows:rows + HIST_ROWS, :]
        acc = None
        for j in range(CONV_WIDTH):
            lo = HIST_ROWS - (CONV_WIDTH - 1) + j
            term = pre[lo:lo + rows, :] * cw_ref[j:j + 1, cols]
            acc = term if acc is None else acc + term
        qkv_ref[:, cols] = _silu(acc).astype(qkv_ref.dtype)

    col = qkv_ref.shape[1]
    for o_ref in o_refs[1:]:
        width = o_ref.shape[1]
        o_ref[...] = jnp.dot(xb, w_ref[:, col:col + width], preferred_element_type=F32).astype(o_ref.dtype)
        col += width
    tail_ref[...] = carry[...]


def _in_proj(x2d, gain, w_all, conv_w, hist, rows, tiles_per_seq):
    t = x2d.shape[0]
    const = lambda a: pl.BlockSpec(a.shape, lambda i: (0, 0))
    in_specs = [pl.BlockSpec((rows, D_MODEL), lambda i: (i, 0)), const(gain),
                pl.BlockSpec(w_all.shape, lambda i: (0, 0), pipeline_mode=pl.Buffered(1)),
                const(conv_w), const(hist)]
    out_specs = [pl.BlockSpec((rows, w), lambda i: (i, 0)) for w, _ in IN_GROUPS] + [const(hist)]
    out_shape = [jax.ShapeDtypeStruct((t, w), dt) for w, dt in IN_GROUPS] + [jax.ShapeDtypeStruct(hist.shape, F32)]
    return pl.pallas_call(
        functools.partial(_inproj_body, tiles_per_seq), grid=(t // rows,),
        in_specs=in_specs, out_specs=out_specs, out_shape=out_shape,
        scratch_shapes=[pltpu.VMEM((HIST_ROWS + rows, CONV_COLS), F32), pltpu.VMEM(hist.shape, F32)],
        compiler_params=pltpu.CompilerParams(dimension_semantics=("arbitrary",), vmem_limit_bytes=VMEM_LIMIT),
        name="in_proj",
    )(x2d, gain, w_all, conv_w, hist)


def _unit_lower_inverse(mats):
    n = mats[0].shape[0]
    r, c = _tri_incl(n)
    eye = (r == c).astype(F32)
    pair = (r >> 1) == (c >> 1)
    ts = [eye - jnp.where(pair, a, 0.0) for a in mats]
    m = 2
    while m < n:
        sh = m.bit_length() - 1
        keep = ((r >> (sh + 1)) == (c >> (sh + 1))) & ((r >> sh) != (c >> sh))
        ams = [jnp.where(keep, a, 0.0) for a in mats]
        xs = [jnp.dot(t, am, preferred_element_type=F32) for t, am in zip(ts, ams)]
        ys = [jnp.dot(x, t, preferred_element_type=F32) for x, t in zip(xs, ts)]
        ts = [t - y for t, y in zip(ts, ys)]
        m *= 2
    return ts


def _l2_normalise(xs, ones_bd, scale):
    out = []
    for p in range(0, len(xs), 2):
        sq = jnp.concatenate([xs[p] * xs[p], xs[p + 1] * xs[p + 1]], axis=1)
        ss = jnp.dot(sq, ones_bd, preferred_element_type=F32)
        inv = lax.rsqrt(ss + EPS) * scale
        out.append(xs[p] * inv[:, :DN_HEAD_DIM])
        out.append(xs[p + 1] * inv[:, DN_HEAD_DIM:])
    return out


def _dn_body(qkv_ref, sm_ref, z_ref, s0_ref, alog_ref, dtb_ref, nh_ref, ones_ref, o_ref, sfin_ref, s_scr):
    c_id = pl.program_id(1)
    n_rows = qkv_ref.shape[0]
    items = [(b, h) for b in range(n_rows) for h in range(DN_HEADS)]
    n_items = range(len(items))

    @pl.when(c_id == 0)
    def _():
        for b in range(n_rows):
            s_scr[b] = s0_ref[...]

    def head_cols(b, col):
        return qkv_ref[b, :, col:col + DN_HEAD_DIM].astype(F32)

    gc_all, gc_t, beta_all = [], [], []
    for b in range(n_rows):
        sm = sm_ref[b]
        g_b = -jnp.exp(alog_ref[...]) * _softplus(sm + dtb_ref[...])
        beta_all.append(_sigmoid(sm))
        gc_b = _cumsum_rows(g_b)
        gc_all.append(gc_b)
        gc_t.append(jnp.concatenate([gc_b, jnp.zeros_like(gc_b)], axis=0).T)

    r, c = _tri_incl(CHUNK)
    causal = r >= c
    strict = r > c
    scale = DN_HEAD_DIM ** -0.5

    q = [head_cols(b, h * DN_HEAD_DIM) for b, h in items]
    k = [head_cols(b, DN_QK + h * DN_HEAD_DIM) for b, h in items]
    v = [head_cols(b, 2 * DN_QK + h * DN_HEAD_DIM) for b, h in items]
    ones_bd = ones_ref[...]
    q = _l2_normalise(q, ones_bd, scale)
    k = _l2_normalise(k, ones_bd, 1.0)
    gcol = [gc_all[b][:, h:h + 1] for b, h in items]
    grow = [gc_t[b][h:h + 1, 0:CHUNK] for b, h in items]
    bcol = [beta_all[b][:, DN_HEADS + h:DN_HEADS + h + 1] for b, h in items]
    glast = [gc_all[b][CHUNK - 1:CHUNK, h:h + 1] for b, h in items]
    decay = [jnp.where(causal, jnp.exp(jnp.where(causal, gcol[i] - grow[i], 0.0)), 0.0) for i in n_items]
    eg = [jnp.exp(gcol[i]) for i in n_items]
    kk = [lax.dot_general(k[i], k[i], _NT, preferred_element_type=F32) for i in n_items]
    qk = [lax.dot_general(q[i], k[i], _NT, preferred_element_type=F32) for i in n_items]
    a_mat = [jnp.where(strict, bcol[i] * kk[i] * decay[i], 0.0) for i in n_items]
    qk = [jnp.where(causal, qk[i] * decay[i], 0.0) for i in n_items]
    t_inv = _unit_lower_inverse(a_mat)
    rhs = [jnp.concatenate([v[i] * bcol[i], k[i] * (bcol[i] * eg[i])], axis=1) for i in n_items]
    sol = [jnp.dot(t_inv[i], rhs[i], preferred_element_type=F32) for i in n_items]
    s_old = [s_scr[b, h] for b, h in items]
    qe = [q[i] * eg[i] for i in n_items]
    k_dec = [k[i] * jnp.exp(glast[i] - gcol[i]) for i in n_items]
    ws = [jnp.dot(sol[i][:, DN_HEAD_DIM:], s_old[i], preferred_element_type=F32) for i in n_items]
    o_inter = [jnp.dot(qe[i], s_old[i], preferred_element_type=F32) for i in n_items]
    v_new = [sol[i][:, :DN_HEAD_DIM] - ws[i] for i in n_items]
    o_intra = [jnp.dot(qk[i], v_new[i], preferred_element_type=F32) for i in n_items]
    s_add = [lax.dot_general(k_dec[i], v_new[i], _TN, preferred_element_type=F32) for i in n_items]
    for i, (b, h) in enumerate(items):
        col = h * DN_HEAD_DIM
        s_scr[b, h] = s_old[i] * jnp.exp(glast[i]) + s_add[i]
        zed = z_ref[b, :, col:col + DN_HEAD_DIM].astype(F32)
        o = o_inter[i] + o_intra[i]
        o_ref[b, :, col:col + DN_HEAD_DIM] = (_rms(o) * nh_ref[...] * _silu(zed)).astype(o_ref.dtype)

    @pl.when(c_id == pl.num_programs(1) - 1)
    def _():
        sfin_ref[...] = s_scr[...]


def _rows_per_step(b, want):
    while b % want:
        want //= 2
    return want


def _dn_chunk(qkv, small, z, s0, alog, dtb, nh):
    b, l, _ = qkv.shape
    lane_head = np.arange(2 * DN_HEAD_DIM) // DN_HEAD_DIM
    ones_bd = jnp.asarray(lane_head[:, None] == lane_head[None, :], F32)
    nc = l // CHUNK
    nr = _rows_per_step(b, 4)
    const2 = lambda bi, ci: (0, 0)
    blk = lambda w: pl.BlockSpec((nr, CHUNK, w), lambda bi, ci: (bi, ci, 0))
    return pl.pallas_call(
        _dn_body, grid=(b // nr, nc),
        in_specs=[
            blk(3 * DN_QK), blk(SMALL_W), blk(DN_V),
            pl.BlockSpec(s0.shape, lambda bi, ci: (0, 0, 0)),
            pl.BlockSpec(alog.shape, const2),
            pl.BlockSpec(dtb.shape, const2),
            pl.BlockSpec(nh.shape, const2),
            pl.BlockSpec(ones_bd.shape, const2),
        ],
        out_specs=[
            blk(DN_V),
            pl.BlockSpec((nr, DN_HEADS, DN_HEAD_DIM, DN_HEAD_DIM), lambda bi, ci: (bi, 0, 0, 0)),
        ],
        out_shape=[
            jax.ShapeDtypeStruct((b, l, DN_V), BF16),
            jax.ShapeDtypeStruct((b, DN_HEADS, DN_HEAD_DIM, DN_HEAD_DIM), F32),
        ],
        scratch_shapes=[pltpu.VMEM((nr, DN_HEADS, DN_HEAD_DIM, DN_HEAD_DIM), F32)],
        compiler_params=pltpu.CompilerParams(dimension_semantics=("parallel", "arbitrary"),
                                             vmem_limit_bytes=VMEM_LIMIT),
        name="dn_chunk",
    )(qkv, small, z, s0, alog, dtb, nh, ones_bd)


def _gla_body(qk_ref, v_ref, r_ref, sm_ref, wa_ref, ba_ref, nh_ref, s0_ref, o_ref, sfin_ref, s_scr):
    c_id = pl.program_id(1)
    n_rows = qk_ref.shape[0]
    items = [(b, h) for b in range(n_rows) for h in range(GLA_HEADS)]
    n_items = range(len(items))

    @pl.when(c_id == 0)
    def _():
        for b in range(n_rows):
            s_scr[b] = s0_ref[...]

    b_all = []
    for b in range(n_rows):
        la = jnp.dot(sm_ref[b], wa_ref[...], preferred_element_type=F32) + ba_ref[...]
        log_alpha = (jnp.minimum(la, 0.0) - jnp.log(1.0 + jnp.exp(-jnp.abs(la)))) * (1.0 / GLA_TAU)
        b_all.append(_cumsum_rows(log_alpha))

    r, c = _tri_incl(CHUNK)
    causal = r >= c
    scale = GLA_KEY_DIM ** -0.5
    mid = CHUNK // 2 - 1

    q = [qk_ref[b, :, h * GLA_KEY_DIM:(h + 1) * GLA_KEY_DIM].astype(F32) * scale for b, h in items]
    k = [qk_ref[b, :, GLA_QK + h * GLA_KEY_DIM:GLA_QK + (h + 1) * GLA_KEY_DIM].astype(F32) for b, h in items]
    v = [v_ref[b, :, h * GLA_VAL_DIM:(h + 1) * GLA_VAL_DIM].astype(F32) for b, h in items]
    bh = [b_all[b][:, h * GLA_KEY_DIM:(h + 1) * GLA_KEY_DIM] for b, h in items]
    bmid = [x[mid:mid + 1, :] for x in bh]
    blast = [x[CHUNK - 1:CHUNK, :] for x in bh]
    qs = [q[i] * jnp.exp(bh[i] - bmid[i]) for i in n_items]
    ks = [k[i] * jnp.exp(bmid[i] - bh[i]) for i in n_items]
    qd = [q[i] * jnp.exp(bh[i]) for i in n_items]
    kd = [k[i] * jnp.exp(blast[i] - bh[i]) for i in n_items]
    st = [s_scr[b, h] for b, h in items]
    att = [lax.dot_general(qs[i], ks[i], _NT, preferred_element_type=F32) for i in n_items]
    o_inter = [lax.dot_general(qd[i], st[i], _NT, preferred_element_type=F32) for i in n_items]
    s_add = [lax.dot_general(v[i], kd[i], _TN, preferred_element_type=F32) for i in n_items]
    att = [jnp.where(causal, a, 0.0) for a in att]
    o_intra = [jnp.dot(att[i], v[i], preferred_element_type=F32) for i in n_items]
    for i, (b, h) in enumerate(items):
        vc = h * GLA_VAL_DIM
        s_scr[b, h] = st[i] * jnp.exp(blast[i]) + s_add[i]
        gate = _silu(r_ref[b, :, vc:vc + GLA_VAL_DIM].astype(F32))
        o = o_inter[i] + o_intra[i]
        o_ref[b, :, vc:vc + GLA_VAL_DIM] = (_rms(o) * nh_ref[...] * gate).astype(o_ref.dtype)

    @pl.when(c_id == pl.num_programs(1) - 1)
    def _():
        sfin_ref[...] = s_scr[...]


def _gla_chunk(qk, v, rr, small, wa, ba, nh, s0):
    b, l, _ = qk.shape
    nc = l // CHUNK
    nr = _rows_per_step(b, 4)
    const2 = lambda bi, ci: (0, 0)
    blk = lambda w: pl.BlockSpec((nr, CHUNK, w), lambda bi, ci: (bi, ci, 0))
    return pl.pallas_call(
        _gla_body, grid=(b // nr, nc),
        in_specs=[
            blk(2 * GLA_QK), blk(GLA_V), blk(GLA_V), blk(SMALL_W),
            pl.BlockSpec(wa.shape, const2), pl.BlockSpec(ba.shape, const2), pl.BlockSpec(nh.shape, const2),
            pl.BlockSpec(s0.shape, lambda bi, ci: (0, 0, 0)),
        ],
        out_specs=[
            blk(GLA_V),
            pl.BlockSpec((nr, GLA_HEADS, GLA_VAL_DIM, GLA_KEY_DIM), lambda bi, ci: (bi, 0, 0, 0)),
        ],
        out_shape=[
            jax.ShapeDtypeStruct((b, l, GLA_V), BF16),
            jax.ShapeDtypeStruct((b, GLA_HEADS, GLA_VAL_DIM, GLA_KEY_DIM), F32),
        ],
        scratch_shapes=[pltpu.VMEM((nr, GLA_HEADS, GLA_VAL_DIM, GLA_KEY_DIM), F32)],
        compiler_params=pltpu.CompilerParams(dimension_semantics=("parallel", "arbitrary"),
                                             vmem_limit_bytes=VMEM_LIMIT),
        name="gla_chunk",
    )(qk, v, rr, small, wa, ba, nh, s0)


TOK_SUB = D_MODEL // LANES
assert TOK_SUB == SUBLANES


def _tiles_to_rows(ref, n):
    return jnp.concatenate([ref[pl.ds(s, n, stride=TOK_SUB), :] for s in range(TOK_SUB)], axis=1)


def _rows_to_tiles(ref, value):
    n = value.shape[0]
    for s in range(TOK_SUB):
        ref[pl.ds(s, n, stride=TOK_SUB), :] = value[:, s * LANES:(s + 1) * LANES]


def _tile_rows(tok, count=1):
    return pl.ds(pl.multiple_of(tok * TOK_SUB, TOK_SUB), count * TOK_SUB)


def _outproj_body(odn_ref, ogla_ref, gates_ref, x_ref, wd_ref, wg_ref, wo_ref, gn_ref, wr_ref, br_ref,
                  h2_ref, hn_ref, lg_ref):
    y_dn = jnp.dot(odn_ref[...], wd_ref[...], preferred_element_type=F32)
    y_gla = jnp.dot(ogla_ref[...], wg_ref[...], preferred_element_type=F32)
    gd = _sigmoid(gates_ref[:, 0:D_MODEL].astype(F32))
    gg = _sigmoid(gates_ref[:, D_MODEL:2 * D_MODEL].astype(F32))
    merged = gd * y_dn + gg * y_gla
    h2 = x_ref[...] + _bdot(merged, wo_ref[...])
    h2_ref[...] = h2
    hn = _rms(h2) * gn_ref[...]
    _rows_to_tiles(hn_ref, hn)
    lg_ref[...] = lax.dot_general(wr_ref[...], hn, _NT, preferred_element_type=F32,
                                  precision=lax.Precision.HIGHEST) + br_ref[...]


def _out_proj(o_dn, o_gla, gates, x2d, wd, wg, wo, gn, wr_t, br, rows):
    t = x2d.shape[0]
    row_blk = lambda w: pl.BlockSpec((rows, w), lambda i: (i, 0))
    const = lambda a: pl.BlockSpec(a.shape, lambda i: (0, 0))
    return pl.pallas_call(
        _outproj_body, grid=(t // rows,),
        in_specs=[row_blk(DN_V), row_blk(GLA_V), row_blk(2 * D_MODEL), row_blk(D_MODEL),
                  const(wd), const(wg), const(wo), const(gn), const(wr_t), const(br)],
        out_specs=[row_blk(D_MODEL), pl.BlockSpec((rows * TOK_SUB, LANES), lambda i: (i, 0)),
                   pl.BlockSpec((LANES, rows), lambda i: (0, i))],
        out_shape=[jax.ShapeDtypeStruct((t, D_MODEL), F32), jax.ShapeDtypeStruct((t * TOK_SUB, LANES), F32),
                   jax.ShapeDtypeStruct((LANES, t), F32)],
        compiler_params=pltpu.CompilerParams(dimension_semantics=("parallel",), vmem_limit_bytes=VMEM_LIMIT),
        name="out_proj",
    )(o_dn, o_gla, gates, x2d, wd, wg, wo, gn, wr_t, br)


ROUTE_SUB = 256


def _route_body(lg_ref, idx_ref, gate_ref, cnt_ref, carry):
    step = pl.program_id(0)

    @pl.when(step == 0)
    def _():
        carry[...] = jnp.zeros_like(carry)

    tt = lg_ref.shape[1]
    gl = lg_ref[0:N_GROUPS, :]
    gmax = jnp.max(gl, axis=0, keepdims=True)
    rid8 = lax.broadcasted_iota(jnp.int32, (N_GROUPS, tt), 0)
    gsel = jnp.min(jnp.where(gl == gmax, rid8, N_GROUPS), axis=0, keepdims=True)
    gw = 1.0 / jnp.sum(jnp.exp(gl - gmax), axis=0, keepdims=True)
    el = lg_ref[N_GROUPS:N_GROUPS + N_EXPERTS, :]
    rid = lax.broadcasted_iota(jnp.int32, (N_EXPERTS, tt), 0)
    neg = jnp.float32(-jnp.inf)
    ein = jnp.where((rid >> 3) == gsel, el, neg)
    t1 = jnp.max(ein, axis=0, keepdims=True)
    i1 = jnp.min(jnp.where(ein == t1, rid, N_EXPERTS), axis=0, keepdims=True)
    ein2 = jnp.where(rid == i1, neg, ein)
    t2 = jnp.max(ein2, axis=0, keepdims=True)
    i2 = jnp.min(jnp.where(ein2 == t2, rid, N_EXPERTS), axis=0, keepdims=True)
    e21 = jnp.exp(t2 - t1)
    den = 1.0 / (1.0 + e21)
    sel1 = rid == i1
    sel2 = rid == i2
    onehot = jnp.where(sel1 | sel2, 1.0, 0.0)

    ur, uc = _tri_incl(ROUTE_SUB)
    upper = (ur <= uc).astype(BF16)
    run = carry[...]
    r1_parts, r2_parts = [], []
    for s in range(tt // ROUTE_SUB):
        sl = slice(s * ROUTE_SUB, (s + 1) * ROUTE_SUB)
        oh = onehot[:, sl]
        incl = jnp.dot(oh.astype(BF16), upper, preferred_element_type=F32) + run
        excl = incl - oh
        r1_parts.append(jnp.sum(jnp.where(sel1[:, sl], excl, 0.0), axis=0, keepdims=True))
        r2_parts.append(jnp.sum(jnp.where(sel2[:, sl], excl, 0.0), axis=0, keepdims=True))
        run = jnp.broadcast_to(incl[:, ROUTE_SUB - 1:ROUTE_SUB], run.shape)
    carry[...] = run
    r1 = jnp.concatenate(r1_parts, axis=1) if len(r1_parts) > 1 else r1_parts[0]
    r2 = jnp.concatenate(r2_parts, axis=1) if len(r2_parts) > 1 else r2_parts[0]

    idx_ref[...] = jnp.zeros_like(idx_ref)
    idx_ref[0:1, :] = i1
    idx_ref[1:2, :] = i2
    idx_ref[2:3, :] = r1.astype(jnp.int32)
    idx_ref[3:4, :] = r2.astype(jnp.int32)
    gate_ref[...] = jnp.zeros_like(gate_ref)
    gate_ref[0:1, :] = den * gw
    gate_ref[1:2, :] = e21 * den * gw
    cnt_ref[...] = run[:, 0:LANES]


def _route(logits_t, lanes):
    t = logits_t.shape[1]
    return pl.pallas_call(
        _route_body, grid=(t // lanes,),
        in_specs=[pl.BlockSpec((LANES, lanes), lambda i: (0, i))],
        out_specs=[pl.BlockSpec((8, lanes), lambda i: (0, i)), pl.BlockSpec((8, lanes), lambda i: (0, i)),
                   pl.BlockSpec((N_EXPERTS, LANES), lambda i: (0, 0))],
        out_shape=[jax.ShapeDtypeStruct((8, t), jnp.int32), jax.ShapeDtypeStruct((8, t), F32),
                   jax.ShapeDtypeStruct((N_EXPERTS, LANES), F32)],
        scratch_shapes=[pltpu.VMEM((N_EXPERTS, ROUTE_SUB), F32)],
        compiler_params=pltpu.CompilerParams(dimension_semantics=("arbitrary",)),
        name="route",
    )(logits_t)


def _row_copy(src_ref, src_tok, dst_ref, dst_tok, sem):
    return pltpu.make_async_copy(src_ref.at[_tile_rows(src_tok)], dst_ref.at[_tile_rows(dst_tok)], sem)


ROW_UNROLL = 8


def _slots_body(pstart_ref, idx_ref, slot_ref):
    idx = idx_ref[...]
    base = jnp.zeros_like(idx)
    for e in range(N_EXPERTS):
        base = jnp.where(idx == e, pstart_ref[e], base)
    slot_ref[...] = jnp.zeros_like(idx)
    slot_ref[0:2, :] = base[0:2, :] + idx[2:4, :]


def _slots(pstart, idx, lanes):
    t = idx.shape[1]
    grid_spec = pltpu.PrefetchScalarGridSpec(
        num_scalar_prefetch=1, grid=(t // lanes,),
        in_specs=[pl.BlockSpec((8, lanes), lambda i, ps: (0, i))],
        out_specs=pl.BlockSpec((8, lanes), lambda i, ps: (0, i)),
    )
    return pl.pallas_call(
        _slots_body, grid_spec=grid_spec, out_shape=jax.ShapeDtypeStruct(idx.shape, jnp.int32),
        compiler_params=pltpu.CompilerParams(dimension_semantics=("parallel",)),
        name="slots",
    )(pstart, idx)


def _dispatch_body(pstart_ref, size_ref, slot_ref, hn_ref, xs_ref, zblk, sem):
    n_tok = hn_ref.shape[0] // TOK_SUB

    @pl.when(pl.program_id(0) == 0)
    def _():
        zblk[...] = jnp.zeros_like(zblk)
        parts = MOE_BLOCK // ZERO_ROWS

        def clear_copy(row0, part):
            return pltpu.make_async_copy(zblk, xs_ref.at[_tile_rows(row0 + part * ZERO_ROWS, ZERO_ROWS)], sem)

        def per_expert(e, n_rows):
            size = size_ref[e]
            full = size // MOE_BLOCK * MOE_BLOCK

            @pl.when(size != full)
            def _():
                for part in range(parts):
                    clear_copy(pstart_ref[e] + full, part).start()
                for part in range(parts):
                    clear_copy(pstart_ref[e] + full, part).wait()

            return n_rows + (size + MOE_BLOCK - 1) // MOE_BLOCK * MOE_BLOCK

        n_rows = lax.fori_loop(0, N_EXPERTS, per_expert, 0)

        def clear_unused(j, c):
            for part in range(parts):
                clear_copy(j * MOE_BLOCK, part).start()
            for part in range(parts):
                clear_copy(j * MOE_BLOCK, part).wait()
            return c

        lax.fori_loop(n_rows // MOE_BLOCK, xs_ref.shape[0] // (MOE_BLOCK * TOK_SUB), clear_unused, 0)

    def issue(t, carry):
        for k in range(2):
            _row_copy(hn_ref, t, xs_ref, slot_ref[k, t], sem).start()
        return carry

    lax.fori_loop(0, n_tok, issue, 0, unroll=ROW_UNROLL)
    for k in range(2):
        pltpu.make_async_copy(hn_ref, xs_ref.at[_tile_rows(0, n_tok)], sem).wait()


def _dispatch(pstart, sizes, slots, hn, n_rows_out, tokens):
    t = hn.shape[0] // TOK_SUB
    grid_spec = pltpu.PrefetchScalarGridSpec(
        num_scalar_prefetch=2, grid=(t // tokens,),
        in_specs=[pl.BlockSpec((8, tokens), lambda i, ps, sz: (0, i), memory_space=pltpu.SMEM),
                  pl.BlockSpec((tokens * TOK_SUB, LANES), lambda i, ps, sz: (i, 0))],
        out_specs=pl.BlockSpec(memory_space=pl.ANY),
        scratch_shapes=[pltpu.VMEM((ZERO_ROWS * TOK_SUB, LANES), F32), pltpu.SemaphoreType.DMA(())],
    )
    return pl.pallas_call(
        _dispatch_body, grid_spec=grid_spec,
        out_shape=jax.ShapeDtypeStruct((n_rows_out * TOK_SUB, LANES), F32),
        compiler_params=pltpu.CompilerParams(dimension_semantics=("arbitrary",)),
        name="dispatch",
    )(pstart, sizes, slots, hn)


def _experts_body(be_ref, nused_ref, xs_ref, w1_ref, w3_ref, w2_ref, yb_ref, w1b, w3b, w2b):
    j = pl.program_id(0)
    used = j < nused_ref[0]
    new_expert = jnp.logical_or(j == 0, be_ref[j] != be_ref[jnp.maximum(j - 1, 0)])

    @pl.when(jnp.logical_and(used, new_expert))
    def _():
        w1b[...] = w1_ref[...].astype(BF16)
        w3b[...] = w3_ref[...].astype(BF16)
        w2b[...] = w2_ref[...].astype(BF16)

    @pl.when(used)
    def _():
        xb = _tiles_to_rows(xs_ref, MOE_BLOCK).astype(BF16)
        h1 = jnp.dot(xb, w1b[...], preferred_element_type=F32)
        h3 = jnp.dot(xb, w3b[...], preferred_element_type=F32)
        hid = (_silu(h1) * h3).astype(BF16)
        _rows_to_tiles(yb_ref, jnp.dot(hid, w2b[...], preferred_element_type=F32))

    @pl.when(jnp.logical_not(used))
    def _():
        yb_ref[...] = jnp.zeros_like(yb_ref)


def _experts(block_expert, n_used, xs, w1, w3, w2):
    p = xs.shape[0] // TOK_SUB
    nb = p // MOE_BLOCK
    row_map = lambda j, be, nu: (jnp.minimum(j, nu[0] - 1), 0)
    w_map = lambda j, be, nu: (be[j], 0, 0)
    grid_spec = pltpu.PrefetchScalarGridSpec(
        num_scalar_prefetch=2, grid=(nb,),
        in_specs=[pl.BlockSpec((MOE_BLOCK * TOK_SUB, LANES), row_map),
                  pl.BlockSpec((None, D_MODEL, D_EXPERT), w_map),
                  pl.BlockSpec((None, D_MODEL, D_EXPERT), w_map),
                  pl.BlockSpec((None, D_EXPERT, D_MODEL), w_map)],
        out_specs=pl.BlockSpec((MOE_BLOCK * TOK_SUB, LANES), lambda j, be, nu: (j, 0)),
        scratch_shapes=[pltpu.VMEM((D_MODEL, D_EXPERT), BF16), pltpu.VMEM((D_MODEL, D_EXPERT), BF16),
                        pltpu.VMEM((D_EXPERT, D_MODEL), BF16)],
    )
    return pl.pallas_call(
        _experts_body, grid_spec=grid_spec,
        out_shape=jax.ShapeDtypeStruct((p * TOK_SUB, LANES), F32),
        compiler_params=pltpu.CompilerParams(dimension_semantics=("arbitrary",), vmem_limit_bytes=VMEM_LIMIT),
        name="experts",
    )(block_expert, n_used, xs, w1, w3, w2)


def _combine_body(slot_ref, slot_next_ref, gate_ref, h2_ref, gf_ref, yb_ref, o_ref, ybuf, sems):
    n_tok = h2_ref.shape[0]
    step = pl.program_id(0)
    cur = step % 2

    def gather(slots, buf_id):
        def issue(t, carry):
            for k in range(2):
                pltpu.make_async_copy(yb_ref.at[_tile_rows(slots[k, t])], ybuf.at[buf_id, k, _tile_rows(t)],
                                      sems.at[buf_id]).start()
            return carry

        lax.fori_loop(0, n_tok, issue, 0, unroll=ROW_UNROLL)

    @pl.when(step == 0)
    def _():
        gather(slot_ref, 0)

    @pl.when(step + 1 < pl.num_programs(0))
    def _():
        gather(slot_next_ref, 1 - cur)

    for k in range(2):
        pltpu.make_async_copy(yb_ref.at[_tile_rows(0, n_tok)], ybuf.at[cur, k], sems.at[cur]).wait()
    y0 = _tiles_to_rows(ybuf.at[cur, 0], n_tok)
    y1 = _tiles_to_rows(ybuf.at[cur, 1], n_tok)
    y = h2_ref[...] + gate_ref[:, 0:1] * y0 + gate_ref[:, 1:2] * y1
    o_ref[...] = _rms(y) * gf_ref[...]


def _combine(slots, gate_t, h2, gf, yb, tokens):
    t = h2.shape[0]
    last = t // tokens - 1
    return pl.pallas_call(
        _combine_body, grid=(t // tokens,),
        in_specs=[pl.BlockSpec((8, tokens), lambda i: (0, i), memory_space=pltpu.SMEM),
                  pl.BlockSpec((8, tokens), lambda i: (0, jnp.minimum(i + 1, last)), memory_space=pltpu.SMEM),
                  pl.BlockSpec((tokens, 8), lambda i: (i, 0)),
                  pl.BlockSpec((tokens, D_MODEL), lambda i: (i, 0)),
                  pl.BlockSpec((1, D_MODEL), lambda i: (0, 0)),
                  pl.BlockSpec(memory_space=pl.ANY)],
        out_specs=pl.BlockSpec((tokens, D_MODEL), lambda i: (i, 0)),
        scratch_shapes=[pltpu.VMEM((2, 2, tokens * TOK_SUB, LANES), F32), pltpu.SemaphoreType.DMA((2,))],
        out_shape=jax.ShapeDtypeStruct((t, D_MODEL), F32),
        compiler_params=pltpu.CompilerParams(dimension_semantics=("arbitrary",)),
        name="combine",
    )(slots, slots, gate_t, h2, gf, yb)


def _pad_lanes(v, width=LANES):
    v = v.reshape(1, -1).astype(F32)
    return jnp.pad(v, ((0, 0), (0, width - v.shape[1])))


def _largest_tile(n, cap):
    t = cap
    while n % t:
        t //= 2
    return t


def kernel(x, meta_tokens, norm_mix, w_in, conv_dn, a_log, dt_bias, norm_head_dn, w_proj_dn, w_alpha, b_alpha,
           norm_head_gla, w_proj_gla, w_out, norm_ffn, w_group, b_group, w_router, b_router, w1, w3, w2, norm_final):
    assert norm_mix.shape[0] == 1, "single-layer block"
    bsz, seq, d = x.shape
    assert d == D_MODEL and seq % CHUNK == 0
    t = bsz * seq

    wi = w_in[0]
    offs = np.cumsum([0, DN_QK, DN_QK, DN_V, DN_V, DN_HEADS, DN_HEADS, GLA_QK, GLA_QK, GLA_V, GLA_V, GLA_RANK,
                      D_MODEL, D_MODEL]).tolist()
    seg = lambda i, j: wi[:, offs[i]:offs[j]]
    n_small = 2 * DN_HEADS + GLA_RANK
    w_all = jnp.concatenate([seg(0, 4), seg(6, 10), seg(11, 13), seg(4, 6), seg(10, 11),
                             jnp.zeros((D_MODEL, SMALL_W - n_small), F32)], axis=1).astype(BF16)
    g_mix = norm_mix[0].reshape(1, D_MODEL)

    conv_w = jnp.pad(conv_dn[0], ((0, 8 - CONV_WIDTH), (0, 0)))
    alog = _pad_lanes(a_log[0])
    dtb = _pad_lanes(dt_bias[0])
    nh_dn = norm_head_dn[0].reshape(1, DN_HEAD_DIM)
    lr_off = 2 * DN_HEADS
    wa = jnp.zeros((SMALL_W, GLA_QK), F32).at[lr_off:lr_off + GLA_RANK].set(w_alpha[0])
    ba = b_alpha[0].reshape(1, GLA_QK)
    nh_gla = norm_head_gla[0].reshape(1, GLA_VAL_DIM)

    def mixers(tokens2d, nb, hist, s0_dn, s0_gla, rows):
        tiles_per_seq = tokens2d.shape[0] // nb // rows
        qkv, z, qk_g, v_g, r_g, gates, small, tail = _in_proj(tokens2d, g_mix, w_all, conv_w, hist, rows, tiles_per_seq)
        r3 = lambda a: a.reshape(nb, -1, a.shape[-1])
        o_dn, s_dn = _dn_chunk(r3(qkv), r3(small), r3(z), s0_dn, alog, dtb, nh_dn)
        o_gla, s_gla = _gla_chunk(r3(qk_g), r3(v_g), r3(r_g), r3(small), wa, ba, nh_gla, s0_gla)
        return tail, o_dn, o_gla, gates, s_dn, s_gla

    meta_rows = jnp.pad(meta_tokens.astype(F32), ((CHUNK - N_META, 0), (0, 0)))
    zero_hist = jnp.zeros((HIST_ROWS, 3 * DN_QK), F32)
    zero_dn = jnp.zeros((DN_HEADS, DN_HEAD_DIM, DN_HEAD_DIM), F32)
    zero_gla = jnp.zeros((GLA_HEADS, GLA_VAL_DIM, GLA_KEY_DIM), F32)
    hist, _, _, _, s_dn_m, s_gla_m = mixers(meta_rows, 1, zero_hist, zero_dn, zero_gla, CHUNK)

    x2d = x.reshape(t, d)
    rows = _largest_tile(seq, 512)
    _, o_dn, o_gla, gates, _, _ = mixers(x2d, bsz, hist, s_dn_m[0], s_gla_m[0], rows)

    wr_t = jnp.concatenate([w_group[0], w_router[0]], axis=1).T
    wr_t = jnp.pad(wr_t, ((0, LANES - wr_t.shape[0]), (0, 0)))
    br = jnp.pad(jnp.concatenate([b_group[0], b_router[0]]), (0, LANES - N_GROUPS - N_EXPERTS)).reshape(LANES, 1)
    h2, hn, logits_t = _out_proj(
        o_dn.reshape(t, DN_V), o_gla.reshape(t, GLA_V), gates, x2d,
        w_proj_dn[0].astype(BF16), w_proj_gla[0].astype(BF16), w_out[0].astype(BF16),
        norm_ffn[0].reshape(1, D_MODEL), wr_t, br, rows)

    idx, gate, cnt = _route(logits_t, _largest_tile(t, 512))

    sizes = cnt[:, 0].astype(jnp.int32)
    padded = (sizes + MOE_BLOCK - 1) // MOE_BLOCK * MOE_BLOCK
    pends = jnp.cumsum(padded)
    pstart = (pends - padded).astype(jnp.int32)
    n_blocks = (2 * t) // MOE_BLOCK + N_EXPERTS
    n_used = (pends[-1:] // MOE_BLOCK).astype(jnp.int32)
    block_row0 = jnp.arange(n_blocks, dtype=jnp.int32) * MOE_BLOCK
    block_expert = jnp.minimum(jnp.sum(pends[None, :] <= block_row0[:, None], axis=1), N_EXPERTS - 1).astype(jnp.int32)

    tok_tile = _largest_tile(t, 256)
    slots = _slots(pstart, idx, _largest_tile(t, 2048))
    xs = _dispatch(pstart, sizes, slots, hn, n_blocks * MOE_BLOCK, tok_tile)
    yb = _experts(block_expert, n_used, xs, w1[0], w3[0], w2[0])
    out = _combine(slots, gate.T, h2, norm_final.reshape(1, D_MODEL), yb, tok_tile)
    return out.reshape(bsz, seq, d)
```

```python
import functools

import jax
import jax.numpy as jnp
import numpy as np
from jax import lax
from jax.experimental import pallas as pl
from jax.experimental.pallas import tpu as pltpu

F32 = jnp.float32
BF16 = jnp.bfloat16

D_MODEL = 1024
CHUNK = 64
N_META = 16
EPS = 1e-6
DN_HEADS = 8
DN_HEAD_DIM = 128
DN_QK = DN_HEADS * DN_HEAD_DIM
DN_V = DN_HEADS * DN_HEAD_DIM
CONV_WIDTH = 4
GLA_HEADS = 4
GLA_KEY_DIM = 128
GLA_VAL_DIM = 256
GLA_QK = GLA_HEADS * GLA_KEY_DIM
GLA_V = GLA_HEADS * GLA_VAL_DIM
GLA_RANK = 16
GLA_TAU = 16.0
N_GROUPS = 8
EXPERTS_PER_GROUP = 8
N_EXPERTS = N_GROUPS * EXPERTS_PER_GROUP
D_EXPERT = 512
MOE_BLOCK = 512
ZERO_ROWS = 256

LANES = 128
SUBLANES = 8
SMALL_W = LANES
HIST_ROWS = 8
VMEM_LIMIT = 56 * 1024 * 1024

_NT = (((1,), (1,)), ((), ()))
_TN = (((0,), (0,)), ((), ()))


def _bdot(a, b):
    return jnp.dot(a.astype(BF16), b.astype(BF16), preferred_element_type=F32)


def _bdot_nt(a, b):
    return lax.dot_general(a.astype(BF16), b.astype(BF16), _NT, preferred_element_type=F32)


def _bdot_tn(a, b):
    return lax.dot_general(a.astype(BF16), b.astype(BF16), _TN, preferred_element_type=F32)


def _sigmoid(x):
    return 1.0 / (1.0 + jnp.exp(-x))


def _silu(x):
    half = 0.5 * x
    return half + half * jnp.tanh(half)


def _softplus(x):
    return jnp.maximum(x, 0.0) + jnp.log(1.0 + jnp.exp(-jnp.abs(x)))


def _rms(x, eps=EPS):
    return x * lax.rsqrt(jnp.mean(x * x, axis=-1, keepdims=True) + eps)


def _tri_incl(n):
    r = lax.broadcasted_iota(jnp.int32, (n, n), 0)
    c = lax.broadcasted_iota(jnp.int32, (n, n), 1)
    return r, c


def _bf16_part(x):
    bits = pltpu.bitcast(x, jnp.uint32) & jnp.uint32(0xFFFF0000)
    return pltpu.bitcast(bits, F32)


def _cumsum_rows(x):
    r, c = _tri_incl(x.shape[0])
    tri = (r >= c).astype(F32)
    hi = _bf16_part(x)
    r1 = x - hi
    mid = _bf16_part(r1)
    lo = r1 - mid
    return (jnp.dot(tri, hi, preferred_element_type=F32) + jnp.dot(tri, mid, preferred_element_type=F32)
            + jnp.dot(tri, lo, preferred_element_type=F32))


IN_GROUPS = ((3 * DN_QK, BF16), (DN_V, BF16), (2 * GLA_QK, BF16), (GLA_V, BF16), (GLA_V, BF16),
             (2 * D_MODEL, BF16), (SMALL_W, F32))


CONV_COLS = 256
PLAIN_COLS = 512


def _inproj_body(tiles_per_seq, x_ref, g_ref, w_ref, cw_ref, hist_ref, *refs):
    o_refs, tail_ref, pre, carry = refs[:-3], refs[-3], refs[-2], refs[-1]
    rows = x_ref.shape[0]

    @pl.when(pl.program_id(0) % tiles_per_seq == 0)
    def _():
        carry[...] = hist_ref[...]

    x = x_ref[...]
    xb = (_rms(x) * g_ref[...]).astype(BF16)

    qkv_ref = o_refs[0]

    def conv_pass(c0):
        cols = slice(c0, c0 + CONV_COLS)
        pre[0:HIST_ROWS, :] = carry[:, cols]
        pre[HIST_ROWS:HIST_ROWS + rows, :] = jnp.dot(xb, w_ref[:, cols], preferred_element_type=F32)
        carry[:, cols] = pre[rows:rows + HIST_ROWS, :]
        acc = None
        for j in range(CONV_WIDTH):
            lo = HIST_ROWS - (CONV_WIDTH - 1) + j
            term = pre[lo:lo + rows, :] * cw_ref[j:j + 1, cols]
            acc = term if acc is None else acc + term
        qkv_ref[:, cols] = _silu(acc).astype(qkv_ref.dtype)

    def plain_pass(o_ref, c0, width, wcol):
        o_ref[:, c0:c0 + width] = jnp.dot(xb, w_ref[:, wcol + c0:wcol + c0 + width],
                                          preferred_element_type=F32).astype(o_ref.dtype)

    plain_tasks = []
    wcol = qkv_ref.shape[1]
    for o_ref in o_refs[1:]:
        width = o_ref.shape[1]
        step = min(width, PLAIN_COLS)
        plain_tasks += [functools.partial(plain_pass, o_ref, c0, step, wcol) for c0 in range(0, width, step)]
        wcol += width

    conv_tasks = [functools.partial(conv_pass, c0) for c0 in range(0, qkv_ref.shape[1], CONV_COLS)]
    for i in range(max(len(conv_tasks), len(plain_tasks))):
        if i < len(plain_tasks):
            plain_tasks[i]()
        if i < len(conv_tasks):
            conv_tasks[i]()
    tail_ref[...] = carry[...]


def _in_proj(x2d, gain, w_all, conv_w, hist, rows, tiles_per_seq):
    t = x2d.shape[0]
    const = lambda a: pl.BlockSpec(a.shape, lambda i: (0, 0))
    in_specs = [pl.BlockSpec((rows, D_MODEL), lambda i: (i, 0)), const(gain),
                pl.BlockSpec(w_all.shape, lambda i: (0, 0), pipeline_mode=pl.Buffered(1)),
                const(conv_w), const(hist)]
    out_specs = [pl.BlockSpec((rows, w), lambda i: (i, 0)) for w, _ in IN_GROUPS] + [const(hist)]
    out_shape = [jax.ShapeDtypeStruct((t, w), dt) for w, dt in IN_GROUPS] + [jax.ShapeDtypeStruct(hist.shape, F32)]
    return pl.pallas_call(
        functools.partial(_inproj_body, tiles_per_seq), grid=(t // rows,),
        in_specs=in_specs, out_specs=out_specs, out_shape=out_shape,
        scratch_shapes=[pltpu.VMEM((HIST_ROWS + rows, CONV_COLS), F32), pltpu.VMEM(hist.shape, F32)],
        compiler_params=pltpu.CompilerParams(dimension_semantics=("arbitrary",), vmem_limit_bytes=VMEM_LIMIT),
        name="in_proj",
    )(x2d, gain, w_all, conv_w, hist)


def _unit_lower_inverse(mats):
    n = mats[0].shape[0]
    r, c = _tri_incl(n)
    eye = (r == c).astype(F32)
    pair = (r >> 1) == (c >> 1)
    ts = [eye - jnp.where(pair, a, 0.0) for a in mats]
    m = 2
    while m < n:
        sh = m.bit_length() - 1
        keep = ((r >> (sh + 1)) == (c >> (sh + 1))) & ((r >> sh) != (c >> sh))
        ams = [jnp.where(keep, a, 0.0) for a in mats]
        xs = [jnp.dot(t, am, preferred_element_type=F32) for t, am in zip(ts, ams)]
        ys = [jnp.dot(x, t, preferred_element_type=F32) for x, t in zip(xs, ts)]
        ts = [t - y for t, y in zip(ts, ys)]
        m *= 2
    return ts


def _l2_normalise(xs, ones_bd, scale):
    out = []
    for p in range(0, len(xs), 2):
        sq = jnp.concatenate([xs[p] * xs[p], xs[p + 1] * xs[p + 1]], axis=1)
        ss = jnp.dot(sq, ones_bd, preferred_element_type=F32)
        inv = lax.rsqrt(ss + EPS) * scale
        out.append(xs[p] * inv[:, :DN_HEAD_DIM])
        out.append(xs[p + 1] * inv[:, DN_HEAD_DIM:])
    return out


def _dn_body(qkv_ref, sm_ref, z_ref, s0_ref, alog_ref, dtb_ref, nh_ref, ones_ref, o_ref, sfin_ref, s_scr):
    c_id = pl.program_id(1)
    n_rows = qkv_ref.shape[0]
    items = [(b, h) for b in range(n_rows) for h in range(DN_HEADS)]
    n_items = range(len(items))

    @pl.when(c_id == 0)
    def _():
        for b in range(n_rows):
            s_scr[b] = s0_ref[...]

    def head_cols(b, col):
        return qkv_ref[b, :, col:col + DN_HEAD_DIM].astype(F32)

    gc_all, gc_t, beta_all = [], [], []
    for b in range(n_rows):
        sm = sm_ref[b]
        g_b = -jnp.exp(alog_ref[...]) * _softplus(sm + dtb_ref[...])
        beta_all.append(_sigmoid(sm))
        gc_b = _cumsum_rows(g_b)
        gc_all.append(gc_b)
        gc_t.append(jnp.concatenate([gc_b, jnp.zeros_like(gc_b)], axis=0).T)

    r, c = _tri_incl(CHUNK)
    causal = r >= c
    strict = r > c
    scale = DN_HEAD_DIM ** -0.5

    q = [head_cols(b, h * DN_HEAD_DIM) for b, h in items]
    k = [head_cols(b, DN_QK + h * DN_HEAD_DIM) for b, h in items]
    v = [head_cols(b, 2 * DN_QK + h * DN_HEAD_DIM) for b, h in items]
    ones_bd = ones_ref[...]
    q = _l2_normalise(q, ones_bd, scale)
    k = _l2_normalise(k, ones_bd, 1.0)
    gcol = [gc_all[b][:, h:h + 1] for b, h in items]
    grow = [gc_t[b][h:h + 1, 0:CHUNK] for b, h in items]
    bcol = [beta_all[b][:, DN_HEADS + h:DN_HEADS + h + 1] for b, h in items]
    glast = [gc_all[b][CHUNK - 1:CHUNK, h:h + 1] for b, h in items]
    decay = [jnp.where(causal, jnp.exp(jnp.where(causal, gcol[i] - grow[i], 0.0)), 0.0) for i in n_items]
    eg = [jnp.exp(gcol[i]) for i in n_items]
    kk = [lax.dot_general(k[i], k[i], _NT, preferred_element_type=F32) for i in n_items]
    qk = [lax.dot_general(q[i], k[i], _NT, preferred_element_type=F32) for i in n_items]
    a_mat = [jnp.where(strict, bcol[i] * kk[i] * decay[i], 0.0) for i in n_items]
    qk = [jnp.where(causal, qk[i] * decay[i], 0.0) for i in n_items]
    t_inv = _unit_lower_inverse(a_mat)
    rhs = [jnp.concatenate([v[i] * bcol[i], k[i] * (bcol[i] * eg[i])], axis=1) for i in n_items]
    sol = [jnp.dot(t_inv[i], rhs[i], preferred_element_type=F32) for i in n_items]
    s_old = [s_scr[b, h] for b, h in items]
    qe = [q[i] * eg[i] for i in n_items]
    k_dec = [k[i] * jnp.exp(glast[i] - gcol[i]) for i in n_items]
    ws = [jnp.dot(sol[i][:, DN_HEAD_DIM:], s_old[i], preferred_element_type=F32) for i in n_items]
    o_inter = [jnp.dot(qe[i], s_old[i], preferred_element_type=F32) for i in n_items]
    v_new = [sol[i][:, :DN_HEAD_DIM] - ws[i] for i in n_items]
    o_intra = [jnp.dot(qk[i], v_new[i], preferred_element_type=F32) for i in n_items]
    s_add = [lax.dot_general(k_dec[i], v_new[i], _TN, preferred_element_type=F32) for i in n_items]
    for i, (b, h) in enumerate(items):
        col = h * DN_HEAD_DIM
        s_scr[b, h] = s_old[i] * jnp.exp(glast[i]) + s_add[i]
        zed = z_ref[b, :, col:col + DN_HEAD_DIM].astype(F32)
        o = o_inter[i] + o_intra[i]
        o_ref[b, :, col:col + DN_HEAD_DIM] = (_rms(o) * nh_ref[...] * _silu(zed)).astype(o_ref.dtype)

    @pl.when(c_id == pl.num_programs(1) - 1)
    def _():
        sfin_ref[...] = s_scr[...]


def _rows_per_step(b, want):
    while b % want:
        want //= 2
    return want


def _dn_chunk(qkv, small, z, s0, alog, dtb, nh):
    b, l, _ = qkv.shape
    lane_head = np.arange(2 * DN_HEAD_DIM) // DN_HEAD_DIM
    ones_bd = jnp.asarray(lane_head[:, None] == lane_head[None, :], F32)
    nc = l // CHUNK
    nr = _rows_per_step(b, 4)
    const2 = lambda bi, ci: (0, 0)
    blk = lambda w: pl.BlockSpec((nr, CHUNK, w), lambda bi, ci: (bi, ci, 0))
    return pl.pallas_call(
        _dn_body, grid=(b // nr, nc),
        in_specs=[
            blk(3 * DN_QK), blk(SMALL_W), blk(DN_V),
            pl.BlockSpec(s0.shape, lambda bi, ci: (0, 0, 0)),
            pl.BlockSpec(alog.shape, const2),
            pl.BlockSpec(dtb.shape, const2),
            pl.BlockSpec(nh.shape, const2),
            pl.BlockSpec(ones_bd.shape, const2),
        ],
        out_specs=[
            blk(DN_V),
            pl.BlockSpec((nr, DN_HEADS, DN_HEAD_DIM, DN_HEAD_DIM), lambda bi, ci: (bi, 0, 0, 0)),
        ],
        out_shape=[
            jax.ShapeDtypeStruct((b, l, DN_V), BF16),
            jax.ShapeDtypeStruct((b, DN_HEADS, DN_HEAD_DIM, DN_HEAD_DIM), F32),
        ],
        scratch_shapes=[pltpu.VMEM((nr, DN_HEADS, DN_HEAD_DIM, DN_HEAD_DIM), F32)],
        compiler_params=pltpu.CompilerParams(dimension_semantics=("parallel", "arbitrary"),
                                             vmem_limit_bytes=VMEM_LIMIT),
        name="dn_chunk",
    )(qkv, small, z, s0, alog, dtb, nh, ones_bd)


def _gla_body(qk_ref, v_ref, r_ref, sm_ref, wa_ref, ba_ref, nh_ref, s0_ref, o_ref, sfin_ref, s_scr):
    c_id = pl.program_id(1)
    n_rows = qk_ref.shape[0]
    items = [(b, h) for b in range(n_rows) for h in range(GLA_HEADS)]
    n_items = range(len(items))

    @pl.when(c_id == 0)
    def _():
        for b in range(n_rows):
            s_scr[b] = s0_ref[...]

    b_all = []
    for b in range(n_rows):
        la = jnp.dot(sm_ref[b], wa_ref[...], preferred_element_type=F32) + ba_ref[...]
        log_alpha = (jnp.minimum(la, 0.0) - jnp.log(1.0 + jnp.exp(-jnp.abs(la)))) * (1.0 / GLA_TAU)
        b_all.append(_cumsum_rows(log_alpha))

    r, c = _tri_incl(CHUNK)
    causal = r >= c
    scale = GLA_KEY_DIM ** -0.5
    mid = CHUNK // 2 - 1

    q = [qk_ref[b, :, h * GLA_KEY_DIM:(h + 1) * GLA_KEY_DIM].astype(F32) * scale for b, h in items]
    k = [qk_ref[b, :, GLA_QK + h * GLA_KEY_DIM:GLA_QK + (h + 1) * GLA_KEY_DIM].astype(F32) for b, h in items]
    v = [v_ref[b, :, h * GLA_VAL_DIM:(h + 1) * GLA_VAL_DIM].astype(F32) for b, h in items]
    bh = [b_all[b][:, h * GLA_KEY_DIM:(h + 1) * GLA_KEY_DIM] for b, h in items]
    bmid = [x[mid:mid + 1, :] for x in bh]
    blast = [x[CHUNK - 1:CHUNK, :] for x in bh]
    qs = [q[i] * jnp.exp(bh[i] - bmid[i]) for i in n_items]
    ks = [k[i] * jnp.exp(bmid[i] - bh[i]) for i in n_items]
    qd = [q[i] * jnp.exp(bh[i]) for i in n_items]
    kd = [k[i] * jnp.exp(blast[i] - bh[i]) for i in n_items]
    st = [s_scr[b, h] for b, h in items]
    att = [lax.dot_general(qs[i], ks[i], _NT, preferred_element_type=F32) for i in n_items]
    o_inter = [lax.dot_general(qd[i], st[i], _NT, preferred_element_type=F32) for i in n_items]
    s_add = [lax.dot_general(v[i], kd[i], _TN, preferred_element_type=F32) for i in n_items]
    att = [jnp.where(causal, a, 0.0) for a in att]
    o_intra = [jnp.dot(att[i], v[i], preferred_element_type=F32) for i in n_items]
    for i, (b, h) in enumerate(items):
        vc = h * GLA_VAL_DIM
        s_scr[b, h] = st[i] * jnp.exp(blast[i]) + s_add[i]
        gate = _silu(r_ref[b, :, vc:vc + GLA_VAL_DIM].astype(F32))
        o = o_inter[i] + o_intra[i]
        o_ref[b, :, vc:vc + GLA_VAL_DIM] = (_rms(o) * nh_ref[...] * gate).astype(o_ref.dtype)

    @pl.when(c_id == pl.num_programs(1) - 1)
    def _():
        sfin_ref[...] = s_scr[...]


def _gla_chunk(qk, v, rr, small, wa, ba, nh, s0):
    b, l, _ = qk.shape
    nc = l // CHUNK
    nr = _rows_per_step(b, 4)
    const2 = lambda bi, ci: (0, 0)
    blk = lambda w: pl.BlockSpec((nr, CHUNK, w), lambda bi, ci: (bi, ci, 0))
    return pl.pallas_call(
        _gla_body, grid=(b // nr, nc),
        in_specs=[
            blk(2 * GLA_QK), blk(GLA_V), blk(GLA_V), blk(SMALL_W),
            pl.BlockSpec(wa.shape, const2), pl.BlockSpec(ba.shape, const2), pl.BlockSpec(nh.shape, const2),
            pl.BlockSpec(s0.shape, lambda bi, ci: (0, 0, 0)),
        ],
        out_specs=[
            blk(GLA_V),
            pl.BlockSpec((nr, GLA_HEADS, GLA_VAL_DIM, GLA_KEY_DIM), lambda bi, ci: (bi, 0, 0, 0)),
        ],
        out_shape=[
            jax.ShapeDtypeStruct((b, l, GLA_V), BF16),
            jax.ShapeDtypeStruct((b, GLA_HEADS, GLA_VAL_DIM, GLA_KEY_DIM), F32),
        ],
        scratch_shapes=[pltpu.VMEM((nr, GLA_HEADS, GLA_VAL_DIM, GLA_KEY_DIM), F32)],
        compiler_params=pltpu.CompilerParams(dimension_semantics=("parallel", "arbitrary"),
                                             vmem_limit_bytes=VMEM_LIMIT),
        name="gla_chunk",
    )(qk, v, rr, small, wa, ba, nh, s0)


TOK_SUB = D_MODEL // LANES
assert TOK_SUB == SUBLANES


def _tiles_to_rows(ref, n):
    return jnp.concatenate([ref[pl.ds(s, n, stride=TOK_SUB), :] for s in range(TOK_SUB)], axis=1)


def _rows_to_tiles(ref, value):
    n = value.shape[0]
    for s in range(TOK_SUB):
        ref[pl.ds(s, n, stride=TOK_SUB), :] = value[:, s * LANES:(s + 1) * LANES]


def _tile_rows(tok, count=1):
    return pl.ds(pl.multiple_of(tok * TOK_SUB, TOK_SUB), count * TOK_SUB)


def _outproj_body(odn_ref, ogla_ref, gates_ref, x_ref, wd_ref, wg_ref, wo_ref, gn_ref, wr_ref, br_ref,
                  h2_ref, hn_ref, lg_ref):
    y_dn = jnp.dot(odn_ref[...], wd_ref[...], preferred_element_type=F32)
    y_gla = jnp.dot(ogla_ref[...], wg_ref[...], preferred_element_type=F32)
    gd = _sigmoid(gates_ref[:, 0:D_MODEL].astype(F32))
    gg = _sigmoid(gates_ref[:, D_MODEL:2 * D_MODEL].astype(F32))
    merged = gd * y_dn + gg * y_gla
    h2 = x_ref[...] + _bdot(merged, wo_ref[...])
    h2_ref[...] = h2
    hn = _rms(h2) * gn_ref[...]
    _rows_to_tiles(hn_ref, hn)
    wr = wr_ref[...]
    wr_hi, hn_hi = _bf16_part(wr), _bf16_part(hn)
    nt = lambda a, b: lax.dot_general(a, b, _NT, preferred_element_type=F32)
    lg_ref[...] = nt(wr_hi, hn_hi) + nt(wr_hi, hn - hn_hi) + nt(wr - wr_hi, hn_hi) + br_ref[...]


def _out_proj(o_dn, o_gla, gates, x2d, wd, wg, wo, gn, wr_t, br, rows):
    t = x2d.shape[0]
    row_blk = lambda w: pl.BlockSpec((rows, w), lambda i: (i, 0))
    const = lambda a: pl.BlockSpec(a.shape, lambda i: (0, 0))
    return pl.pallas_call(
        _outproj_body, grid=(t // rows,),
        in_specs=[row_blk(DN_V), row_blk(GLA_V), row_blk(2 * D_MODEL), row_blk(D_MODEL),
                  const(wd), const(wg), const(wo), const(gn), const(wr_t), const(br)],
        out_specs=[row_blk(D_MODEL), pl.BlockSpec((rows * TOK_SUB, LANES), lambda i: (i, 0)),
                   pl.BlockSpec((LANES, rows), lambda i: (0, i))],
        out_shape=[jax.ShapeDtypeStruct((t, D_MODEL), F32), jax.ShapeDtypeStruct((t * TOK_SUB, LANES), F32),
                   jax.ShapeDtypeStruct((LANES, t), F32)],
        compiler_params=pltpu.CompilerParams(dimension_semantics=("parallel",), vmem_limit_bytes=VMEM_LIMIT),
        name="out_proj",
    )(o_dn, o_gla, gates, x2d, wd, wg, wo, gn, wr_t, br)


ROUTE_SUB = 256


def _route_body(lg_ref, idx_ref, gate_ref, cnt_ref, carry):
    step = pl.program_id(0)

    @pl.when(step == 0)
    def _():
        carry[...] = jnp.zeros_like(carry)

    tt = lg_ref.shape[1]
    gl = lg_ref[0:N_GROUPS, :]
    gmax = jnp.max(gl, axis=0, keepdims=True)
    rid8 = lax.broadcasted_iota(jnp.int32, (N_GROUPS, tt), 0)
    gsel = jnp.min(jnp.where(gl == gmax, rid8, N_GROUPS), axis=0, keepdims=True)
    gw = 1.0 / jnp.sum(jnp.exp(gl - gmax), axis=0, keepdims=True)
    el = lg_ref[N_GROUPS:N_GROUPS + N_EXPERTS, :]
    rid = lax.broadcasted_iota(jnp.int32, (N_EXPERTS, tt), 0)
    neg = jnp.float32(-jnp.inf)
    ein = jnp.where((rid >> 3) == gsel, el, neg)
    t1 = jnp.max(ein, axis=0, keepdims=True)
    i1 = jnp.min(jnp.where(ein == t1, rid, N_EXPERTS), axis=0, keepdims=True)
    ein2 = jnp.where(rid == i1, neg, ein)
    t2 = jnp.max(ein2, axis=0, keepdims=True)
    i2 = jnp.min(jnp.where(ein2 == t2, rid, N_EXPERTS), axis=0, keepdims=True)
    e21 = jnp.exp(t2 - t1)
    den = 1.0 / (1.0 + e21)
    sel1 = rid == i1
    sel2 = rid == i2
    onehot = jnp.where(sel1 | sel2, 1.0, 0.0)

    ur, uc = _tri_incl(ROUTE_SUB)
    upper = (ur <= uc).astype(BF16)
    run = carry[...]
    r1_parts, r2_parts = [], []
    for s in range(tt // ROUTE_SUB):
        sl = slice(s * ROUTE_SUB, (s + 1) * ROUTE_SUB)
        oh = onehot[:, sl]
        incl = jnp.dot(oh.astype(BF16), upper, preferred_element_type=F32) + run
        excl = incl - oh
        r1_parts.append(jnp.sum(jnp.where(sel1[:, sl], excl, 0.0), axis=0, keepdims=True))
        r2_parts.append(jnp.sum(jnp.where(sel2[:, sl], excl, 0.0), axis=0, keepdims=True))
        run = jnp.broadcast_to(incl[:, ROUTE_SUB - 1:ROUTE_SUB], run.shape)
    carry[...] = run
    r1 = jnp.concatenate(r1_parts, axis=1) if len(r1_parts) > 1 else r1_parts[0]
    r2 = jnp.concatenate(r2_parts, axis=1) if len(r2_parts) > 1 else r2_parts[0]

    idx_ref[...] = jnp.zeros_like(idx_ref)
    idx_ref[0:1, :] = i1
    idx_ref[1:2, :] = i2
    idx_ref[2:3, :] = r1.astype(jnp.int32)
    idx_ref[3:4, :] = r2.astype(jnp.int32)
    gate_ref[...] = jnp.zeros_like(gate_ref)
    gate_ref[0:1, :] = den * gw
    gate_ref[1:2, :] = e21 * den * gw
    cnt_ref[...] = run[:, 0:LANES]


def _route(logits_t, lanes):
    t = logits_t.shape[1]
    return pl.pallas_call(
        _route_body, grid=(t // lanes,),
        in_specs=[pl.BlockSpec((LANES, lanes), lambda i: (0, i))],
        out_specs=[pl.BlockSpec((8, lanes), lambda i: (0, i)), pl.BlockSpec((8, lanes), lambda i: (0, i)),
                   pl.BlockSpec((N_EXPERTS, LANES), lambda i: (0, 0))],
        out_shape=[jax.ShapeDtypeStruct((8, t), jnp.int32), jax.ShapeDtypeStruct((8, t), F32),
                   jax.ShapeDtypeStruct((N_EXPERTS, LANES), F32)],
        scratch_shapes=[pltpu.VMEM((N_EXPERTS, ROUTE_SUB), F32)],
        compiler_params=pltpu.CompilerParams(dimension_semantics=("arbitrary",)),
        name="route",
    )(logits_t)


def _row_copy(src_ref, src_tok, dst_ref, dst_tok, sem):
    return pltpu.make_async_copy(src_ref.at[_tile_rows(src_tok)], dst_ref.at[_tile_rows(dst_tok)], sem)


ROW_UNROLL = 8


def _slots_body(pstart_ref, idx_ref, slot_ref):
    idx = idx_ref[...]
    base = jnp.zeros_like(idx)
    for e in range(N_EXPERTS):
        base = jnp.where(idx == e, pstart_ref[e], base)
    slot_ref[...] = jnp.zeros_like(idx)
    slot_ref[0:2, :] = base[0:2, :] + idx[2:4, :]


def _slots(pstart, idx, lanes):
    t = idx.shape[1]
    grid_spec = pltpu.PrefetchScalarGridSpec(
        num_scalar_prefetch=1, grid=(t // lanes,),
        in_specs=[pl.BlockSpec((8, lanes), lambda i, ps: (0, i))],
        out_specs=pl.BlockSpec((8, lanes), lambda i, ps: (0, i)),
    )
    return pl.pallas_call(
        _slots_body, grid_spec=grid_spec, out_shape=jax.ShapeDtypeStruct(idx.shape, jnp.int32),
        compiler_params=pltpu.CompilerParams(dimension_semantics=("parallel",)),
        name="slots",
    )(pstart, idx)


DISPATCH_BUFS = 3


def _dispatch_body(pstart_ref, size_ref, slot_ref, hn_hbm, xs_ref, zblk, hbuf, sem, load_sems, row_sems):
    step = pl.program_id(0)
    n_steps = pl.num_programs(0)
    n_tok = hbuf.shape[1] // TOK_SUB

    def load(tile):
        buf = tile % DISPATCH_BUFS
        return pltpu.make_async_copy(hn_hbm.at[_tile_rows(tile * n_tok, n_tok)], hbuf.at[buf], load_sems.at[buf])

    def wait_rows(parity):
        for k in range(2):
            pltpu.make_async_copy(hbuf.at[0], xs_ref.at[_tile_rows(0, n_tok)], row_sems.at[parity]).wait()

    @pl.when(step == 0)
    def _():
        load(0).start()

        @pl.when(n_steps > 1)
        def _():
            load(1).start()

        zblk[...] = jnp.zeros_like(zblk)
        parts = MOE_BLOCK // ZERO_ROWS

        def clear_copy(row0, part):
            return pltpu.make_async_copy(zblk, xs_ref.at[_tile_rows(row0 + part * ZERO_ROWS, ZERO_ROWS)], sem)

        def per_expert(e, n_rows):
            size = size_ref[e]
            full = size // MOE_BLOCK * MOE_BLOCK

            @pl.when(size != full)
            def _():
                for part in range(parts):
                    clear_copy(pstart_ref[e] + full, part).start()
                for part in range(parts):
                    clear_copy(pstart_ref[e] + full, part).wait()

            return n_rows + (size + MOE_BLOCK - 1) // MOE_BLOCK * MOE_BLOCK

        n_rows = lax.fori_loop(0, N_EXPERTS, per_expert, 0)

        def clear_unused(j, c):
            for part in range(parts):
                clear_copy(j * MOE_BLOCK, part).start()
            for part in range(parts):
                clear_copy(j * MOE_BLOCK, part).wait()
            return c

        lax.fori_loop(n_rows // MOE_BLOCK, xs_ref.shape[0] // (MOE_BLOCK * TOK_SUB), clear_unused, 0)

    load(step).wait()
    src = hbuf.at[step % DISPATCH_BUFS]
    row_sem = row_sems.at[step % 2]

    def issue(t, carry):
        for k in range(2):
            _row_copy(src, t, xs_ref, slot_ref[k, t], row_sem).start()
        return carry

    lax.fori_loop(0, n_tok, issue, 0, unroll=ROW_UNROLL)

    @pl.when(step > 0)
    def _():
        wait_rows((step - 1) % 2)

    @pl.when(step + 2 < n_steps)
    def _():
        load(step + 2).start()

    @pl.when(step == n_steps - 1)
    def _():
        wait_rows(step % 2)


def _dispatch(pstart, sizes, slots, hn, n_rows_out, tokens):
    t = hn.shape[0] // TOK_SUB
    grid_spec = pltpu.PrefetchScalarGridSpec(
        num_scalar_prefetch=2, grid=(t // tokens,),
        in_specs=[pl.BlockSpec((8, tokens), lambda i, ps, sz: (0, i), memory_space=pltpu.SMEM),
                  pl.BlockSpec(memory_space=pl.ANY)],
        out_specs=pl.BlockSpec(memory_space=pl.ANY),
        scratch_shapes=[pltpu.VMEM((ZERO_ROWS * TOK_SUB, LANES), F32),
                        pltpu.VMEM((DISPATCH_BUFS, tokens * TOK_SUB, LANES), F32),
                        pltpu.SemaphoreType.DMA(()), pltpu.SemaphoreType.DMA((DISPATCH_BUFS,)),
                        pltpu.SemaphoreType.DMA((2,))],
    )
    return pl.pallas_call(
        _dispatch_body, grid_spec=grid_spec,
        out_shape=jax.ShapeDtypeStruct((n_rows_out * TOK_SUB, LANES), F32),
        compiler_params=pltpu.CompilerParams(dimension_semantics=("arbitrary",)),
        name="dispatch",
    )(pstart, sizes, slots, hn)


def _experts_body(be_ref, nused_ref, xs_ref, w1_ref, w3_ref, w2_ref, yb_ref, w1b, w3b, w2b):
    j = pl.program_id(0)
    used = j < nused_ref[0]
    new_expert = jnp.logical_or(j == 0, be_ref[j] != be_ref[jnp.maximum(j - 1, 0)])

    @pl.when(jnp.logical_and(used, new_expert))
    def _():
        w1b[...] = w1_ref[...].astype(BF16)
        w3b[...] = w3_ref[...].astype(BF16)
        w2b[...] = w2_ref[...].astype(BF16)

    @pl.when(used)
    def _():
        xb = _tiles_to_rows(xs_ref, MOE_BLOCK).astype(BF16)
        h1 = jnp.dot(xb, w1b[...], preferred_element_type=F32)
        h3 = jnp.dot(xb, w3b[...], preferred_element_type=F32)
        hid = (_silu(h1) * h3).astype(BF16)
        _rows_to_tiles(yb_ref, jnp.dot(hid, w2b[...], preferred_element_type=F32))

    @pl.when(jnp.logical_not(used))
    def _():
        yb_ref[...] = jnp.zeros_like(yb_ref)


def _experts(block_expert, n_used, xs, w1, w3, w2):
    p = xs.shape[0] // TOK_SUB
    nb = p // MOE_BLOCK
    row_map = lambda j, be, nu: (jnp.minimum(j, nu[0] - 1), 0)
    w_map = lambda j, be, nu: (be[j], 0, 0)
    grid_spec = pltpu.PrefetchScalarGridSpec(
        num_scalar_prefetch=2, grid=(nb,),
        in_specs=[pl.BlockSpec((MOE_BLOCK * TOK_SUB, LANES), row_map),
                  pl.BlockSpec((None, D_MODEL, D_EXPERT), w_map),
                  pl.BlockSpec((None, D_MODEL, D_EXPERT), w_map),
                  pl.BlockSpec((None, D_EXPERT, D_MODEL), w_map)],
        out_specs=pl.BlockSpec((MOE_BLOCK * TOK_SUB, LANES), lambda j, be, nu: (j, 0)),
        scratch_shapes=[pltpu.VMEM((D_MODEL, D_EXPERT), BF16), pltpu.VMEM((D_MODEL, D_EXPERT), BF16),
                        pltpu.VMEM((D_EXPERT, D_MODEL), BF16)],
    )
    return pl.pallas_call(
        _experts_body, grid_spec=grid_spec,
        out_shape=jax.ShapeDtypeStruct((p * TOK_SUB, LANES), F32),
        compiler_params=pltpu.CompilerParams(dimension_semantics=("arbitrary",), vmem_limit_bytes=VMEM_LIMIT),
        name="experts",
    )(block_expert, n_used, xs, w1, w3, w2)


def _combine_body(slot_ref, slot_next_ref, gate_ref, h2_ref, gf_ref, yb_ref, o_ref, ybuf, sems):
    n_tok = h2_ref.shape[0]
    step = pl.program_id(0)
    cur = step % 2

    def gather(slots, buf_id):
        def issue(t, carry):
            for k in range(2):
                pltpu.make_async_copy(yb_ref.at[_tile_rows(slots[k, t])], ybuf.at[buf_id, k, _tile_rows(t)],
                                      sems.at[buf_id]).start()
            return carry

        lax.fori_loop(0, n_tok, issue, 0, unroll=ROW_UNROLL)

    @pl.when(step == 0)
    def _():
        gather(slot_ref, 0)

    @pl.when(step + 1 < pl.num_programs(0))
    def _():
        gather(slot_next_ref, 1 - cur)

    for k in range(2):
        pltpu.make_async_copy(yb_ref.at[_tile_rows(0, n_tok)], ybuf.at[cur, k], sems.at[cur]).wait()
    y0 = _tiles_to_rows(ybuf.at[cur, 0], n_tok)
    y1 = _tiles_to_rows(ybuf.at[cur, 1], n_tok)
    y = h2_ref[...] + gate_ref[:, 0:1] * y0 + gate_ref[:, 1:2] * y1
    o_ref[...] = _rms(y) * gf_ref[...]


def _combine(slots, gate_t, h2, gf, yb, tokens):
    t = h2.shape[0]
    last = t // tokens - 1
    return pl.pallas_call(
        _combine_body, grid=(t // tokens,),
        in_specs=[pl.BlockSpec((8, tokens), lambda i: (0, i), memory_space=pltpu.SMEM),
                  pl.BlockSpec((8, tokens), lambda i: (0, jnp.minimum(i + 1, last)), memory_space=pltpu.SMEM),
                  pl.BlockSpec((tokens, 8), lambda i: (i, 0)),
                  pl.BlockSpec((tokens, D_MODEL), lambda i: (i, 0)),
                  pl.BlockSpec((1, D_MODEL), lambda i: (0, 0)),
                  pl.BlockSpec(memory_space=pl.ANY)],
        out_specs=pl.BlockSpec((tokens, D_MODEL), lambda i: (i, 0)),
        scratch_shapes=[pltpu.VMEM((2, 2, tokens * TOK_SUB, LANES), F32), pltpu.SemaphoreType.DMA((2,))],
        out_shape=jax.ShapeDtypeStruct((t, D_MODEL), F32),
        compiler_params=pltpu.CompilerParams(dimension_semantics=("arbitrary",)),
        name="combine",
    )(slots, slots, gate_t, h2, gf, yb)


def _pad_lanes(v, width=LANES):
    v = v.reshape(1, -1).astype(F32)
    return jnp.pad(v, ((0, 0), (0, width - v.shape[1])))


def _largest_tile(n, cap):
    t = cap
    while n % t:
        t //= 2
    return t


def kernel(x, meta_tokens, norm_mix, w_in, conv_dn, a_log, dt_bias, norm_head_dn, w_proj_dn, w_alpha, b_alpha,
           norm_head_gla, w_proj_gla, w_out, norm_ffn, w_group, b_group, w_router, b_router, w1, w3, w2, norm_final):
    assert norm_mix.shape[0] == 1, "single-layer block"
    bsz, seq, d = x.shape
    assert d == D_MODEL and seq % CHUNK == 0
    t = bsz * seq

    wi = w_in[0]
    offs = np.cumsum([0, DN_QK, DN_QK, DN_V, DN_V, DN_HEADS, DN_HEADS, GLA_QK, GLA_QK, GLA_V, GLA_V, GLA_RANK,
                      D_MODEL, D_MODEL]).tolist()
    seg = lambda i, j: wi[:, offs[i]:offs[j]]
    n_small = 2 * DN_HEADS + GLA_RANK
    w_all = jnp.concatenate([seg(0, 4), seg(6, 10), seg(11, 13), seg(4, 6), seg(10, 11),
                             jnp.zeros((D_MODEL, SMALL_W - n_small), F32)], axis=1).astype(BF16)
    g_mix = norm_mix[0].reshape(1, D_MODEL)

    conv_w = jnp.pad(conv_dn[0], ((0, 8 - CONV_WIDTH), (0, 0)))
    alog = _pad_lanes(a_log[0])
    dtb = _pad_lanes(dt_bias[0])
    nh_dn = norm_head_dn[0].reshape(1, DN_HEAD_DIM)
    lr_off = 2 * DN_HEADS
    wa = jnp.zeros((SMALL_W, GLA_QK), F32).at[lr_off:lr_off + GLA_RANK].set(w_alpha[0])
    ba = b_alpha[0].reshape(1, GLA_QK)
    nh_gla = norm_head_gla[0].reshape(1, GLA_VAL_DIM)

    def mixers(tokens2d, nb, hist, s0_dn, s0_gla, rows):
        tiles_per_seq = tokens2d.shape[0] // nb // rows
        qkv, z, qk_g, v_g, r_g, gates, small, tail = _in_proj(tokens2d, g_mix, w_all, conv_w, hist, rows, tiles_per_seq)
        r3 = lambda a: a.reshape(nb, -1, a.shape[-1])
        o_dn, s_dn = _dn_chunk(r3(qkv), r3(small), r3(z), s0_dn, alog, dtb, nh_dn)
        o_gla, s_gla = _gla_chunk(r3(qk_g), r3(v_g), r3(r_g), r3(small), wa, ba, nh_gla, s0_gla)
        return tail, o_dn, o_gla, gates, s_dn, s_gla

    meta_rows = jnp.pad(meta_tokens.astype(F32), ((CHUNK - N_META, 0), (0, 0)))
    zero_hist = jnp.zeros((HIST_ROWS, 3 * DN_QK), F32)
    zero_dn = jnp.zeros((DN_HEADS, DN_HEAD_DIM, DN_HEAD_DIM), F32)
    zero_gla = jnp.zeros((GLA_HEADS, GLA_VAL_DIM, GLA_KEY_DIM), F32)
    hist, _, _, _, s_dn_m, s_gla_m = mixers(meta_rows, 1, zero_hist, zero_dn, zero_gla, CHUNK)

    x2d = x.reshape(t, d)
    rows = _largest_tile(seq, 512)
    _, o_dn, o_gla, gates, _, _ = mixers(x2d, bsz, hist, s_dn_m[0], s_gla_m[0], rows)

    wr_t = jnp.concatenate([w_group[0], w_router[0]], axis=1).T
    wr_t = jnp.pad(wr_t, ((0, LANES - wr_t.shape[0]), (0, 0)))
    br = jnp.pad(jnp.concatenate([b_group[0], b_router[0]]), (0, LANES - N_GROUPS - N_EXPERTS)).reshape(LANES, 1)
    h2, hn, logits_t = _out_proj(
        o_dn.reshape(t, DN_V), o_gla.reshape(t, GLA_V), gates, x2d,
        w_proj_dn[0].astype(BF16), w_proj_gla[0].astype(BF16), w_out[0].astype(BF16),
        norm_ffn[0].reshape(1, D_MODEL), wr_t, br, rows)

    idx, gate, cnt = _route(logits_t, _largest_tile(t, 512))

    sizes = cnt[:, 0].astype(jnp.int32)
    padded = (sizes + MOE_BLOCK - 1) // MOE_BLOCK * MOE_BLOCK
    pends = jnp.cumsum(padded)
    pstart = (pends - padded).astype(jnp.int32)
    n_blocks = (2 * t) // MOE_BLOCK + N_EXPERTS
    n_used = (pends[-1:] // MOE_BLOCK).astype(jnp.int32)
    block_row0 = jnp.arange(n_blocks, dtype=jnp.int32) * MOE_BLOCK
    block_expert = jnp.minimum(jnp.sum(pends[None, :] <= block_row0[:, None], axis=1), N_EXPERTS - 1).astype(jnp.int32)

    tok_tile = _largest_tile(t, 256)
    slots = _slots(pstart, idx, _largest_tile(t, 2048))
    xs = _dispatch(pstart, sizes, slots, hn, n_blocks * MOE_BLOCK, tok_tile)
    yb = _experts(block_expert, n_used, xs, w1[0], w3[0], w2[0])
    out = _combine(slots, gate.T, h2, norm_final.reshape(1, D_MODEL), yb, tok_tile)
    return out.reshape(bsz, seq, d)
```

```python
import functools

import jax
import jax.numpy as jnp
import numpy as np
from jax import lax
from jax.experimental import pallas as pl
from jax.experimental.pallas import tpu as pltpu

F32 = jnp.float32
BF16 = jnp.bfloat16

D_MODEL = 1024
CHUNK = 64
N_META = 16
EPS = 1e-6
DN_HEADS = 8
DN_HEAD_DIM = 128
DN_QK = DN_HEADS * DN_HEAD_DIM
DN_V = DN_HEADS * DN_HEAD_DIM
CONV_WIDTH = 4
GLA_HEADS = 4
GLA_KEY_DIM = 128
GLA_VAL_DIM = 256
GLA_QK = GLA_HEADS * GLA_KEY_DIM
GLA_V = GLA_HEADS * GLA_VAL_DIM
GLA_RANK = 16
GLA_TAU = 16.0
N_GROUPS = 8
EXPERTS_PER_GROUP = 8
N_EXPERTS = N_GROUPS * EXPERTS_PER_GROUP
D_EXPERT = 512
MOE_BLOCK = 512
ZERO_ROWS = 256

LANES = 128
SUBLANES = 8
SMALL_W = LANES
HIST_ROWS = 8
VMEM_LIMIT = 56 * 1024 * 1024

_NT = (((1,), (1,)), ((), ()))
_TN = (((0,), (0,)), ((), ()))


def _bdot(a, b):
    return jnp.dot(a.astype(BF16), b.astype(BF16), preferred_element_type=F32)


def _bdot_nt(a, b):
    return lax.dot_general(a.astype(BF16), b.astype(BF16), _NT, preferred_element_type=F32)


def _bdot_tn(a, b):
    return lax.dot_general(a.astype(BF16), b.astype(BF16), _TN, preferred_element_type=F32)


def _sigmoid(x):
    return 1.0 / (1.0 + jnp.exp(-x))


def _silu(x):
    half = 0.5 * x
    return half + half * jnp.tanh(half)


def _softplus(x):
    return jnp.maximum(x, 0.0) + jnp.log(1.0 + jnp.exp(-jnp.abs(x)))


def _rms(x, eps=EPS):
    return x * lax.rsqrt(jnp.mean(x * x, axis=-1, keepdims=True) + eps)


def _tri_incl(n):
    r = lax.broadcasted_iota(jnp.int32, (n, n), 0)
    c = lax.broadcasted_iota(jnp.int32, (n, n), 1)
    return r, c


def _bf16_part(x):
    bits = pltpu.bitcast(x, jnp.uint32) & jnp.uint32(0xFFFF0000)
    return pltpu.bitcast(bits, F32)


def _cumsum_rows(x):
    r, c = _tri_incl(x.shape[0])
    tri = (r >= c).astype(F32)
    hi = _bf16_part(x)
    r1 = x - hi
    mid = _bf16_part(r1)
    lo = r1 - mid
    return (jnp.dot(tri, hi, preferred_element_type=F32) + jnp.dot(tri, mid, preferred_element_type=F32)
            + jnp.dot(tri, lo, preferred_element_type=F32))


IN_GROUPS = ((3 * DN_QK, BF16), (DN_V, BF16), (2 * GLA_QK, BF16), (GLA_V, BF16), (GLA_V, BF16),
             (2 * D_MODEL, BF16), (SMALL_W, F32))


CONV_COLS = 256
PLAIN_COLS = 512


def _inproj_body(tiles_per_seq, x_ref, g_ref, w_ref, cw_ref, hist_ref, *refs):
    o_refs, tail_ref, pre, carry = refs[:-3], refs[-3], refs[-2], refs[-1]
    rows = x_ref.shape[0]

    @pl.when(pl.program_id(0) % tiles_per_seq == 0)
    def _():
        carry[...] = hist_ref[...]

    x = x_ref[...]
    xb = (_rms(x) * g_ref[...]).astype(BF16)

    qkv_ref = o_refs[0]

    def conv_pass(c0):
        cols = slice(c0, c0 + CONV_COLS)
        pre[0:HIST_ROWS, :] = carry[:, cols]
        pre[HIST_ROWS:HIST_ROWS + rows, :] = jnp.dot(xb, w_ref[:, cols], preferred_element_type=F32)
        carry[:, cols] = pre[rows:rows + HIST_ROWS, :]
        acc = None
        for j in range(CONV_WIDTH):
            lo = HIST_ROWS - (CONV_WIDTH - 1) + j
            term = pre[lo:lo + rows, :] * cw_ref[j:j + 1, cols]
            acc = term if acc is None else acc + term
        qkv_ref[:, cols] = _silu(acc).astype(qkv_ref.dtype)

    def plain_pass(o_ref, c0, width, wcol):
        o_ref[:, c0:c0 + width] = jnp.dot(xb, w_ref[:, wcol + c0:wcol + c0 + width],
                                          preferred_element_type=F32).astype(o_ref.dtype)

    plain_tasks = []
    wcol = qkv_ref.shape[1]
    for o_ref in o_refs[1:]:
        width = o_ref.shape[1]
        step = min(width, PLAIN_COLS)
        plain_tasks += [functools.partial(plain_pass, o_ref, c0, step, wcol) for c0 in range(0, width, step)]
        wcol += width

    conv_tasks = [functools.partial(conv_pass, c0) for c0 in range(0, qkv_ref.shape[1], CONV_COLS)]
    for i in range(max(len(conv_tasks), len(plain_tasks))):
        if i < len(plain_tasks):
            plain_tasks[i]()
        if i < len(conv_tasks):
            conv_tasks[i]()
    tail_ref[...] = carry[...]


def _in_proj(x2d, gain, w_all, conv_w, hist, rows, tiles_per_seq):
    t = x2d.shape[0]
    const = lambda a: pl.BlockSpec(a.shape, lambda i: (0, 0))
    in_specs = [pl.BlockSpec((rows, D_MODEL), lambda i: (i, 0)), const(gain),
                pl.BlockSpec(w_all.shape, lambda i: (0, 0), pipeline_mode=pl.Buffered(1)),
                const(conv_w), const(hist)]
    out_specs = [pl.BlockSpec((rows, w), lambda i: (i, 0)) for w, _ in IN_GROUPS] + [const(hist)]
    out_shape = [jax.ShapeDtypeStruct((t, w), dt) for w, dt in IN_GROUPS] + [jax.ShapeDtypeStruct(hist.shape, F32)]
    return pl.pallas_call(
        functools.partial(_inproj_body, tiles_per_seq), grid=(t // rows,),
        in_specs=in_specs, out_specs=out_specs, out_shape=out_shape,
        scratch_shapes=[pltpu.VMEM((HIST_ROWS + rows, CONV_COLS), F32), pltpu.VMEM(hist.shape, F32)],
        compiler_params=pltpu.CompilerParams(dimension_semantics=("arbitrary",), vmem_limit_bytes=VMEM_LIMIT),
        name="in_proj",
    )(x2d, gain, w_all, conv_w, hist)


def _block_diag2(m2):
    n = m2.shape[0]
    lane = lax.broadcasted_iota(jnp.int32, m2.shape, 1)
    return jnp.concatenate([jnp.where(lane < n, m2, 0.0), jnp.where(lane >= n, m2, 0.0)], axis=0)


def _unit_lower_inverse(mats):
    n = mats[0].shape[0]
    r = lax.broadcasted_iota(jnp.int32, (n, 2 * n), 0)
    c = lax.broadcasted_iota(jnp.int32, (n, 2 * n), 1) & (n - 1)
    eye = (r == c).astype(F32)
    pair = (r >> 1) == (c >> 1)
    ts = [eye - jnp.where(pair, a, 0.0) for a in mats]
    m = 2
    while m < n:
        sh = m.bit_length() - 1
        keep = ((r >> (sh + 1)) == (c >> (sh + 1))) & ((r >> sh) != (c >> sh))
        ams = [_block_diag2(jnp.where(keep, a, 0.0)) for a in mats]
        xs = [jnp.dot(t, am, preferred_element_type=F32) for t, am in zip(ts, ams)]
        ys = [jnp.dot(x, _block_diag2(t), preferred_element_type=F32) for x, t in zip(xs, ts)]
        ts = [t - y for t, y in zip(ts, ys)]
        m *= 2
    return ts


def _l2_normalise(xs, ones_bd, scale):
    out = []
    for p in range(0, len(xs), 2):
        sq = jnp.concatenate([xs[p] * xs[p], xs[p + 1] * xs[p + 1]], axis=1)
        ss = jnp.dot(sq, ones_bd, preferred_element_type=F32)
        inv = lax.rsqrt(ss + EPS) * scale
        out.append(xs[p] * inv[:, :DN_HEAD_DIM])
        out.append(xs[p + 1] * inv[:, DN_HEAD_DIM:])
    return out


DN_WAVE = 32


def _dn_body(qkv_ref, sm_ref, z_ref, s0_ref, alog_ref, dtb_ref, nh_ref, ones_ref, o_ref, sfin_ref, s_scr):
    c_id = pl.program_id(1)
    n_rows = qkv_ref.shape[0]
    all_items = [(b, h) for b in range(n_rows) for h in range(DN_HEADS)]

    @pl.when(c_id == 0)
    def _():
        for b in range(n_rows):
            s_scr[b] = s0_ref[...]

    def head_cols(b, col):
        return qkv_ref[b, :, col:col + DN_HEAD_DIM].astype(F32)

    gc_all, gc_t, beta_all = [], [], []
    for b in range(n_rows):
        sm = sm_ref[b]
        g_b = -jnp.exp(alog_ref[...]) * _softplus(sm + dtb_ref[...])
        beta_all.append(_sigmoid(sm))
        gc_b = _cumsum_rows(g_b)
        gc_all.append(gc_b)
        gc_t.append(jnp.concatenate([gc_b, jnp.zeros_like(gc_b)], axis=0).T)

    r2 = lax.broadcasted_iota(jnp.int32, (CHUNK, 2 * CHUNK), 0)
    lane2 = lax.broadcasted_iota(jnp.int32, (CHUNK, 2 * CHUNK), 1)
    c2 = lane2 & (CHUNK - 1)
    first = lane2 < CHUNK
    causal2 = r2 >= c2
    strict2 = r2 > c2
    scale = DN_HEAD_DIM ** -0.5
    ones_bd = ones_ref[...]

    def wave(items):
        n_items = range(len(items))
        q = [head_cols(b, h * DN_HEAD_DIM) for b, h in items]
        k = [head_cols(b, DN_QK + h * DN_HEAD_DIM) for b, h in items]
        v = [head_cols(b, 2 * DN_QK + h * DN_HEAD_DIM) for b, h in items]
        q = _l2_normalise(q, ones_bd, scale)
        k = _l2_normalise(k, ones_bd, 1.0)
        gcol = [gc_all[b][:, h:h + 1] for b, h in items]
        bcol = [beta_all[b][:, DN_HEADS + h:DN_HEADS + h + 1] for b, h in items]
        glast = [gc_all[b][CHUNK - 1:CHUNK, h:h + 1] for b, h in items]
        eg = [jnp.exp(gcol[i]) for i in n_items]

        pairs = range(0, len(items), 2)
        n_pairs = range(len(pairs))

        def diag2(x0, x1):
            return jnp.concatenate([jnp.concatenate([x0, jnp.zeros_like(x1)], axis=1),
                                    jnp.concatenate([jnp.zeros_like(x0), x1], axis=1)], axis=0)

        def lanes2(x0, x1):
            return jnp.where(first, x0, x1)

        def head_row(i):
            b, h = items[i]
            return gc_t[b][h:h + 1, :]

        k_bd = [diag2(k[p], k[p + 1]) for p in pairs]
        k_cat = [jnp.concatenate([k[p], k[p + 1]], axis=1) for p in pairs]
        q_cat = [jnp.concatenate([q[p], q[p + 1]], axis=1) for p in pairs]
        kk2 = [lax.dot_general(k_cat[j], k_bd[j], _NT, preferred_element_type=F32) for j in n_pairs]
        qk2 = [lax.dot_general(q_cat[j], k_bd[j], _NT, preferred_element_type=F32) for j in n_pairs]
        grow2 = [head_row(p) + pltpu.roll(head_row(p + 1), CHUNK, axis=1) for p in pairs]
        gcol2 = [lanes2(gcol[p], gcol[p + 1]) for p in pairs]
        bcol2 = [lanes2(bcol[p], bcol[p + 1]) for p in pairs]
        decay2 = [jnp.where(causal2, jnp.exp(jnp.where(causal2, gcol2[j] - grow2[j], 0.0)), 0.0) for j in n_pairs]
        a_mat2 = [jnp.where(strict2, bcol2[j] * kk2[j] * decay2[j], 0.0) for j in n_pairs]
        qk2 = [jnp.where(causal2, qk2[j] * decay2[j], 0.0) for j in n_pairs]
        t_inv2 = _unit_lower_inverse(a_mat2)
        rhs = [jnp.concatenate([v[i] * bcol[i], k[i] * (bcol[i] * eg[i])], axis=1) for i in n_items]
        sol2 = [jnp.dot(t_inv2[j], diag2(rhs[p], rhs[p + 1]), preferred_element_type=F32)
                for j, p in enumerate(pairs)]
        sol = [sol2[i // 2][:, (i % 2) * 2 * DN_HEAD_DIM:(i % 2 + 1) * 2 * DN_HEAD_DIM] for i in n_items]
        s_old = [s_scr[b, h] for b, h in items]
        qe = [q[i] * eg[i] for i in n_items]
        k_dec = [k[i] * jnp.exp(glast[i] - gcol[i]) for i in n_items]
        ws = [jnp.dot(sol[i][:, DN_HEAD_DIM:], s_old[i], preferred_element_type=F32) for i in n_items]
        o_inter = [jnp.dot(qe[i], s_old[i], preferred_element_type=F32) for i in n_items]
        v_new = [sol[i][:, :DN_HEAD_DIM] - ws[i] for i in n_items]
        o_intra2 = [jnp.dot(qk2[j], diag2(v_new[p], v_new[p + 1]), preferred_element_type=F32)
                    for j, p in enumerate(pairs)]
        o_intra = [o_intra2[i // 2][:, (i % 2) * DN_HEAD_DIM:(i % 2 + 1) * DN_HEAD_DIM] for i in n_items]
        s_add = [lax.dot_general(k_dec[i], v_new[i], _TN, preferred_element_type=F32) for i in n_items]
        for i, (b, h) in enumerate(items):
            col = h * DN_HEAD_DIM
            s_scr[b, h] = s_old[i] * jnp.exp(glast[i]) + s_add[i]
            zed = z_ref[b, :, col:col + DN_HEAD_DIM].astype(F32)
            o = o_inter[i] + o_intra[i]
            o_ref[b, :, col:col + DN_HEAD_DIM] = (_rms(o) * nh_ref[...] * _silu(zed)).astype(o_ref.dtype)

    for w0 in range(0, len(all_items), DN_WAVE):
        wave(all_items[w0:w0 + DN_WAVE])

    @pl.when(c_id == pl.num_programs(1) - 1)
    def _():
        sfin_ref[...] = s_scr[...]


def _rows_per_step(b, want):
    while b % want:
        want //= 2
    return want


def _dn_chunk(qkv, small, z, s0, alog, dtb, nh):
    b, l, _ = qkv.shape
    lane_head = np.arange(2 * DN_HEAD_DIM) // DN_HEAD_DIM
    ones_bd = jnp.asarray(lane_head[:, None] == lane_head[None, :], F32)
    nc = l // CHUNK
    nr = _rows_per_step(b, 4)
    const2 = lambda bi, ci: (0, 0)
    blk = lambda w: pl.BlockSpec((nr, CHUNK, w), lambda bi, ci: (bi, ci, 0))
    return pl.pallas_call(
        _dn_body, grid=(b // nr, nc),
        in_specs=[
            blk(3 * DN_QK), blk(SMALL_W), blk(DN_V),
            pl.BlockSpec(s0.shape, lambda bi, ci: (0, 0, 0)),
            pl.BlockSpec(alog.shape, const2),
            pl.BlockSpec(dtb.shape, const2),
            pl.BlockSpec(nh.shape, const2),
            pl.BlockSpec(ones_bd.shape, const2),
        ],
        out_specs=[
            blk(DN_V),
            pl.BlockSpec((nr, DN_HEADS, DN_HEAD_DIM, DN_HEAD_DIM), lambda bi, ci: (bi, 0, 0, 0)),
        ],
        out_shape=[
            jax.ShapeDtypeStruct((b, l, DN_V), BF16),
            jax.ShapeDtypeStruct((b, DN_HEADS, DN_HEAD_DIM, DN_HEAD_DIM), F32),
        ],
        scratch_shapes=[pltpu.VMEM((nr, DN_HEADS, DN_HEAD_DIM, DN_HEAD_DIM), F32)],
        compiler_params=pltpu.CompilerParams(dimension_semantics=("parallel", "arbitrary"),
                                             vmem_limit_bytes=VMEM_LIMIT),
        name="dn_chunk",
    )(qkv, small, z, s0, alog, dtb, nh, ones_bd)


def _gla_body(qk_ref, v_ref, r_ref, sm_ref, wa_ref, ba_ref, nh_ref, s0_ref, o_ref, sfin_ref, s_scr):
    c_id = pl.program_id(1)
    n_rows = qk_ref.shape[0]
    items = [(b, h) for b in range(n_rows) for h in range(GLA_HEADS)]
    n_items = range(len(items))

    @pl.when(c_id == 0)
    def _():
        for b in range(n_rows):
            s_scr[b] = s0_ref[...]

    b_all = []
    for b in range(n_rows):
        la = jnp.dot(sm_ref[b], wa_ref[...], preferred_element_type=F32) + ba_ref[...]
        log_alpha = (jnp.minimum(la, 0.0) - jnp.log(1.0 + jnp.exp(-jnp.abs(la)))) * (1.0 / GLA_TAU)
        b_all.append(_cumsum_rows(log_alpha))

    r, c = _tri_incl(CHUNK)
    causal = r >= c
    scale = GLA_KEY_DIM ** -0.5
    mid = CHUNK // 2 - 1

    q = [qk_ref[b, :, h * GLA_KEY_DIM:(h + 1) * GLA_KEY_DIM].astype(F32) * scale for b, h in items]
    k = [qk_ref[b, :, GLA_QK + h * GLA_KEY_DIM:GLA_QK + (h + 1) * GLA_KEY_DIM].astype(F32) for b, h in items]
    v = [v_ref[b, :, h * GLA_VAL_DIM:(h + 1) * GLA_VAL_DIM].astype(F32) for b, h in items]
    bh = [b_all[b][:, h * GLA_KEY_DIM:(h + 1) * GLA_KEY_DIM] for b, h in items]
    bmid = [x[mid:mid + 1, :] for x in bh]
    blast = [x[CHUNK - 1:CHUNK, :] for x in bh]
    qs = [q[i] * jnp.exp(bh[i] - bmid[i]) for i in n_items]
    ks = [k[i] * jnp.exp(bmid[i] - bh[i]) for i in n_items]
    qd = [q[i] * jnp.exp(bh[i]) for i in n_items]
    kd = [k[i] * jnp.exp(blast[i] - bh[i]) for i in n_items]
    st = [s_scr[b, h] for b, h in items]
    att = [lax.dot_general(qs[i], ks[i], _NT, preferred_element_type=F32) for i in n_items]
    o_inter = [lax.dot_general(qd[i], st[i], _NT, preferred_element_type=F32) for i in n_items]
    s_add = [lax.dot_general(v[i], kd[i], _TN, preferred_element_type=F32) for i in n_items]
    att = [jnp.where(causal, a, 0.0) for a in att]
    o_intra = [jnp.dot(att[i], v[i], preferred_element_type=F32) for i in n_items]
    for i, (b, h) in enumerate(items):
        vc = h * GLA_VAL_DIM
        s_scr[b, h] = st[i] * jnp.exp(blast[i]) + s_add[i]
        gate = _silu(r_ref[b, :, vc:vc + GLA_VAL_DIM].astype(F32))
        o = o_inter[i] + o_intra[i]
        o_ref[b, :, vc:vc + GLA_VAL_DIM] = (_rms(o) * nh_ref[...] * gate).astype(o_ref.dtype)

    @pl.when(c_id == pl.num_programs(1) - 1)
    def _():
        sfin_ref[...] = s_scr[...]


def _gla_chunk(qk, v, rr, small, wa, ba, nh, s0):
    b, l, _ = qk.shape
    nc = l // CHUNK
    nr = _rows_per_step(b, 4)
    const2 = lambda bi, ci: (0, 0)
    blk = lambda w: pl.BlockSpec((nr, CHUNK, w), lambda bi, ci: (bi, ci, 0))
    return pl.pallas_call(
        _gla_body, grid=(b // nr, nc),
        in_specs=[
            blk(2 * GLA_QK), blk(GLA_V), blk(GLA_V), blk(SMALL_W),
            pl.BlockSpec(wa.shape, const2), pl.BlockSpec(ba.shape, const2), pl.BlockSpec(nh.shape, const2),
            pl.BlockSpec(s0.shape, lambda bi, ci: (0, 0, 0)),
        ],
        out_specs=[
            blk(GLA_V),
            pl.BlockSpec((nr, GLA_HEADS, GLA_VAL_DIM, GLA_KEY_DIM), lambda bi, ci: (bi, 0, 0, 0)),
        ],
        out_shape=[
            jax.ShapeDtypeStruct((b, l, GLA_V), BF16),
            jax.ShapeDtypeStruct((b, GLA_HEADS, GLA_VAL_DIM, GLA_KEY_DIM), F32),
        ],
        scratch_shapes=[pltpu.VMEM((nr, GLA_HEADS, GLA_VAL_DIM, GLA_KEY_DIM), F32)],
        compiler_params=pltpu.CompilerParams(dimension_semantics=("parallel", "arbitrary"),
                                             vmem_limit_bytes=VMEM_LIMIT),
        name="gla_chunk",
    )(qk, v, rr, small, wa, ba, nh, s0)


TOK_SUB = D_MODEL // LANES
assert TOK_SUB == SUBLANES


def _tiles_to_rows(ref, n):
    return jnp.concatenate([ref[pl.ds(s, n, stride=TOK_SUB), :] for s in range(TOK_SUB)], axis=1)


def _rows_to_tiles(ref, value):
    n = value.shape[0]
    for s in range(TOK_SUB):
        ref[pl.ds(s, n, stride=TOK_SUB), :] = value[:, s * LANES:(s + 1) * LANES]


def _tile_rows(tok, count=1):
    return pl.ds(pl.multiple_of(tok * TOK_SUB, TOK_SUB), count * TOK_SUB)


def _outproj_body(odn_ref, ogla_ref, gates_ref, x_ref, wd_ref, wg_ref, wo_ref, gn_ref, wr_ref, br_ref,
                  h2_ref, hn_ref, lg_ref):
    y_dn = jnp.dot(odn_ref[...], wd_ref[...], preferred_element_type=F32)
    y_gla = jnp.dot(ogla_ref[...], wg_ref[...], preferred_element_type=F32)
    gd = _sigmoid(gates_ref[:, 0:D_MODEL].astype(F32))
    gg = _sigmoid(gates_ref[:, D_MODEL:2 * D_MODEL].astype(F32))
    merged = gd * y_dn + gg * y_gla
    h2 = x_ref[...] + _bdot(merged, wo_ref[...])
    h2_ref[...] = h2
    hn = _rms(h2) * gn_ref[...]
    _rows_to_tiles(hn_ref, hn)
    wr = wr_ref[...]
    wr_hi, hn_hi = _bf16_part(wr), _bf16_part(hn)
    nt = lambda a, b: lax.dot_general(a, b, _NT, preferred_element_type=F32)
    lg_ref[...] = nt(wr_hi, hn_hi) + nt(wr_hi, hn - hn_hi) + nt(wr - wr_hi, hn_hi) + br_ref[...]


def _out_proj(o_dn, o_gla, gates, x2d, wd, wg, wo, gn, wr_t, br, rows):
    t = x2d.shape[0]
    row_blk = lambda w: pl.BlockSpec((rows, w), lambda i: (i, 0))
    const = lambda a: pl.BlockSpec(a.shape, lambda i: (0, 0))
    return pl.pallas_call(
        _outproj_body, grid=(t // rows,),
        in_specs=[row_blk(DN_V), row_blk(GLA_V), row_blk(2 * D_MODEL), row_blk(D_MODEL),
                  const(wd), const(wg), const(wo), const(gn), const(wr_t), const(br)],
        out_specs=[row_blk(D_MODEL), pl.BlockSpec((rows * TOK_SUB, LANES), lambda i: (i, 0)),
                   pl.BlockSpec((LANES, rows), lambda i: (0, i))],
        out_shape=[jax.ShapeDtypeStruct((t, D_MODEL), F32), jax.ShapeDtypeStruct((t * TOK_SUB, LANES), F32),
                   jax.ShapeDtypeStruct((LANES, t), F32)],
        compiler_params=pltpu.CompilerParams(dimension_semantics=("parallel",), vmem_limit_bytes=VMEM_LIMIT),
        name="out_proj",
    )(o_dn, o_gla, gates, x2d, wd, wg, wo, gn, wr_t, br)


ROUTE_SUB = 256


def _route_body(lg_ref, idx_ref, gate_ref, cnt_ref, carry):
    step = pl.program_id(0)

    @pl.when(step == 0)
    def _():
        carry[...] = jnp.zeros_like(carry)

    tt = lg_ref.shape[1]
    gl = lg_ref[0:N_GROUPS, :]
    gmax = jnp.max(gl, axis=0, keepdims=True)
    rid8 = lax.broadcasted_iota(jnp.int32, (N_GROUPS, tt), 0)
    gsel = jnp.min(jnp.where(gl == gmax, rid8, N_GROUPS), axis=0, keepdims=True)
    gw = 1.0 / jnp.sum(jnp.exp(gl - gmax), axis=0, keepdims=True)
    el = lg_ref[N_GROUPS:N_GROUPS + N_EXPERTS, :]
    rid = lax.broadcasted_iota(jnp.int32, (N_EXPERTS, tt), 0)
    neg = jnp.float32(-jnp.inf)
    ein = jnp.where((rid >> 3) == gsel, el, neg)
    t1 = jnp.max(ein, axis=0, keepdims=True)
    i1 = jnp.min(jnp.where(ein == t1, rid, N_EXPERTS), axis=0, keepdims=True)
    ein2 = jnp.where(rid == i1, neg, ein)
    t2 = jnp.max(ein2, axis=0, keepdims=True)
    i2 = jnp.min(jnp.where(ein2 == t2, rid, N_EXPERTS), axis=0, keepdims=True)
    e21 = jnp.exp(t2 - t1)
    den = 1.0 / (1.0 + e21)
    sel1 = rid == i1
    sel2 = rid == i2
    onehot = jnp.where(sel1 | sel2, 1.0, 0.0)

    ur, uc = _tri_incl(ROUTE_SUB)
    upper = (ur <= uc).astype(BF16)
    run = carry[...]
    r1_parts, r2_parts = [], []
    for s in range(tt // ROUTE_SUB):
        sl = slice(s * ROUTE_SUB, (s + 1) * ROUTE_SUB)
        oh = onehot[:, sl]
        incl = jnp.dot(oh.astype(BF16), upper, preferred_element_type=F32) + run
        excl = incl - oh
        r1_parts.append(jnp.sum(jnp.where(sel1[:, sl], excl, 0.0), axis=0, keepdims=True))
        r2_parts.append(jnp.sum(jnp.where(sel2[:, sl], excl, 0.0), axis=0, keepdims=True))
        run = jnp.broadcast_to(incl[:, ROUTE_SUB - 1:ROUTE_SUB], run.shape)
    carry[...] = run
    r1 = jnp.concatenate(r1_parts, axis=1) if len(r1_parts) > 1 else r1_parts[0]
    r2 = jnp.concatenate(r2_parts, axis=1) if len(r2_parts) > 1 else r2_parts[0]

    idx_ref[...] = jnp.zeros_like(idx_ref)
    idx_ref[0:1, :] = i1
    idx_ref[1:2, :] = i2
    idx_ref[2:3, :] = r1.astype(jnp.int32)
    idx_ref[3:4, :] = r2.astype(jnp.int32)
    gate_ref[...] = jnp.zeros_like(gate_ref)
    gate_ref[0:1, :] = den * gw
    gate_ref[1:2, :] = e21 * den * gw
    cnt_ref[...] = run[:, 0:LANES]


def _route(logits_t, lanes):
    t = logits_t.shape[1]
    return pl.pallas_call(
        _route_body, grid=(t // lanes,),
        in_specs=[pl.BlockSpec((LANES, lanes), lambda i: (0, i))],
        out_specs=[pl.BlockSpec((8, lanes), lambda i: (0, i)), pl.BlockSpec((8, lanes), lambda i: (0, i)),
                   pl.BlockSpec((N_EXPERTS, LANES), lambda i: (0, 0))],
        out_shape=[jax.ShapeDtypeStruct((8, t), jnp.int32), jax.ShapeDtypeStruct((8, t), F32),
                   jax.ShapeDtypeStruct((N_EXPERTS, LANES), F32)],
        scratch_shapes=[pltpu.VMEM((N_EXPERTS, ROUTE_SUB), F32)],
        compiler_params=pltpu.CompilerParams(dimension_semantics=("arbitrary",)),
        name="route",
    )(logits_t)


def _row_copy(src_ref, src_tok, dst_ref, dst_tok, sem):
    return pltpu.make_async_copy(src_ref.at[_tile_rows(src_tok)], dst_ref.at[_tile_rows(dst_tok)], sem)


ROW_UNROLL = 8


def _slots_body(pstart_ref, idx_ref, slot_ref):
    idx = idx_ref[...]
    base = jnp.zeros_like(idx)
    for e in range(N_EXPERTS):
        base = jnp.where(idx == e, pstart_ref[e], base)
    slot_ref[...] = jnp.zeros_like(idx)
    slot_ref[0:2, :] = base[0:2, :] + idx[2:4, :]


def _slots(pstart, idx, lanes):
    t = idx.shape[1]
    grid_spec = pltpu.PrefetchScalarGridSpec(
        num_scalar_prefetch=1, grid=(t // lanes,),
        in_specs=[pl.BlockSpec((8, lanes), lambda i, ps: (0, i))],
        out_specs=pl.BlockSpec((8, lanes), lambda i, ps: (0, i)),
    )
    return pl.pallas_call(
        _slots_body, grid_spec=grid_spec, out_shape=jax.ShapeDtypeStruct(idx.shape, jnp.int32),
        compiler_params=pltpu.CompilerParams(dimension_semantics=("parallel",)),
        name="slots",
    )(pstart, idx)


DISPATCH_BUFS = 3


def _dispatch_body(pstart_ref, size_ref, slot_ref, hn_hbm, xs_ref, zblk, hbuf, sem, load_sems, row_sems):
    step = pl.program_id(0)
    n_steps = pl.num_programs(0)
    n_tok = hbuf.shape[1] // TOK_SUB

    def load(tile):
        buf = tile % DISPATCH_BUFS
        return pltpu.make_async_copy(hn_hbm.at[_tile_rows(tile * n_tok, n_tok)], hbuf.at[buf], load_sems.at[buf])

    def wait_rows(parity):
        for k in range(2):
            pltpu.make_async_copy(hbuf.at[0], xs_ref.at[_tile_rows(0, n_tok)], row_sems.at[parity]).wait()

    @pl.when(step == 0)
    def _():
        load(0).start()

        @pl.when(n_steps > 1)
        def _():
            load(1).start()

        zblk[...] = jnp.zeros_like(zblk)
        parts = MOE_BLOCK // ZERO_ROWS

        def clear_copy(row0, part):
            return pltpu.make_async_copy(zblk, xs_ref.at[_tile_rows(row0 + part * ZERO_ROWS, ZERO_ROWS)], sem)

        def per_expert(e, n_rows):
            size = size_ref[e]
            full = size // MOE_BLOCK * MOE_BLOCK

            @pl.when(size != full)
            def _():
                for part in range(parts):
                    clear_copy(pstart_ref[e] + full, part).start()
                for part in range(parts):
                    clear_copy(pstart_ref[e] + full, part).wait()

            return n_rows + (size + MOE_BLOCK - 1) // MOE_BLOCK * MOE_BLOCK

        n_rows = lax.fori_loop(0, N_EXPERTS, per_expert, 0)

        def clear_unused(j, c):
            for part in range(parts):
                clear_copy(j * MOE_BLOCK, part).start()
            for part in range(parts):
                clear_copy(j * MOE_BLOCK, part).wait()
            return c

        lax.fori_loop(n_rows // MOE_BLOCK, xs_ref.shape[0] // (MOE_BLOCK * TOK_SUB), clear_unused, 0)

    load(step).wait()
    src = hbuf.at[step % DISPATCH_BUFS]
    row_sem = row_sems.at[step % 2]

    def issue(t, carry):
        for k in range(2):
            _row_copy(src, t, xs_ref, slot_ref[k, t], row_sem).start()
        return carry

    lax.fori_loop(0, n_tok, issue, 0, unroll=ROW_UNROLL)

    @pl.when(step > 0)
    def _():
        wait_rows((step - 1) % 2)

    @pl.when(step + 2 < n_steps)
    def _():
        load(step + 2).start()

    @pl.when(step == n_steps - 1)
    def _():
        wait_rows(step % 2)


def _dispatch(pstart, sizes, slots, hn, n_rows_out, tokens):
    t = hn.shape[0] // TOK_SUB
    grid_spec = pltpu.PrefetchScalarGridSpec(
        num_scalar_prefetch=2, grid=(t // tokens,),
        in_specs=[pl.BlockSpec((8, tokens), lambda i, ps, sz: (0, i), memory_space=pltpu.SMEM),
                  pl.BlockSpec(memory_space=pl.ANY)],
        out_specs=pl.BlockSpec(memory_space=pl.ANY),
        scratch_shapes=[pltpu.VMEM((ZERO_ROWS * TOK_SUB, LANES), F32),
                        pltpu.VMEM((DISPATCH_BUFS, tokens * TOK_SUB, LANES), F32),
                        pltpu.SemaphoreType.DMA(()), pltpu.SemaphoreType.DMA((DISPATCH_BUFS,)),
                        pltpu.SemaphoreType.DMA((2,))],
    )
    return pl.pallas_call(
        _dispatch_body, grid_spec=grid_spec,
        out_shape=jax.ShapeDtypeStruct((n_rows_out * TOK_SUB, LANES), F32),
        compiler_params=pltpu.CompilerParams(dimension_semantics=("arbitrary",)),
        name="dispatch",
    )(pstart, sizes, slots, hn)


def _experts_body(be_ref, nused_ref, xs_ref, w1_ref, w3_ref, w2_ref, yb_ref, w1b, w3b, w2b):
    j = pl.program_id(0)
    used = j < nused_ref[0]
    new_expert = jnp.logical_or(j == 0, be_ref[j] != be_ref[jnp.maximum(j - 1, 0)])

    @pl.when(jnp.logical_and(used, new_expert))
    def _():
        w1b[...] = w1_ref[...].astype(BF16)
        w3b[...] = w3_ref[...].astype(BF16)
        w2b[...] = w2_ref[...].astype(BF16)

    @pl.when(used)
    def _():
        xb = _tiles_to_rows(xs_ref, MOE_BLOCK).astype(BF16)
        h1 = jnp.dot(xb, w1b[...], preferred_element_type=F32)
        h3 = jnp.dot(xb, w3b[...], preferred_element_type=F32)
        hid = (_silu(h1) * h3).astype(BF16)
        _rows_to_tiles(yb_ref, jnp.dot(hid, w2b[...], preferred_element_type=F32))

    @pl.when(jnp.logical_not(used))
    def _():
        yb_ref[...] = jnp.zeros_like(yb_ref)


def _experts(block_expert, n_used, xs, w1, w3, w2):
    p = xs.shape[0] // TOK_SUB
    nb = p // MOE_BLOCK
    row_map = lambda j, be, nu: (jnp.minimum(j, nu[0] - 1), 0)
    w_map = lambda j, be, nu: (be[j], 0, 0)
    grid_spec = pltpu.PrefetchScalarGridSpec(
        num_scalar_prefetch=2, grid=(nb,),
        in_specs=[pl.BlockSpec((MOE_BLOCK * TOK_SUB, LANES), row_map),
                  pl.BlockSpec((None, D_MODEL, D_EXPERT), w_map),
                  pl.BlockSpec((None, D_MODEL, D_EXPERT), w_map),
                  pl.BlockSpec((None, D_EXPERT, D_MODEL), w_map)],
        out_specs=pl.BlockSpec((MOE_BLOCK * TOK_SUB, LANES), lambda j, be, nu: (j, 0)),
        scratch_shapes=[pltpu.VMEM((D_MODEL, D_EXPERT), BF16), pltpu.VMEM((D_MODEL, D_EXPERT), BF16),
                        pltpu.VMEM((D_EXPERT, D_MODEL), BF16)],
    )
    return pl.pallas_call(
        _experts_body, grid_spec=grid_spec,
        out_shape=jax.ShapeDtypeStruct((p * TOK_SUB, LANES), F32),
        compiler_params=pltpu.CompilerParams(dimension_semantics=("arbitrary",), vmem_limit_bytes=VMEM_LIMIT),
        name="experts",
    )(block_expert, n_used, xs, w1, w3, w2)


def _combine_body(slot_ref, slot_next_ref, gate_ref, h2_ref, gf_ref, yb_ref, o_ref, ybuf, sems):
    n_tok = h2_ref.shape[0]
    step = pl.program_id(0)
    cur = step % 2

    def gather(slots, buf_id):
        def issue(t, carry):
            for k in range(2):
                pltpu.make_async_copy(yb_ref.at[_tile_rows(slots[k, t])], ybuf.at[buf_id, k, _tile_rows(t)],
                                      sems.at[buf_id]).start()
            return carry

        lax.fori_loop(0, n_tok, issue, 0, unroll=ROW_UNROLL)

    @pl.when(step == 0)
    def _():
        gather(slot_ref, 0)

    @pl.when(step + 1 < pl.num_programs(0))
    def _():
        gather(slot_next_ref, 1 - cur)

    for k in range(2):
        pltpu.make_async_copy(yb_ref.at[_tile_rows(0, n_tok)], ybuf.at[cur, k], sems.at[cur]).wait()
    y0 = _tiles_to_rows(ybuf.at[cur, 0], n_tok)
    y1 = _tiles_to_rows(ybuf.at[cur, 1], n_tok)
    y = h2_ref[...] + gate_ref[:, 0:1] * y0 + gate_ref[:, 1:2] * y1
    o_ref[...] = _rms(y) * gf_ref[...]


def _combine(slots, gate_t, h2, gf, yb, tokens):
    t = h2.shape[0]
    last = t // tokens - 1
    return pl.pallas_call(
        _combine_body, grid=(t // tokens,),
        in_specs=[pl.BlockSpec((8, tokens), lambda i: (0, i), memory_space=pltpu.SMEM),
                  pl.BlockSpec((8, tokens), lambda i: (0, jnp.minimum(i + 1, last)), memory_space=pltpu.SMEM),
                  pl.BlockSpec((tokens, 8), lambda i: (i, 0)),
                  pl.BlockSpec((tokens, D_MODEL), lambda i: (i, 0)),
                  pl.BlockSpec((1, D_MODEL), lambda i: (0, 0)),
                  pl.BlockSpec(memory_space=pl.ANY)],
        out_specs=pl.BlockSpec((tokens, D_MODEL), lambda i: (i, 0)),
        scratch_shapes=[pltpu.VMEM((2, 2, tokens * TOK_SUB, LANES), F32), pltpu.SemaphoreType.DMA((2,))],
        out_shape=jax.ShapeDtypeStruct((t, D_MODEL), F32),
        compiler_params=pltpu.CompilerParams(dimension_semantics=("arbitrary",)),
        name="combine",
    )(slots, slots, gate_t, h2, gf, yb)


def _pad_lanes(v, width=LANES):
    v = v.reshape(1, -1).astype(F32)
    return jnp.pad(v, ((0, 0), (0, width - v.shape[1])))


def _largest_tile(n, cap):
    t = cap
    while n % t:
        t //= 2
    return t


def kernel(x, meta_tokens, norm_mix, w_in, conv_dn, a_log, dt_bias, norm_head_dn, w_proj_dn, w_alpha, b_alpha,
           norm_head_gla, w_proj_gla, w_out, norm_ffn, w_group, b_group, w_router, b_router, w1, w3, w2, norm_final):
    assert norm_mix.shape[0] == 1, "single-layer block"
    bsz, seq, d = x.shape
    assert d == D_MODEL and seq % CHUNK == 0
    t = bsz * seq

    wi = w_in[0]
    offs = np.cumsum([0, DN_QK, DN_QK, DN_V, DN_V, DN_HEADS, DN_HEADS, GLA_QK, GLA_QK, GLA_V, GLA_V, GLA_RANK,
                      D_MODEL, D_MODEL]).tolist()
    seg = lambda i, j: wi[:, offs[i]:offs[j]]
    n_small = 2 * DN_HEADS + GLA_RANK
    w_all = jnp.concatenate([seg(0, 4), seg(6, 10), seg(11, 13), seg(4, 6), seg(10, 11),
                             jnp.zeros((D_MODEL, SMALL_W - n_small), F32)], axis=1).astype(BF16)
    g_mix = norm_mix[0].reshape(1, D_MODEL)

    conv_w = jnp.pad(conv_dn[0], ((0, 8 - CONV_WIDTH), (0, 0)))
    alog = _pad_lanes(a_log[0])
    dtb = _pad_lanes(dt_bias[0])
    nh_dn = norm_head_dn[0].reshape(1, DN_HEAD_DIM)
    lr_off = 2 * DN_HEADS
    wa = jnp.zeros((SMALL_W, GLA_QK), F32).at[lr_off:lr_off + GLA_RANK].set(w_alpha[0])
    ba = b_alpha[0].reshape(1, GLA_QK)
    nh_gla = norm_head_gla[0].reshape(1, GLA_VAL_DIM)

    def mixers(tokens2d, nb, hist, s0_dn, s0_gla, rows):
        tiles_per_seq = tokens2d.shape[0] // nb // rows
        qkv, z, qk_g, v_g, r_g, gates, small, tail = _in_proj(tokens2d, g_mix, w_all, conv_w, hist, rows, tiles_per_seq)
        r3 = lambda a: a.reshape(nb, -1, a.shape[-1])
        o_dn, s_dn = _dn_chunk(r3(qkv), r3(small), r3(z), s0_dn, alog, dtb, nh_dn)
        o_gla, s_gla = _gla_chunk(r3(qk_g), r3(v_g), r3(r_g), r3(small), wa, ba, nh_gla, s0_gla)
        return tail, o_dn, o_gla, gates, s_dn, s_gla

    meta_rows = jnp.pad(meta_tokens.astype(F32), ((CHUNK - N_META, 0), (0, 0)))
    zero_hist = jnp.zeros((HIST_ROWS, 3 * DN_QK), F32)
    zero_dn = jnp.zeros((DN_HEADS, DN_HEAD_DIM, DN_HEAD_DIM), F32)
    zero_gla = jnp.zeros((GLA_HEADS, GLA_VAL_DIM, GLA_KEY_DIM), F32)
    hist, _, _, _, s_dn_m, s_gla_m = mixers(meta_rows, 1, zero_hist, zero_dn, zero_gla, CHUNK)

    x2d = x.reshape(t, d)
    rows = _largest_tile(seq, 512)
    _, o_dn, o_gla, gates, _, _ = mixers(x2d, bsz, hist, s_dn_m[0], s_gla_m[0], rows)

    wr_t = jnp.concatenate([w_group[0], w_router[0]], axis=1).T
    wr_t = jnp.pad(wr_t, ((0, LANES - wr_t.shape[0]), (0, 0)))
    br = jnp.pad(jnp.concatenate([b_group[0], b_router[0]]), (0, LANES - N_GROUPS - N_EXPERTS)).reshape(LANES, 1)
    h2, hn, logits_t = _out_proj(
        o_dn.reshape(t, DN_V), o_gla.reshape(t, GLA_V), gates, x2d,
        w_proj_dn[0].astype(BF16), w_proj_gla[0].astype(BF16), w_out[0].astype(BF16),
        norm_ffn[0].reshape(1, D_MODEL), wr_t, br, rows)

    idx, gate, cnt = _route(logits_t, _largest_tile(t, 512))

    sizes = cnt[:, 0].astype(jnp.int32)
    padded = (sizes + MOE_BLOCK - 1) // MOE_BLOCK * MOE_BLOCK
    pends = jnp.cumsum(padded)
    pstart = (pends - padded).astype(jnp.int32)
    n_blocks = (2 * t) // MOE_BLOCK + N_EXPERTS
    n_used = (pends[-1:] // MOE_BLOCK).astype(jnp.int32)
    block_row0 = jnp.arange(n_blocks, dtype=jnp.int32) * MOE_BLOCK
    block_expert = jnp.minimum(jnp.sum(pends[None, :] <= block_row0[:, None], axis=1), N_EXPERTS - 1).astype(jnp.int32)

    tok_tile = _largest_tile(t, 256)
    slots = _slots(pstart, idx, _largest_tile(t, 2048))
    xs = _dispatch(pstart, sizes, slots, hn, n_blocks * MOE_BLOCK, tok_tile)
    yb = _experts(block_expert, n_used, xs, w1[0], w3[0], w2[0])
    out = _combine(slots, gate.T, h2, norm_final.reshape(1, D_MODEL), yb, tok_tile)
    return out.reshape(bsz, seq, d)
```

```python
import functools

import jax
import jax.numpy as jnp
import numpy as np
from jax import lax
from jax.experimental import pallas as pl
from jax.experimental.pallas import tpu as pltpu

F32 = jnp.float32
BF16 = jnp.bfloat16

D_MODEL = 1024
CHUNK = 64
N_META = 16
EPS = 1e-6
DN_HEADS = 8
DN_HEAD_DIM = 128
DN_QK = DN_HEADS * DN_HEAD_DIM
DN_V = DN_HEADS * DN_HEAD_DIM
CONV_WIDTH = 4
GLA_HEADS = 4
GLA_KEY_DIM = 128
GLA_VAL_DIM = 256
GLA_QK = GLA_HEADS * GLA_KEY_DIM
GLA_V = GLA_HEADS * GLA_VAL_DIM
GLA_RANK = 16
GLA_TAU = 16.0
N_GROUPS = 8
EXPERTS_PER_GROUP = 8
N_EXPERTS = N_GROUPS * EXPERTS_PER_GROUP
D_EXPERT = 512
MOE_BLOCK = 512
ZERO_ROWS = 256

LANES = 128
SUBLANES = 8
SMALL_W = LANES
HIST_ROWS = 8
VMEM_LIMIT = 56 * 1024 * 1024

_NT = (((1,), (1,)), ((), ()))
_TN = (((0,), (0,)), ((), ()))


def _bdot(a, b):
    return jnp.dot(a.astype(BF16), b.astype(BF16), preferred_element_type=F32)


def _bdot_nt(a, b):
    return lax.dot_general(a.astype(BF16), b.astype(BF16), _NT, preferred_element_type=F32)


def _bdot_tn(a, b):
    return lax.dot_general(a.astype(BF16), b.astype(BF16), _TN, preferred_element_type=F32)


def _sigmoid(x):
    return 0.5 + 0.5 * jnp.tanh(0.5 * x)


def _silu(x):
    half = 0.5 * x
    return half + half * jnp.tanh(half)


def _softplus(x):
    return jnp.maximum(x, 0.0) + jnp.log(1.0 + jnp.exp(-jnp.abs(x)))


def _rms(x, eps=EPS):
    return x * lax.rsqrt(jnp.mean(x * x, axis=-1, keepdims=True) + eps)


def _tri_incl(n):
    r = lax.broadcasted_iota(jnp.int32, (n, n), 0)
    c = lax.broadcasted_iota(jnp.int32, (n, n), 1)
    return r, c


def _bf16_part(x):
    bits = pltpu.bitcast(x, jnp.uint32) & jnp.uint32(0xFFFF0000)
    return pltpu.bitcast(bits, F32)


def _cumsum_rows(x):
    r, c = _tri_incl(x.shape[0])
    tri = (r >= c).astype(F32)
    hi = _bf16_part(x)
    r1 = x - hi
    mid = _bf16_part(r1)
    lo = r1 - mid
    return (jnp.dot(tri, hi, preferred_element_type=F32) + jnp.dot(tri, mid, preferred_element_type=F32)
            + jnp.dot(tri, lo, preferred_element_type=F32))


IN_GROUPS = ((3 * DN_QK, BF16), (DN_V, BF16), (2 * GLA_QK, BF16), (GLA_V, BF16), (GLA_V, BF16),
             (2 * D_MODEL, BF16), (SMALL_W, F32))


CONV_COLS = 256
PLAIN_COLS = 256


def _inproj_body(tiles_per_seq, x_ref, g_ref, w_ref, cw_ref, hist_ref, *refs):
    o_refs, tail_ref, pre, carry = refs[:-3], refs[-3], refs[-2], refs[-1]
    rows = x_ref.shape[0]

    @pl.when(pl.program_id(0) % tiles_per_seq == 0)
    def _():
        carry[...] = hist_ref[...]

    x = x_ref[...]
    xb = (_rms(x) * g_ref[...]).astype(BF16)

    qkv_ref = o_refs[0]

    def conv_pass(c0):
        cols = slice(c0, c0 + CONV_COLS)
        pre[0:HIST_ROWS, :] = carry[:, cols]
        pre[HIST_ROWS:HIST_ROWS + rows, :] = jnp.dot(xb, w_ref[:, cols], preferred_element_type=F32)
        carry[:, cols] = pre[rows:rows + HIST_ROWS, :]
        acc = None
        for j in range(CONV_WIDTH):
            lo = HIST_ROWS - (CONV_WIDTH - 1) + j
            term = pre[lo:lo + rows, :] * cw_ref[j:j + 1, cols]
            acc = term if acc is None else acc + term
        qkv_ref[:, cols] = _silu(acc).astype(qkv_ref.dtype)

    def plain_pass(o_ref, c0, width, wcol):
        o_ref[:, c0:c0 + width] = jnp.dot(xb, w_ref[:, wcol + c0:wcol + c0 + width],
                                          preferred_element_type=F32).astype(o_ref.dtype)

    plain_tasks = []
    wcol = qkv_ref.shape[1]
    for o_ref in o_refs[1:]:
        width = o_ref.shape[1]
        step = min(width, PLAIN_COLS)
        plain_tasks += [functools.partial(plain_pass, o_ref, c0, step, wcol) for c0 in range(0, width, step)]
        wcol += width

    conv_tasks = [functools.partial(conv_pass, c0) for c0 in range(0, qkv_ref.shape[1], CONV_COLS)]
    n_conv, n_plain = len(conv_tasks), len(plain_tasks)
    for i, conv_task in enumerate(conv_tasks):
        for task in plain_tasks[i * n_plain // n_conv:(i + 1) * n_plain // n_conv]:
            task()
        conv_task()
    tail_ref[...] = carry[...]


def _in_proj(x2d, gain, w_all, conv_w, hist, rows, tiles_per_seq):
    t = x2d.shape[0]
    const = lambda a: pl.BlockSpec(a.shape, lambda i: (0, 0))
    in_specs = [pl.BlockSpec((rows, D_MODEL), lambda i: (i, 0)), const(gain),
                pl.BlockSpec(w_all.shape, lambda i: (0, 0), pipeline_mode=pl.Buffered(1)),
                const(conv_w), const(hist)]
    out_specs = [pl.BlockSpec((rows, w), lambda i: (i, 0)) for w, _ in IN_GROUPS] + [const(hist)]
    out_shape = [jax.ShapeDtypeStruct((t, w), dt) for w, dt in IN_GROUPS] + [jax.ShapeDtypeStruct(hist.shape, F32)]
    return pl.pallas_call(
        functools.partial(_inproj_body, tiles_per_seq), grid=(t // rows,),
        in_specs=in_specs, out_specs=out_specs, out_shape=out_shape,
        scratch_shapes=[pltpu.VMEM((HIST_ROWS + rows, CONV_COLS), F32), pltpu.VMEM(hist.shape, F32)],
        compiler_params=pltpu.CompilerParams(dimension_semantics=("arbitrary",), vmem_limit_bytes=VMEM_LIMIT),
        name="in_proj",
    )(x2d, gain, w_all, conv_w, hist)


def _block_diag2(m2):
    n = m2.shape[0]
    lane = lax.broadcasted_iota(jnp.int32, m2.shape, 1)
    return jnp.concatenate([jnp.where(lane < n, m2, 0.0), jnp.where(lane >= n, m2, 0.0)], axis=0)


def _unit_lower_inverse(mats):
    n = mats[0].shape[0]
    r = lax.broadcasted_iota(jnp.int32, (n, 2 * n), 0)
    c = lax.broadcasted_iota(jnp.int32, (n, 2 * n), 1) & (n - 1)
    eye = (r == c).astype(F32)
    pair = (r >> 1) == (c >> 1)
    ts = [eye - jnp.where(pair, a, 0.0) for a in mats]
    m = 2
    while m < n:
        sh = m.bit_length() - 1
        keep = ((r >> (sh + 1)) == (c >> (sh + 1))) & ((r >> sh) != (c >> sh))
        ams = [_block_diag2(jnp.where(keep, a, 0.0)) for a in mats]
        xs = [jnp.dot(t, am, preferred_element_type=F32) for t, am in zip(ts, ams)]
        ys = [jnp.dot(x, _block_diag2(t), preferred_element_type=F32) for x, t in zip(xs, ts)]
        ts = [t - y for t, y in zip(ts, ys)]
        m *= 2
    return ts


def _l2_normalise(xs, ones_bd, scale):
    out = []
    for p in range(0, len(xs), 2):
        sq = jnp.concatenate([xs[p] * xs[p], xs[p + 1] * xs[p + 1]], axis=1)
        ss = jnp.dot(sq, ones_bd, preferred_element_type=F32)
        inv = lax.rsqrt(ss + EPS) * scale
        out.append(xs[p] * inv[:, :DN_HEAD_DIM])
        out.append(xs[p + 1] * inv[:, DN_HEAD_DIM:])
    return out


DN_WAVE = 32


def _dn_body(qkv_ref, sm_ref, z_ref, s0_ref, alog_ref, dtb_ref, nh_ref, ones_ref, o_ref, sfin_ref, s_scr):
    c_id = pl.program_id(1)
    n_rows = qkv_ref.shape[0]
    all_items = [(b, h) for b in range(n_rows) for h in range(DN_HEADS)]

    @pl.when(c_id == 0)
    def _():
        for b in range(n_rows):
            s_scr[b] = s0_ref[...]

    def head_cols(b, col):
        return qkv_ref[b, :, col:col + DN_HEAD_DIM].astype(F32)

    gc_all, gc_t, beta_all = [], [], []
    for b in range(n_rows):
        sm = sm_ref[b]
        g_b = -jnp.exp(alog_ref[...]) * _softplus(sm + dtb_ref[...])
        beta_all.append(_sigmoid(sm))
        gc_b = _cumsum_rows(g_b)
        gc_all.append(gc_b)
        gc_t.append(jnp.concatenate([gc_b, jnp.zeros_like(gc_b)], axis=0).T)

    r2 = lax.broadcasted_iota(jnp.int32, (CHUNK, 2 * CHUNK), 0)
    lane2 = lax.broadcasted_iota(jnp.int32, (CHUNK, 2 * CHUNK), 1)
    c2 = lane2 & (CHUNK - 1)
    first = lane2 < CHUNK
    causal2 = r2 >= c2
    strict2 = r2 > c2
    scale = DN_HEAD_DIM ** -0.5
    ones_bd = ones_ref[...]

    def wave(items):
        n_items = range(len(items))
        q = [head_cols(b, h * DN_HEAD_DIM) for b, h in items]
        k = [head_cols(b, DN_QK + h * DN_HEAD_DIM) for b, h in items]
        v = [head_cols(b, 2 * DN_QK + h * DN_HEAD_DIM) for b, h in items]
        q = _l2_normalise(q, ones_bd, scale)
        k = _l2_normalise(k, ones_bd, 1.0)
        gcol = [gc_all[b][:, h:h + 1] for b, h in items]
        bcol = [beta_all[b][:, DN_HEADS + h:DN_HEADS + h + 1] for b, h in items]
        glast = [gc_all[b][CHUNK - 1:CHUNK, h:h + 1] for b, h in items]
        eg = [jnp.exp(gcol[i]) for i in n_items]

        pairs = range(0, len(items), 2)
        n_pairs = range(len(pairs))

        def diag2(x0, x1):
            return jnp.concatenate([jnp.concatenate([x0, jnp.zeros_like(x1)], axis=1),
                                    jnp.concatenate([jnp.zeros_like(x0), x1], axis=1)], axis=0)

        def lanes2(x0, x1):
            return jnp.where(first, x0, x1)

        def head_row(i):
            b, h = items[i]
            return gc_t[b][h:h + 1, :]

        k_bd = [diag2(k[p], k[p + 1]) for p in pairs]
        k_cat = [jnp.concatenate([k[p], k[p + 1]], axis=1) for p in pairs]
        q_cat = [jnp.concatenate([q[p], q[p + 1]], axis=1) for p in pairs]
        kk2 = [lax.dot_general(k_cat[j], k_bd[j], _NT, preferred_element_type=F32) for j in n_pairs]
        qk2 = [lax.dot_general(q_cat[j], k_bd[j], _NT, preferred_element_type=F32) for j in n_pairs]
        grow2 = [head_row(p) + pltpu.roll(head_row(p + 1), CHUNK, axis=1) for p in pairs]
        gcol2 = [lanes2(gcol[p], gcol[p + 1]) for p in pairs]
        bcol2 = [lanes2(bcol[p], bcol[p + 1]) for p in pairs]
        decay2 = [jnp.where(causal2, jnp.exp(jnp.where(causal2, gcol2[j] - grow2[j], 0.0)), 0.0) for j in n_pairs]
        a_mat2 = [jnp.where(strict2, bcol2[j] * kk2[j] * decay2[j], 0.0) for j in n_pairs]
        qk2 = [jnp.where(causal2, qk2[j] * decay2[j], 0.0) for j in n_pairs]
        t_inv2 = _unit_lower_inverse(a_mat2)
        rhs = [jnp.concatenate([v[i] * bcol[i], k[i] * (bcol[i] * eg[i])], axis=1) for i in n_items]
        sol2 = [jnp.dot(t_inv2[j], diag2(rhs[p], rhs[p + 1]), preferred_element_type=F32)
                for j, p in enumerate(pairs)]
        sol = [sol2[i // 2][:, (i % 2) * 2 * DN_HEAD_DIM:(i % 2 + 1) * 2 * DN_HEAD_DIM] for i in n_items]
        s_old = [s_scr[b, h] for b, h in items]
        qe = [q[i] * eg[i] for i in n_items]
        k_dec = [k[i] * jnp.exp(glast[i] - gcol[i]) for i in n_items]
        ws = [jnp.dot(sol[i][:, DN_HEAD_DIM:], s_old[i], preferred_element_type=F32) for i in n_items]
        o_inter = [jnp.dot(qe[i], s_old[i], preferred_element_type=F32) for i in n_items]
        v_new = [sol[i][:, :DN_HEAD_DIM] - ws[i] for i in n_items]
        o_intra2 = [jnp.dot(qk2[j], diag2(v_new[p], v_new[p + 1]), preferred_element_type=F32)
                    for j, p in enumerate(pairs)]
        o_intra = [o_intra2[i // 2][:, (i % 2) * DN_HEAD_DIM:(i % 2 + 1) * DN_HEAD_DIM] for i in n_items]
        s_add = [lax.dot_general(k_dec[i], v_new[i], _TN, preferred_element_type=F32) for i in n_items]
        for i, (b, h) in enumerate(items):
            col = h * DN_HEAD_DIM
            s_scr[b, h] = s_old[i] * jnp.exp(glast[i]) + s_add[i]
            zed = z_ref[b, :, col:col + DN_HEAD_DIM].astype(F32)
            o = o_inter[i] + o_intra[i]
            o_ref[b, :, col:col + DN_HEAD_DIM] = (_rms(o) * nh_ref[...] * _silu(zed)).astype(o_ref.dtype)

    for w0 in range(0, len(all_items), DN_WAVE):
        wave(all_items[w0:w0 + DN_WAVE])

    @pl.when(c_id == pl.num_programs(1) - 1)
    def _():
        sfin_ref[...] = s_scr[...]


def _rows_per_step(b, want):
    while b % want:
        want //= 2
    return want


def _dn_chunk(qkv, small, z, s0, alog, dtb, nh):
    b, l, _ = qkv.shape
    lane_head = np.arange(2 * DN_HEAD_DIM) // DN_HEAD_DIM
    ones_bd = jnp.asarray(lane_head[:, None] == lane_head[None, :], F32)
    nc = l // CHUNK
    nr = _rows_per_step(b, 4)
    const2 = lambda bi, ci: (0, 0)
    blk = lambda w: pl.BlockSpec((nr, CHUNK, w), lambda bi, ci: (bi, ci, 0))
    return pl.pallas_call(
        _dn_body, grid=(b // nr, nc),
        in_specs=[
            blk(3 * DN_QK), blk(SMALL_W), blk(DN_V),
            pl.BlockSpec(s0.shape, lambda bi, ci: (0, 0, 0)),
            pl.BlockSpec(alog.shape, const2),
            pl.BlockSpec(dtb.shape, const2),
            pl.BlockSpec(nh.shape, const2),
            pl.BlockSpec(ones_bd.shape, const2),
        ],
        out_specs=[
            blk(DN_V),
            pl.BlockSpec((nr, DN_HEADS, DN_HEAD_DIM, DN_HEAD_DIM), lambda bi, ci: (bi, 0, 0, 0)),
        ],
        out_shape=[
            jax.ShapeDtypeStruct((b, l, DN_V), BF16),
            jax.ShapeDtypeStruct((b, DN_HEADS, DN_HEAD_DIM, DN_HEAD_DIM), F32),
        ],
        scratch_shapes=[pltpu.VMEM((nr, DN_HEADS, DN_HEAD_DIM, DN_HEAD_DIM), F32)],
        compiler_params=pltpu.CompilerParams(dimension_semantics=("parallel", "arbitrary"),
                                             vmem_limit_bytes=VMEM_LIMIT),
        name="dn_chunk",
    )(qkv, small, z, s0, alog, dtb, nh, ones_bd)


def _gla_body(qk_ref, v_ref, r_ref, sm_ref, wa_ref, ba_ref, nh_ref, s0_ref, o_ref, sfin_ref, s_scr):
    c_id = pl.program_id(1)
    n_rows = qk_ref.shape[0]
    items = [(b, h) for b in range(n_rows) for h in range(GLA_HEADS)]
    n_items = range(len(items))

    @pl.when(c_id == 0)
    def _():
        for b in range(n_rows):
            s_scr[b] = s0_ref[...]

    b_all = []
    for b in range(n_rows):
        la = jnp.dot(sm_ref[b], wa_ref[...], preferred_element_type=F32) + ba_ref[...]
        log_alpha = (jnp.minimum(la, 0.0) - jnp.log(1.0 + jnp.exp(-jnp.abs(la)))) * (1.0 / GLA_TAU)
        b_all.append(_cumsum_rows(log_alpha))

    r, c = _tri_incl(CHUNK)
    causal = r >= c
    scale = GLA_KEY_DIM ** -0.5
    mid = CHUNK // 2 - 1

    q = [qk_ref[b, :, h * GLA_KEY_DIM:(h + 1) * GLA_KEY_DIM].astype(F32) * scale for b, h in items]
    k = [qk_ref[b, :, GLA_QK + h * GLA_KEY_DIM:GLA_QK + (h + 1) * GLA_KEY_DIM].astype(F32) for b, h in items]
    v = [v_ref[b, :, h * GLA_VAL_DIM:(h + 1) * GLA_VAL_DIM].astype(F32) for b, h in items]
    bh = [b_all[b][:, h * GLA_KEY_DIM:(h + 1) * GLA_KEY_DIM] for b, h in items]
    bmid = [x[mid:mid + 1, :] for x in bh]
    blast = [x[CHUNK - 1:CHUNK, :] for x in bh]
    qs = [q[i] * jnp.exp(bh[i] - bmid[i]) for i in n_items]
    ks = [k[i] * jnp.exp(bmid[i] - bh[i]) for i in n_items]
    qd = [q[i] * jnp.exp(bh[i]) for i in n_items]
    kd = [k[i] * jnp.exp(blast[i] - bh[i]) for i in n_items]
    st = [s_scr[b, h] for b, h in items]
    att = [lax.dot_general(qs[i], ks[i], _NT, preferred_element_type=F32) for i in n_items]
    o_inter = [lax.dot_general(qd[i], st[i], _NT, preferred_element_type=F32) for i in n_items]
    s_add = [lax.dot_general(v[i], kd[i], _TN, preferred_element_type=F32) for i in n_items]
    att = [jnp.where(causal, a, 0.0) for a in att]
    o_intra = [jnp.dot(att[i], v[i], preferred_element_type=F32) for i in n_items]
    for i, (b, h) in enumerate(items):
        vc = h * GLA_VAL_DIM
        s_scr[b, h] = st[i] * jnp.exp(blast[i]) + s_add[i]
        gate = _silu(r_ref[b, :, vc:vc + GLA_VAL_DIM].astype(F32))
        o = o_inter[i] + o_intra[i]
        o_ref[b, :, vc:vc + GLA_VAL_DIM] = (_rms(o) * nh_ref[...] * gate).astype(o_ref.dtype)

    @pl.when(c_id == pl.num_programs(1) - 1)
    def _():
        sfin_ref[...] = s_scr[...]


def _gla_chunk(qk, v, rr, small, wa, ba, nh, s0):
    b, l, _ = qk.shape
    nc = l // CHUNK
    nr = _rows_per_step(b, 4)
    const2 = lambda bi, ci: (0, 0)
    blk = lambda w: pl.BlockSpec((nr, CHUNK, w), lambda bi, ci: (bi, ci, 0))
    return pl.pallas_call(
        _gla_body, grid=(b // nr, nc),
        in_specs=[
            blk(2 * GLA_QK), blk(GLA_V), blk(GLA_V), blk(SMALL_W),
            pl.BlockSpec(wa.shape, const2), pl.BlockSpec(ba.shape, const2), pl.BlockSpec(nh.shape, const2),
            pl.BlockSpec(s0.shape, lambda bi, ci: (0, 0, 0)),
        ],
        out_specs=[
            blk(GLA_V),
            pl.BlockSpec((nr, GLA_HEADS, GLA_VAL_DIM, GLA_KEY_DIM), lambda bi, ci: (bi, 0, 0, 0)),
        ],
        out_shape=[
            jax.ShapeDtypeStruct((b, l, GLA_V), BF16),
            jax.ShapeDtypeStruct((b, GLA_HEADS, GLA_VAL_DIM, GLA_KEY_DIM), F32),
        ],
        scratch_shapes=[pltpu.VMEM((nr, GLA_HEADS, GLA_VAL_DIM, GLA_KEY_DIM), F32)],
        compiler_params=pltpu.CompilerParams(dimension_semantics=("parallel", "arbitrary"),
                                             vmem_limit_bytes=VMEM_LIMIT),
        name="gla_chunk",
    )(qk, v, rr, small, wa, ba, nh, s0)


TOK_SUB = D_MODEL // LANES
assert TOK_SUB == SUBLANES


def _tiles_to_rows(ref, n):
    return jnp.concatenate([ref[pl.ds(s, n, stride=TOK_SUB), :] for s in range(TOK_SUB)], axis=1)


def _rows_to_tiles(ref, value):
    n = value.shape[0]
    for s in range(TOK_SUB):
        ref[pl.ds(s, n, stride=TOK_SUB), :] = value[:, s * LANES:(s + 1) * LANES]


def _tile_rows(tok, count=1):
    return pl.ds(pl.multiple_of(tok * TOK_SUB, TOK_SUB), count * TOK_SUB)


def _outproj_body(odn_ref, ogla_ref, gates_ref, x_ref, wd_ref, wg_ref, wo_ref, gn_ref, wr_ref, br_ref,
                  h2_ref, hn_ref, lg_ref):
    y_dn = jnp.dot(odn_ref[...], wd_ref[...], preferred_element_type=F32)
    y_gla = jnp.dot(ogla_ref[...], wg_ref[...], preferred_element_type=F32)
    gd = _sigmoid(gates_ref[:, 0:D_MODEL].astype(F32))
    gg = _sigmoid(gates_ref[:, D_MODEL:2 * D_MODEL].astype(F32))
    merged = gd * y_dn + gg * y_gla
    h2 = x_ref[...] + _bdot(merged, wo_ref[...])
    h2_ref[...] = h2
    hn = _rms(h2) * gn_ref[...]
    _rows_to_tiles(hn_ref, hn)
    wr = wr_ref[...]
    wr_hi, hn_hi = _bf16_part(wr), _bf16_part(hn)
    nt = lambda a, b: lax.dot_general(a, b, _NT, preferred_element_type=F32)
    lg_ref[...] = nt(wr_hi, hn_hi) + nt(wr_hi, hn - hn_hi) + nt(wr - wr_hi, hn_hi) + br_ref[...]


def _out_proj(o_dn, o_gla, gates, x2d, wd, wg, wo, gn, wr_t, br, rows):
    t = x2d.shape[0]
    row_blk = lambda w: pl.BlockSpec((rows, w), lambda i: (i, 0))
    const = lambda a: pl.BlockSpec(a.shape, lambda i: (0, 0))
    return pl.pallas_call(
        _outproj_body, grid=(t // rows,),
        in_specs=[row_blk(DN_V), row_blk(GLA_V), row_blk(2 * D_MODEL), row_blk(D_MODEL),
                  const(wd), const(wg), const(wo), const(gn), const(wr_t), const(br)],
        out_specs=[row_blk(D_MODEL), pl.BlockSpec((rows * TOK_SUB, LANES), lambda i: (i, 0)),
                   pl.BlockSpec((LANES, rows), lambda i: (0, i))],
        out_shape=[jax.ShapeDtypeStruct((t, D_MODEL), F32), jax.ShapeDtypeStruct((t * TOK_SUB, LANES), F32),
                   jax.ShapeDtypeStruct((LANES, t), F32)],
        compiler_params=pltpu.CompilerParams(dimension_semantics=("parallel",), vmem_limit_bytes=VMEM_LIMIT),
        name="out_proj",
    )(o_dn, o_gla, gates, x2d, wd, wg, wo, gn, wr_t, br)


ROUTE_SUB = 256


def _route_body(lg_ref, idx_ref, gate_ref, cnt_ref, carry):
    step = pl.program_id(0)

    @pl.when(step == 0)
    def _():
        carry[...] = jnp.zeros_like(carry)

    tt = lg_ref.shape[1]
    gl = lg_ref[0:N_GROUPS, :]
    gmax = jnp.max(gl, axis=0, keepdims=True)
    rid8 = lax.broadcasted_iota(jnp.int32, (N_GROUPS, tt), 0)
    gsel = jnp.min(jnp.where(gl == gmax, rid8, N_GROUPS), axis=0, keepdims=True)
    gw = 1.0 / jnp.sum(jnp.exp(gl - gmax), axis=0, keepdims=True)
    el = lg_ref[N_GROUPS:N_GROUPS + N_EXPERTS, :]
    rid = lax.broadcasted_iota(jnp.int32, (N_EXPERTS, tt), 0)
    neg = jnp.float32(-jnp.inf)
    ein = jnp.where((rid >> 3) == gsel, el, neg)
    t1 = jnp.max(ein, axis=0, keepdims=True)
    i1 = jnp.min(jnp.where(ein == t1, rid, N_EXPERTS), axis=0, keepdims=True)
    ein2 = jnp.where(rid == i1, neg, ein)
    t2 = jnp.max(ein2, axis=0, keepdims=True)
    i2 = jnp.min(jnp.where(ein2 == t2, rid, N_EXPERTS), axis=0, keepdims=True)
    e21 = jnp.exp(t2 - t1)
    den = 1.0 / (1.0 + e21)
    sel1 = rid == i1
    sel2 = rid == i2
    onehot = jnp.where(sel1 | sel2, 1.0, 0.0)

    ur, uc = _tri_incl(ROUTE_SUB)
    upper = (ur <= uc).astype(BF16)
    run = carry[...]
    r1_parts, r2_parts = [], []
    for s in range(tt // ROUTE_SUB):
        sl = slice(s * ROUTE_SUB, (s + 1) * ROUTE_SUB)
        oh = onehot[:, sl]
        incl = jnp.dot(oh.astype(BF16), upper, preferred_element_type=F32) + run
        excl = incl - oh
        r1_parts.append(jnp.sum(jnp.where(sel1[:, sl], excl, 0.0), axis=0, keepdims=True))
        r2_parts.append(jnp.sum(jnp.where(sel2[:, sl], excl, 0.0), axis=0, keepdims=True))
        run = jnp.broadcast_to(incl[:, ROUTE_SUB - 1:ROUTE_SUB], run.shape)
    carry[...] = run
    r1 = jnp.concatenate(r1_parts, axis=1) if len(r1_parts) > 1 else r1_parts[0]
    r2 = jnp.concatenate(r2_parts, axis=1) if len(r2_parts) > 1 else r2_parts[0]

    idx_ref[...] = jnp.zeros_like(idx_ref)
    idx_ref[0:1, :] = i1
    idx_ref[1:2, :] = i2
    idx_ref[2:3, :] = r1.astype(jnp.int32)
    idx_ref[3:4, :] = r2.astype(jnp.int32)
    gate_ref[...] = jnp.zeros_like(gate_ref)
    gate_ref[0:1, :] = den * gw
    gate_ref[1:2, :] = e21 * den * gw
    cnt_ref[...] = run[:, 0:LANES]


def _route(logits_t, lanes):
    t = logits_t.shape[1]
    return pl.pallas_call(
        _route_body, grid=(t // lanes,),
        in_specs=[pl.BlockSpec((LANES, lanes), lambda i: (0, i))],
        out_specs=[pl.BlockSpec((8, lanes), lambda i: (0, i)), pl.BlockSpec((8, lanes), lambda i: (0, i)),
                   pl.BlockSpec((N_EXPERTS, LANES), lambda i: (0, 0))],
        out_shape=[jax.ShapeDtypeStruct((8, t), jnp.int32), jax.ShapeDtypeStruct((8, t), F32),
                   jax.ShapeDtypeStruct((N_EXPERTS, LANES), F32)],
        scratch_shapes=[pltpu.VMEM((N_EXPERTS, ROUTE_SUB), F32)],
        compiler_params=pltpu.CompilerParams(dimension_semantics=("arbitrary",)),
        name="route",
    )(logits_t)


def _row_copy(src_ref, src_tok, dst_ref, dst_tok, sem):
    return pltpu.make_async_copy(src_ref.at[_tile_rows(src_tok)], dst_ref.at[_tile_rows(dst_tok)], sem)


ROW_UNROLL = 8


def _slots_body(pstart_ref, idx_ref, slot_ref):
    idx = idx_ref[...]
    base = jnp.zeros_like(idx)
    for e in range(N_EXPERTS):
        base = jnp.where(idx == e, pstart_ref[e], base)
    slot_ref[...] = jnp.zeros_like(idx)
    slot_ref[0:2, :] = base[0:2, :] + idx[2:4, :]


def _slots(pstart, idx, lanes):
    t = idx.shape[1]
    grid_spec = pltpu.PrefetchScalarGridSpec(
        num_scalar_prefetch=1, grid=(t // lanes,),
        in_specs=[pl.BlockSpec((8, lanes), lambda i, ps: (0, i))],
        out_specs=pl.BlockSpec((8, lanes), lambda i, ps: (0, i)),
    )
    return pl.pallas_call(
        _slots_body, grid_spec=grid_spec, out_shape=jax.ShapeDtypeStruct(idx.shape, jnp.int32),
        compiler_params=pltpu.CompilerParams(dimension_semantics=("parallel",)),
        name="slots",
    )(pstart, idx)


DISPATCH_BUFS = 3


def _dispatch_body(pstart_ref, size_ref, slot_ref, hn_hbm, xs_ref, zblk, hbuf, sem, load_sems, row_sems):
    step = pl.program_id(0)
    n_steps = pl.num_programs(0)
    n_tok = hbuf.shape[1] // TOK_SUB

    def load(tile):
        buf = tile % DISPATCH_BUFS
        return pltpu.make_async_copy(hn_hbm.at[_tile_rows(tile * n_tok, n_tok)], hbuf.at[buf], load_sems.at[buf])

    def wait_rows(parity):
        for k in range(2):
            pltpu.make_async_copy(hbuf.at[0], xs_ref.at[_tile_rows(0, n_tok)], row_sems.at[parity]).wait()

    @pl.when(step == 0)
    def _():
        load(0).start()

        @pl.when(n_steps > 1)
        def _():
            load(1).start()

        zblk[...] = jnp.zeros_like(zblk)
        parts = MOE_BLOCK // ZERO_ROWS

        def clear_copy(row0, part):
            return pltpu.make_async_copy(zblk, xs_ref.at[_tile_rows(row0 + part * ZERO_ROWS, ZERO_ROWS)], sem)

        def per_expert(e, n_rows):
            size = size_ref[e]
            full = size // MOE_BLOCK * MOE_BLOCK

            @pl.when(size != full)
            def _():
                for part in range(parts):
                    clear_copy(pstart_ref[e] + full, part).start()
                for part in range(parts):
                    clear_copy(pstart_ref[e] + full, part).wait()

            return n_rows + (size + MOE_BLOCK - 1) // MOE_BLOCK * MOE_BLOCK

        n_rows = lax.fori_loop(0, N_EXPERTS, per_expert, 0)

        def clear_unused(j, c):
            for part in range(parts):
                clear_copy(j * MOE_BLOCK, part).start()
            for part in range(parts):
                clear_copy(j * MOE_BLOCK, part).wait()
            return c

        lax.fori_loop(n_rows // MOE_BLOCK, xs_ref.shape[0] // (MOE_BLOCK * TOK_SUB), clear_unused, 0)

    load(step).wait()
    src = hbuf.at[step % DISPATCH_BUFS]
    row_sem = row_sems.at[step % 2]

    def issue(t, carry):
        for k in range(2):
            _row_copy(src, t, xs_ref, slot_ref[k, t], row_sem).start()
        return carry

    lax.fori_loop(0, n_tok, issue, 0, unroll=ROW_UNROLL)

    @pl.when(step > 0)
    def _():
        wait_rows((step - 1) % 2)

    @pl.when(step + 2 < n_steps)
    def _():
        load(step + 2).start()

    @pl.when(step == n_steps - 1)
    def _():
        wait_rows(step % 2)


def _dispatch(pstart, sizes, slots, hn, n_rows_out, tokens):
    t = hn.shape[0] // TOK_SUB
    grid_spec = pltpu.PrefetchScalarGridSpec(
        num_scalar_prefetch=2, grid=(t // tokens,),
        in_specs=[pl.BlockSpec((8, tokens), lambda i, ps, sz: (0, i), memory_space=pltpu.SMEM),
                  pl.BlockSpec(memory_space=pl.ANY)],
        out_specs=pl.BlockSpec(memory_space=pl.ANY),
        scratch_shapes=[pltpu.VMEM((ZERO_ROWS * TOK_SUB, LANES), F32),
                        pltpu.VMEM((DISPATCH_BUFS, tokens * TOK_SUB, LANES), F32),
                        pltpu.SemaphoreType.DMA(()), pltpu.SemaphoreType.DMA((DISPATCH_BUFS,)),
                        pltpu.SemaphoreType.DMA((2,))],
    )
    return pl.pallas_call(
        _dispatch_body, grid_spec=grid_spec,
        out_shape=jax.ShapeDtypeStruct((n_rows_out * TOK_SUB, LANES), F32),
        compiler_params=pltpu.CompilerParams(dimension_semantics=("arbitrary",)),
        name="dispatch",
    )(pstart, sizes, slots, hn)


def _experts_body(be_ref, nused_ref, par_ref, nxt_ref, nxt2_ref, xs_ref, w1_hbm, w3_hbm, w2_hbm, yb_ref,
                  w1f, w3f, w2f, w1b, w3b, w2b, sems):
    j = pl.program_id(0)
    used = j < nused_ref[0]
    expert = be_ref[j]
    slot = par_ref[j]
    new_expert = jnp.logical_or(j == 0, expert != be_ref[jnp.maximum(j - 1, 0)])

    def fetch(e, s):
        return [pltpu.make_async_copy(w_hbm.at[e], w_vmem.at[s], sems.at[s])
                for w_hbm, w_vmem in ((w1_hbm, w1f), (w3_hbm, w3f), (w2_hbm, w2f))]

    @pl.when(j == 0)
    def _():
        for copy in fetch(expert, slot):
            copy.start()

        @pl.when(nxt_ref[0] >= 0)
        def _():
            for copy in fetch(nxt_ref[0], 1 - slot):
                copy.start()

    @pl.when(jnp.logical_and(used, new_expert))
    def _():
        for copy in fetch(expert, slot):
            copy.wait()
        w1b[...] = w1f[slot].astype(BF16)
        w3b[...] = w3f[slot].astype(BF16)
        w2b[...] = w2f[slot].astype(BF16)

        @pl.when(nxt2_ref[j] >= 0)
        def _():
            for copy in fetch(nxt2_ref[j], slot):
                copy.start()

    @pl.when(used)
    def _():
        xb = _tiles_to_rows(xs_ref, MOE_BLOCK).astype(BF16)
        h1 = jnp.dot(xb, w1b[...], preferred_element_type=F32)
        h3 = jnp.dot(xb, w3b[...], preferred_element_type=F32)
        hid = (_silu(h1) * h3).astype(BF16)
        _rows_to_tiles(yb_ref, jnp.dot(hid, w2b[...], preferred_element_type=F32))

    @pl.when(jnp.logical_not(used))
    def _():
        yb_ref[...] = jnp.zeros_like(yb_ref)


def _experts(block_expert, n_used, sizes, xs, w1, w3, w2):
    p = xs.shape[0] // TOK_SUB
    nb = p // MOE_BLOCK
    owns = sizes > 0
    ordinal = jnp.cumsum(owns) - owns
    ids = jnp.arange(N_EXPERTS, dtype=jnp.int32)
    later = jnp.where((ids[None, :] > ids[:, None]) & owns[None, :], ids[None, :], N_EXPERTS)
    nxt_e = jnp.min(later, axis=1)
    nxt_e = jnp.where(nxt_e == N_EXPERTS, -1, nxt_e).astype(jnp.int32)
    nxt2_e = jnp.where(nxt_e >= 0, nxt_e[jnp.maximum(nxt_e, 0)], -1).astype(jnp.int32)
    parity = (ordinal[block_expert] % 2).astype(jnp.int32)
    nxt, nxt2 = nxt_e[block_expert], nxt2_e[block_expert]

    row_map = lambda j, be, nu, pa, n1, n2: (jnp.minimum(j, nu[0] - 1), 0)
    grid_spec = pltpu.PrefetchScalarGridSpec(
        num_scalar_prefetch=5, grid=(nb,),
        in_specs=[pl.BlockSpec((MOE_BLOCK * TOK_SUB, LANES), row_map),
                  pl.BlockSpec(memory_space=pl.ANY), pl.BlockSpec(memory_space=pl.ANY),
                  pl.BlockSpec(memory_space=pl.ANY)],
        out_specs=pl.BlockSpec((MOE_BLOCK * TOK_SUB, LANES), lambda j, be, nu, pa, n1, n2: (j, 0)),
        scratch_shapes=[pltpu.VMEM((2, D_MODEL, D_EXPERT), F32), pltpu.VMEM((2, D_MODEL, D_EXPERT), F32),
                        pltpu.VMEM((2, D_EXPERT, D_MODEL), F32),
                        pltpu.VMEM((D_MODEL, D_EXPERT), BF16), pltpu.VMEM((D_MODEL, D_EXPERT), BF16),
                        pltpu.VMEM((D_EXPERT, D_MODEL), BF16), pltpu.SemaphoreType.DMA((2,))],
    )
    return pl.pallas_call(
        _experts_body, grid_spec=grid_spec,
        out_shape=jax.ShapeDtypeStruct((p * TOK_SUB, LANES), F32),
        compiler_params=pltpu.CompilerParams(dimension_semantics=("arbitrary",), vmem_limit_bytes=VMEM_LIMIT),
        name="experts",
    )(block_expert, n_used, parity, nxt, nxt2, xs, w1, w3, w2)


def _combine_body(slot_ref, slot_next_ref, gate_ref, h2_ref, gf_ref, yb_ref, o_ref, ybuf, sems):
    n_tok = h2_ref.shape[0]
    step = pl.program_id(0)
    cur = step % 2

    def gather(slots, buf_id):
        def issue(t, carry):
            for k in range(2):
                pltpu.make_async_copy(yb_ref.at[_tile_rows(slots[k, t])], ybuf.at[buf_id, k, _tile_rows(t)],
                                      sems.at[buf_id]).start()
            return carry

        lax.fori_loop(0, n_tok, issue, 0, unroll=ROW_UNROLL)

    @pl.when(step == 0)
    def _():
        gather(slot_ref, 0)

    @pl.when(step + 1 < pl.num_programs(0))
    def _():
        gather(slot_next_ref, 1 - cur)

    for k in range(2):
        pltpu.make_async_copy(yb_ref.at[_tile_rows(0, n_tok)], ybuf.at[cur, k], sems.at[cur]).wait()
    y0 = _tiles_to_rows(ybuf.at[cur, 0], n_tok)
    y1 = _tiles_to_rows(ybuf.at[cur, 1], n_tok)
    y = h2_ref[...] + gate_ref[:, 0:1] * y0 + gate_ref[:, 1:2] * y1
    o_ref[...] = _rms(y) * gf_ref[...]


def _combine(slots, gate_t, h2, gf, yb, tokens):
    t = h2.shape[0]
    last = t // tokens - 1
    return pl.pallas_call(
        _combine_body, grid=(t // tokens,),
        in_specs=[pl.BlockSpec((8, tokens), lambda i: (0, i), memory_space=pltpu.SMEM),
                  pl.BlockSpec((8, tokens), lambda i: (0, jnp.minimum(i + 1, last)), memory_space=pltpu.SMEM),
                  pl.BlockSpec((tokens, 8), lambda i: (i, 0)),
                  pl.BlockSpec((tokens, D_MODEL), lambda i: (i, 0)),
                  pl.BlockSpec((1, D_MODEL), lambda i: (0, 0)),
                  pl.BlockSpec(memory_space=pl.ANY)],
        out_specs=pl.BlockSpec((tokens, D_MODEL), lambda i: (i, 0)),
        scratch_shapes=[pltpu.VMEM((2, 2, tokens * TOK_SUB, LANES), F32), pltpu.SemaphoreType.DMA((2,))],
        out_shape=jax.ShapeDtypeStruct((t, D_MODEL), F32),
        compiler_params=pltpu.CompilerParams(dimension_semantics=("arbitrary",)),
        name="combine",
    )(slots, slots, gate_t, h2, gf, yb)


def _pad_lanes(v, width=LANES):
    v = v.reshape(1, -1).astype(F32)
    return jnp.pad(v, ((0, 0), (0, width - v.shape[1])))


def _largest_tile(n, cap):
    t = cap
    while n % t:
        t //= 2
    return t


def kernel(x, meta_tokens, norm_mix, w_in, conv_dn, a_log, dt_bias, norm_head_dn, w_proj_dn, w_alpha, b_alpha,
           norm_head_gla, w_proj_gla, w_out, norm_ffn, w_group, b_group, w_router, b_router, w1, w3, w2, norm_final):
    assert norm_mix.shape[0] == 1, "single-layer block"
    bsz, seq, d = x.shape
    assert d == D_MODEL and seq % CHUNK == 0
    t = bsz * seq

    wi = w_in[0]
    offs = np.cumsum([0, DN_QK, DN_QK, DN_V, DN_V, DN_HEADS, DN_HEADS, GLA_QK, GLA_QK, GLA_V, GLA_V, GLA_RANK,
                      D_MODEL, D_MODEL]).tolist()
    seg = lambda i, j: wi[:, offs[i]:offs[j]]
    n_small = 2 * DN_HEADS + GLA_RANK
    w_all = jnp.concatenate([seg(0, 4), seg(6, 10), seg(11, 13), seg(4, 6), seg(10, 11),
                             jnp.zeros((D_MODEL, SMALL_W - n_small), F32)], axis=1).astype(BF16)
    g_mix = norm_mix[0].reshape(1, D_MODEL)

    conv_w = jnp.pad(conv_dn[0], ((0, 8 - CONV_WIDTH), (0, 0)))
    alog = _pad_lanes(a_log[0])
    dtb = _pad_lanes(dt_bias[0])
    nh_dn = norm_head_dn[0].reshape(1, DN_HEAD_DIM)
    lr_off = 2 * DN_HEADS
    wa = jnp.zeros((SMALL_W, GLA_QK), F32).at[lr_off:lr_off + GLA_RANK].set(w_alpha[0])
    ba = b_alpha[0].reshape(1, GLA_QK)
    nh_gla = norm_head_gla[0].reshape(1, GLA_VAL_DIM)

    def mixers(tokens2d, nb, hist, s0_dn, s0_gla, rows):
        tiles_per_seq = tokens2d.shape[0] // nb // rows
        qkv, z, qk_g, v_g, r_g, gates, small, tail = _in_proj(tokens2d, g_mix, w_all, conv_w, hist, rows, tiles_per_seq)
        r3 = lambda a: a.reshape(nb, -1, a.shape[-1])
        o_dn, s_dn = _dn_chunk(r3(qkv), r3(small), r3(z), s0_dn, alog, dtb, nh_dn)
        o_gla, s_gla = _gla_chunk(r3(qk_g), r3(v_g), r3(r_g), r3(small), wa, ba, nh_gla, s0_gla)
        return tail, o_dn, o_gla, gates, s_dn, s_gla

    meta_rows = jnp.pad(meta_tokens.astype(F32), ((CHUNK - N_META, 0), (0, 0)))
    zero_hist = jnp.zeros((HIST_ROWS, 3 * DN_QK), F32)
    zero_dn = jnp.zeros((DN_HEADS, DN_HEAD_DIM, DN_HEAD_DIM), F32)
    zero_gla = jnp.zeros((GLA_HEADS, GLA_VAL_DIM, GLA_KEY_DIM), F32)
    hist, _, _, _, s_dn_m, s_gla_m = mixers(meta_rows, 1, zero_hist, zero_dn, zero_gla, CHUNK)

    x2d = x.reshape(t, d)
    rows = _largest_tile(seq, 512)
    _, o_dn, o_gla, gates, _, _ = mixers(x2d, bsz, hist, s_dn_m[0], s_gla_m[0], rows)

    wr_t = jnp.concatenate([w_group[0], w_router[0]], axis=1).T
    wr_t = jnp.pad(wr_t, ((0, LANES - wr_t.shape[0]), (0, 0)))
    br = jnp.pad(jnp.concatenate([b_group[0], b_router[0]]), (0, LANES - N_GROUPS - N_EXPERTS)).reshape(LANES, 1)
    h2, hn, logits_t = _out_proj(
        o_dn.reshape(t, DN_V), o_gla.reshape(t, GLA_V), gates, x2d,
        w_proj_dn[0].astype(BF16), w_proj_gla[0].astype(BF16), w_out[0].astype(BF16),
        norm_ffn[0].reshape(1, D_MODEL), wr_t, br, rows)

    idx, gate, cnt = _route(logits_t, _largest_tile(t, 2048))

    sizes = cnt[:, 0].astype(jnp.int32)
    padded = (sizes + MOE_BLOCK - 1) // MOE_BLOCK * MOE_BLOCK
    pends = jnp.cumsum(padded)
    pstart = (pends - padded).astype(jnp.int32)
    n_blocks = (2 * t) // MOE_BLOCK + N_EXPERTS
    n_used = (pends[-1:] // MOE_BLOCK).astype(jnp.int32)
    block_row0 = jnp.arange(n_blocks, dtype=jnp.int32) * MOE_BLOCK
    block_expert = jnp.minimum(jnp.sum(pends[None, :] <= block_row0[:, None], axis=1), N_EXPERTS - 1).astype(jnp.int32)

    tok_tile = _largest_tile(t, 256)
    slots = _slots(pstart, idx, _largest_tile(t, 2048))
    xs = _dispatch(pstart, sizes, slots, hn, n_blocks * MOE_BLOCK, tok_tile)
    yb = _experts(block_expert, n_used, sizes, xs, w1[0], w3[0], w2[0])
    out = _combine(slots, gate.T, h2, norm_final.reshape(1, D_MODEL), yb, tok_tile)
    return out.reshape(bsz, seq, d)
```

```python
import functools

import jax
import jax.numpy as jnp
import numpy as np
from jax import lax
from jax.experimental import pallas as pl
from jax.experimental.pallas import tpu as pltpu

F32 = jnp.float32
BF16 = jnp.bfloat16

D_MODEL = 1024
CHUNK = 64
N_META = 16
EPS = 1e-6
DN_HEADS = 8
DN_HEAD_DIM = 128
DN_QK = DN_HEADS * DN_HEAD_DIM
DN_V = DN_HEADS * DN_HEAD_DIM
CONV_WIDTH = 4
GLA_HEADS = 4
GLA_KEY_DIM = 128
GLA_VAL_DIM = 256
GLA_QK = GLA_HEADS * GLA_KEY_DIM
GLA_V = GLA_HEADS * GLA_VAL_DIM
GLA_RANK = 16
GLA_TAU = 16.0
N_GROUPS = 8
EXPERTS_PER_GROUP = 8
N_EXPERTS = N_GROUPS * EXPERTS_PER_GROUP
D_EXPERT = 512
MOE_BLOCK = 512
ZERO_ROWS = 256

LANES = 128
SUBLANES = 8
SMALL_W = LANES
HIST_ROWS = 8
VMEM_LIMIT = 56 * 1024 * 1024

_NT = (((1,), (1,)), ((), ()))
_TN = (((0,), (0,)), ((), ()))


def _bdot(a, b):
    return jnp.dot(a.astype(BF16), b.astype(BF16), preferred_element_type=F32)


def _bdot_nt(a, b):
    return lax.dot_general(a.astype(BF16), b.astype(BF16), _NT, preferred_element_type=F32)


def _bdot_tn(a, b):
    return lax.dot_general(a.astype(BF16), b.astype(BF16), _TN, preferred_element_type=F32)


def _sigmoid(x):
    return 0.5 + 0.5 * jnp.tanh(0.5 * x)


def _silu(x):
    half = 0.5 * x
    return half + half * jnp.tanh(half)


def _softplus(x):
    return jnp.maximum(x, 0.0) + jnp.log(1.0 + jnp.exp(-jnp.abs(x)))


def _rms(x, eps=EPS):
    return x * lax.rsqrt(jnp.mean(x * x, axis=-1, keepdims=True) + eps)


def _tri_incl(n):
    r = lax.broadcasted_iota(jnp.int32, (n, n), 0)
    c = lax.broadcasted_iota(jnp.int32, (n, n), 1)
    return r, c


def _bf16_part(x):
    bits = pltpu.bitcast(x, jnp.uint32) & jnp.uint32(0xFFFF0000)
    return pltpu.bitcast(bits, F32)


def _cumsum_rows(x):
    r, c = _tri_incl(x.shape[0])
    tri = (r >= c).astype(F32)
    hi = _bf16_part(x)
    r1 = x - hi
    mid = _bf16_part(r1)
    lo = r1 - mid
    return (jnp.dot(tri, hi, preferred_element_type=F32) + jnp.dot(tri, mid, preferred_element_type=F32)
            + jnp.dot(tri, lo, preferred_element_type=F32))


IN_GROUPS = ((3 * DN_QK, BF16), (DN_V, BF16), (2 * GLA_QK, BF16), (GLA_V, BF16), (GLA_V, BF16),
             (2 * D_MODEL, BF16), (SMALL_W, F32))


CONV_COLS = 256
PLAIN_COLS = 256


def _inproj_body(tiles_per_seq, x_ref, g_ref, w_ref, cw_ref, hist_ref, *refs):
    o_refs, tail_ref, pre, carry = refs[:-3], refs[-3], refs[-2], refs[-1]
    rows = x_ref.shape[0]

    @pl.when(pl.program_id(0) % tiles_per_seq == 0)
    def _():
        carry[...] = hist_ref[...]

    x = x_ref[...]
    xb = (_rms(x) * g_ref[...]).astype(BF16)

    qkv_ref = o_refs[0]

    def conv_pass(c0):
        cols = slice(c0, c0 + CONV_COLS)
        pre[0:HIST_ROWS, :] = carry[:, cols]
        pre[HIST_ROWS:HIST_ROWS + rows, :] = jnp.dot(xb, w_ref[:, cols], preferred_element_type=F32)
        carry[:, cols] = pre[rows:rows + HIST_ROWS, :]
        acc = None
        for j in range(CONV_WIDTH):
            lo = HIST_ROWS - (CONV_WIDTH - 1) + j
            term = pre[lo:lo + rows, :] * cw_ref[j:j + 1, cols]
            acc = term if acc is None else acc + term
        qkv_ref[:, cols] = _silu(acc).astype(qkv_ref.dtype)

    def plain_pass(o_ref, c0, width, wcol):
        o_ref[:, c0:c0 + width] = jnp.dot(xb, w_ref[:, wcol + c0:wcol + c0 + width],
                                          preferred_element_type=F32).astype(o_ref.dtype)

    plain_tasks = []
    wcol = qkv_ref.shape[1]
    for o_ref in o_refs[1:]:
        width = o_ref.shape[1]
        step = min(width, PLAIN_COLS)
        plain_tasks += [functools.partial(plain_pass, o_ref, c0, step, wcol) for c0 in range(0, width, step)]
        wcol += width

    conv_tasks = [functools.partial(conv_pass, c0) for c0 in range(0, qkv_ref.shape[1], CONV_COLS)]
    n_conv, n_plain = len(conv_tasks), len(plain_tasks)
    for i, conv_task in enumerate(conv_tasks):
        for task in plain_tasks[i * n_plain // n_conv:(i + 1) * n_plain // n_conv]:
            task()
        conv_task()
    tail_ref[...] = carry[...]


def _in_proj(x2d, gain, w_all, conv_w, hist, rows, tiles_per_seq):
    t = x2d.shape[0]
    const = lambda a: pl.BlockSpec(a.shape, lambda i: (0, 0))
    in_specs = [pl.BlockSpec((rows, D_MODEL), lambda i: (i, 0)), const(gain),
                pl.BlockSpec(w_all.shape, lambda i: (0, 0), pipeline_mode=pl.Buffered(1)),
                const(conv_w), const(hist)]
    out_specs = [pl.BlockSpec((rows, w), lambda i: (i, 0)) for w, _ in IN_GROUPS] + [const(hist)]
    out_shape = [jax.ShapeDtypeStruct((t, w), dt) for w, dt in IN_GROUPS] + [jax.ShapeDtypeStruct(hist.shape, F32)]
    return pl.pallas_call(
        functools.partial(_inproj_body, tiles_per_seq), grid=(t // rows,),
        in_specs=in_specs, out_specs=out_specs, out_shape=out_shape,
        scratch_shapes=[pltpu.VMEM((HIST_ROWS + rows, CONV_COLS), F32), pltpu.VMEM(hist.shape, F32)],
        compiler_params=pltpu.CompilerParams(dimension_semantics=("arbitrary",), vmem_limit_bytes=VMEM_LIMIT),
        name="in_proj",
    )(x2d, gain, w_all, conv_w, hist)


def _block_diag2(m2):
    n = m2.shape[0]
    lane = lax.broadcasted_iota(jnp.int32, m2.shape, 1)
    return jnp.concatenate([jnp.where(lane < n, m2, 0.0), jnp.where(lane >= n, m2, 0.0)], axis=0)


def _unit_lower_inverse(mats):
    n = mats[0].shape[0]
    r = lax.broadcasted_iota(jnp.int32, (n, 2 * n), 0)
    c = lax.broadcasted_iota(jnp.int32, (n, 2 * n), 1) & (n - 1)
    eye = (r == c).astype(F32)
    pair = (r >> 1) == (c >> 1)
    ts = [eye - jnp.where(pair, a, 0.0) for a in mats]
    m = 2
    while m < n:
        sh = m.bit_length() - 1
        keep = ((r >> (sh + 1)) == (c >> (sh + 1))) & ((r >> sh) != (c >> sh))
        ams = [_block_diag2(jnp.where(keep, a, 0.0)) for a in mats]
        xs = [jnp.dot(t, am, preferred_element_type=F32) for t, am in zip(ts, ams)]
        ys = [jnp.dot(x, _block_diag2(t), preferred_element_type=F32) for x, t in zip(xs, ts)]
        ts = [t - y for t, y in zip(ts, ys)]
        m *= 2
    return ts


def _l2_normalise(xs, ones_bd, scale):
    out = []
    for p in range(0, len(xs), 2):
        sq = jnp.concatenate([xs[p] * xs[p], xs[p + 1] * xs[p + 1]], axis=1)
        ss = jnp.dot(sq, ones_bd, preferred_element_type=F32)
        inv = lax.rsqrt(ss + EPS) * scale
        out.append(xs[p] * inv[:, :DN_HEAD_DIM])
        out.append(xs[p + 1] * inv[:, DN_HEAD_DIM:])
    return out


DN_WAVE = 32


def _dn_body(qkv_ref, sm_ref, z_ref, s0_ref, alog_ref, dtb_ref, nh_ref, ones_ref, o_ref, sfin_ref, s_scr):
    c_id = pl.program_id(1)
    n_rows = qkv_ref.shape[0]
    all_items = [(b, h) for b in range(n_rows) for h in range(DN_HEADS)]

    @pl.when(c_id == 0)
    def _():
        for b in range(n_rows):
            s_scr[b] = s0_ref[...]

    def head_cols(b, col):
        return qkv_ref[b, :, col:col + DN_HEAD_DIM].astype(F32)

    gc_all, gc_t, beta_all = [], [], []
    for b in range(n_rows):
        sm = sm_ref[b]
        g_b = -jnp.exp(alog_ref[...]) * _softplus(sm + dtb_ref[...])
        beta_all.append(_sigmoid(sm))
        gc_b = _cumsum_rows(g_b)
        gc_all.append(gc_b)
        gc_t.append(jnp.concatenate([gc_b, jnp.zeros_like(gc_b)], axis=0).T)

    r2 = lax.broadcasted_iota(jnp.int32, (CHUNK, 2 * CHUNK), 0)
    lane2 = lax.broadcasted_iota(jnp.int32, (CHUNK, 2 * CHUNK), 1)
    c2 = lane2 & (CHUNK - 1)
    first = lane2 < CHUNK
    causal2 = r2 >= c2
    strict2 = r2 > c2
    scale = DN_HEAD_DIM ** -0.5
    ones_bd = ones_ref[...]

    def wave(items):
        n_items = range(len(items))
        q = [head_cols(b, h * DN_HEAD_DIM) for b, h in items]
        k = [head_cols(b, DN_QK + h * DN_HEAD_DIM) for b, h in items]
        v = [head_cols(b, 2 * DN_QK + h * DN_HEAD_DIM) for b, h in items]
        q = _l2_normalise(q, ones_bd, scale)
        k = _l2_normalise(k, ones_bd, 1.0)
        gcol = [gc_all[b][:, h:h + 1] for b, h in items]
        bcol = [beta_all[b][:, DN_HEADS + h:DN_HEADS + h + 1] for b, h in items]
        glast = [gc_all[b][CHUNK - 1:CHUNK, h:h + 1] for b, h in items]
        eg = [jnp.exp(gcol[i]) for i in n_items]

        pairs = range(0, len(items), 2)
        n_pairs = range(len(pairs))

        def diag2(x0, x1):
            return jnp.concatenate([jnp.concatenate([x0, jnp.zeros_like(x1)], axis=1),
                                    jnp.concatenate([jnp.zeros_like(x0), x1], axis=1)], axis=0)

        def lanes2(x0, x1):
            return jnp.where(first, x0, x1)

        def head_row(i):
            b, h = items[i]
            return gc_t[b][h:h + 1, :]

        k_bd = [diag2(k[p], k[p + 1]) for p in pairs]
        k_cat = [jnp.concatenate([k[p], k[p + 1]], axis=1) for p in pairs]
        q_cat = [jnp.concatenate([q[p], q[p + 1]], axis=1) for p in pairs]
        kk2 = [lax.dot_general(k_cat[j], k_bd[j], _NT, preferred_element_type=F32) for j in n_pairs]
        qk2 = [lax.dot_general(q_cat[j], k_bd[j], _NT, preferred_element_type=F32) for j in n_pairs]
        grow2 = [head_row(p) + pltpu.roll(head_row(p + 1), CHUNK, axis=1) for p in pairs]
        gcol2 = [lanes2(gcol[p], gcol[p + 1]) for p in pairs]
        bcol2 = [lanes2(bcol[p], bcol[p + 1]) for p in pairs]
        decay2 = [jnp.where(causal2, jnp.exp(jnp.where(causal2, gcol2[j] - grow2[j], 0.0)), 0.0) for j in n_pairs]
        a_mat2 = [jnp.where(strict2, bcol2[j] * kk2[j] * decay2[j], 0.0) for j in n_pairs]
        qk2 = [jnp.where(causal2, qk2[j] * decay2[j], 0.0) for j in n_pairs]
        t_inv2 = _unit_lower_inverse(a_mat2)
        rhs = [jnp.concatenate([v[i] * bcol[i], k[i] * (bcol[i] * eg[i])], axis=1) for i in n_items]
        sol2 = [jnp.dot(t_inv2[j], diag2(rhs[p], rhs[p + 1]), preferred_element_type=F32)
                for j, p in enumerate(pairs)]
        sol = [sol2[i // 2][:, (i % 2) * 2 * DN_HEAD_DIM:(i % 2 + 1) * 2 * DN_HEAD_DIM] for i in n_items]
        s_old = [s_scr[b, h] for b, h in items]
        qe = [q[i] * eg[i] for i in n_items]
        k_dec = [k[i] * jnp.exp(glast[i] - gcol[i]) for i in n_items]
        ws = [jnp.dot(sol[i][:, DN_HEAD_DIM:], s_old[i], preferred_element_type=F32) for i in n_items]
        o_inter = [jnp.dot(qe[i], s_old[i], preferred_element_type=F32) for i in n_items]
        v_new = [sol[i][:, :DN_HEAD_DIM] - ws[i] for i in n_items]
        o_intra2 = [jnp.dot(qk2[j], diag2(v_new[p], v_new[p + 1]), preferred_element_type=F32)
                    for j, p in enumerate(pairs)]
        o_intra = [o_intra2[i // 2][:, (i % 2) * DN_HEAD_DIM:(i % 2 + 1) * DN_HEAD_DIM] for i in n_items]
        s_add = [lax.dot_general(k_dec[i], v_new[i], _TN, preferred_element_type=F32) for i in n_items]
        for i, (b, h) in enumerate(items):
            col = h * DN_HEAD_DIM
            s_scr[b, h] = s_old[i] * jnp.exp(glast[i]) + s_add[i]
            zed = z_ref[b, :, col:col + DN_HEAD_DIM].astype(F32)
            o = o_inter[i] + o_intra[i]
            o_ref[b, :, col:col + DN_HEAD_DIM] = (_rms(o) * nh_ref[...] * _silu(zed)).astype(o_ref.dtype)

    for w0 in range(0, len(all_items), DN_WAVE):
        wave(all_items[w0:w0 + DN_WAVE])

    @pl.when(c_id == pl.num_programs(1) - 1)
    def _():
        sfin_ref[...] = s_scr[...]


def _rows_per_step(b, want):
    while b % want:
        want //= 2
    return want


def _dn_chunk(qkv, small, z, s0, alog, dtb, nh):
    b, l, _ = qkv.shape
    lane_head = np.arange(2 * DN_HEAD_DIM) // DN_HEAD_DIM
    ones_bd = jnp.asarray(lane_head[:, None] == lane_head[None, :], F32)
    nc = l // CHUNK
    nr = _rows_per_step(b, 4)
    const2 = lambda bi, ci: (0, 0)
    blk = lambda w: pl.BlockSpec((nr, CHUNK, w), lambda bi, ci: (bi, ci, 0))
    return pl.pallas_call(
        _dn_body, grid=(b // nr, nc),
        in_specs=[
            blk(3 * DN_QK), blk(SMALL_W), blk(DN_V),
            pl.BlockSpec(s0.shape, lambda bi, ci: (0, 0, 0)),
            pl.BlockSpec(alog.shape, const2),
            pl.BlockSpec(dtb.shape, const2),
            pl.BlockSpec(nh.shape, const2),
            pl.BlockSpec(ones_bd.shape, const2),
        ],
        out_specs=[
            blk(DN_V),
            pl.BlockSpec((nr, DN_HEADS, DN_HEAD_DIM, DN_HEAD_DIM), lambda bi, ci: (bi, 0, 0, 0)),
        ],
        out_shape=[
            jax.ShapeDtypeStruct((b, l, DN_V), BF16),
            jax.ShapeDtypeStruct((b, DN_HEADS, DN_HEAD_DIM, DN_HEAD_DIM), F32),
        ],
        scratch_shapes=[pltpu.VMEM((nr, DN_HEADS, DN_HEAD_DIM, DN_HEAD_DIM), F32)],
        compiler_params=pltpu.CompilerParams(dimension_semantics=("parallel", "arbitrary"),
                                             vmem_limit_bytes=VMEM_LIMIT),
        name="dn_chunk",
    )(qkv, small, z, s0, alog, dtb, nh, ones_bd)


def _gla_body(qk_ref, v_ref, r_ref, sm_ref, wa_ref, ba_ref, nh_ref, s0_ref, o_ref, sfin_ref, s_scr):
    c_id = pl.program_id(1)
    n_rows = qk_ref.shape[0]
    items = [(b, h) for b in range(n_rows) for h in range(GLA_HEADS)]
    n_items = range(len(items))

    @pl.when(c_id == 0)
    def _():
        for b in range(n_rows):
            s_scr[b] = s0_ref[...]

    b_all = []
    for b in range(n_rows):
        la = jnp.dot(sm_ref[b], wa_ref[...], preferred_element_type=F32) + ba_ref[...]
        log_alpha = (jnp.minimum(la, 0.0) - jnp.log(1.0 + jnp.exp(-jnp.abs(la)))) * (1.0 / GLA_TAU)
        b_all.append(_cumsum_rows(log_alpha))

    r, c = _tri_incl(CHUNK)
    causal = r >= c
    scale = GLA_KEY_DIM ** -0.5
    mid = CHUNK // 2 - 1

    q = [qk_ref[b, :, h * GLA_KEY_DIM:(h + 1) * GLA_KEY_DIM].astype(F32) * scale for b, h in items]
    k = [qk_ref[b, :, GLA_QK + h * GLA_KEY_DIM:GLA_QK + (h + 1) * GLA_KEY_DIM].astype(F32) for b, h in items]
    v = [v_ref[b, :, h * GLA_VAL_DIM:(h + 1) * GLA_VAL_DIM].astype(F32) for b, h in items]
    bh = [b_all[b][:, h * GLA_KEY_DIM:(h + 1) * GLA_KEY_DIM] for b, h in items]
    bmid = [x[mid:mid + 1, :] for x in bh]
    blast = [x[CHUNK - 1:CHUNK, :] for x in bh]
    qs = [q[i] * jnp.exp(bh[i] - bmid[i]) for i in n_items]
    ks = [k[i] * jnp.exp(bmid[i] - bh[i]) for i in n_items]
    qd = [q[i] * jnp.exp(bh[i]) for i in n_items]
    kd = [k[i] * jnp.exp(blast[i] - bh[i]) for i in n_items]
    st = [s_scr[b, h] for b, h in items]
    att = [lax.dot_general(qs[i], ks[i], _NT, preferred_element_type=F32) for i in n_items]
    o_inter = [lax.dot_general(qd[i], st[i], _NT, preferred_element_type=F32) for i in n_items]
    s_add = [lax.dot_general(v[i], kd[i], _TN, preferred_element_type=F32) for i in n_items]
    att = [jnp.where(causal, a, 0.0) for a in att]
    o_intra = [jnp.dot(att[i], v[i], preferred_element_type=F32) for i in n_items]
    for i, (b, h) in enumerate(items):
        vc = h * GLA_VAL_DIM
        s_scr[b, h] = st[i] * jnp.exp(blast[i]) + s_add[i]
        gate = _silu(r_ref[b, :, vc:vc + GLA_VAL_DIM].astype(F32))
        o = o_inter[i] + o_intra[i]
        o_ref[b, :, vc:vc + GLA_VAL_DIM] = (_rms(o) * nh_ref[...] * gate).astype(o_ref.dtype)

    @pl.when(c_id == pl.num_programs(1) - 1)
    def _():
        sfin_ref[...] = s_scr[...]


def _gla_chunk(qk, v, rr, small, wa, ba, nh, s0):
    b, l, _ = qk.shape
    nc = l // CHUNK
    nr = _rows_per_step(b, 4)
    const2 = lambda bi, ci: (0, 0)
    blk = lambda w: pl.BlockSpec((nr, CHUNK, w), lambda bi, ci: (bi, ci, 0))
    return pl.pallas_call(
        _gla_body, grid=(b // nr, nc),
        in_specs=[
            blk(2 * GLA_QK), blk(GLA_V), blk(GLA_V), blk(SMALL_W),
            pl.BlockSpec(wa.shape, const2), pl.BlockSpec(ba.shape, const2), pl.BlockSpec(nh.shape, const2),
            pl.BlockSpec(s0.shape, lambda bi, ci: (0, 0, 0)),
        ],
        out_specs=[
            blk(GLA_V),
            pl.BlockSpec((nr, GLA_HEADS, GLA_VAL_DIM, GLA_KEY_DIM), lambda bi, ci: (bi, 0, 0, 0)),
        ],
        out_shape=[
            jax.ShapeDtypeStruct((b, l, GLA_V), BF16),
            jax.ShapeDtypeStruct((b, GLA_HEADS, GLA_VAL_DIM, GLA_KEY_DIM), F32),
        ],
        scratch_shapes=[pltpu.VMEM((nr, GLA_HEADS, GLA_VAL_DIM, GLA_KEY_DIM), F32)],
        compiler_params=pltpu.CompilerParams(dimension_semantics=("parallel", "arbitrary"),
                                             vmem_limit_bytes=VMEM_LIMIT),
        name="gla_chunk",
    )(qk, v, rr, small, wa, ba, nh, s0)


TOK_SUB = D_MODEL // LANES
assert TOK_SUB == SUBLANES


def _tiles_to_rows(ref, n):
    return jnp.concatenate([ref[pl.ds(s, n, stride=TOK_SUB), :] for s in range(TOK_SUB)], axis=1)


def _rows_to_tiles(ref, value):
    n = value.shape[0]
    for s in range(TOK_SUB):
        ref[pl.ds(s, n, stride=TOK_SUB), :] = value[:, s * LANES:(s + 1) * LANES]


def _tile_rows(tok, count=1):
    return pl.ds(pl.multiple_of(tok * TOK_SUB, TOK_SUB), count * TOK_SUB)


def _outproj_body(odn_ref, ogla_ref, gates_ref, x_ref, wd_ref, wg_ref, wo_ref, gn_ref, wr_ref, br_ref,
                  h2_ref, hn_ref, lg_ref):
    y_dn = jnp.dot(odn_ref[...], wd_ref[...], preferred_element_type=F32)
    y_gla = jnp.dot(ogla_ref[...], wg_ref[...], preferred_element_type=F32)
    gd = _sigmoid(gates_ref[:, 0:D_MODEL].astype(F32))
    gg = _sigmoid(gates_ref[:, D_MODEL:2 * D_MODEL].astype(F32))
    merged = gd * y_dn + gg * y_gla
    h2 = x_ref[...] + _bdot(merged, wo_ref[...])
    h2_ref[...] = h2
    hn = _rms(h2) * gn_ref[...]
    _rows_to_tiles(hn_ref, hn)
    wr = wr_ref[...]
    wr_hi, hn_hi = _bf16_part(wr), _bf16_part(hn)
    nt = lambda a, b: lax.dot_general(a, b, _NT, preferred_element_type=F32)
    lg_ref[...] = nt(wr_hi, hn_hi) + nt(wr_hi, hn - hn_hi) + nt(wr - wr_hi, hn_hi) + br_ref[...]


def _out_proj(o_dn, o_gla, gates, x2d, wd, wg, wo, gn, wr_t, br, rows):
    t = x2d.shape[0]
    row_blk = lambda w: pl.BlockSpec((rows, w), lambda i: (i, 0))
    const = lambda a: pl.BlockSpec(a.shape, lambda i: (0, 0))
    return pl.pallas_call(
        _outproj_body, grid=(t // rows,),
        in_specs=[row_blk(DN_V), row_blk(GLA_V), row_blk(2 * D_MODEL), row_blk(D_MODEL),
                  const(wd), const(wg), const(wo), const(gn), const(wr_t), const(br)],
        out_specs=[row_blk(D_MODEL), pl.BlockSpec((rows * TOK_SUB, LANES), lambda i: (i, 0)),
                   pl.BlockSpec((LANES, rows), lambda i: (0, i))],
        out_shape=[jax.ShapeDtypeStruct((t, D_MODEL), F32), jax.ShapeDtypeStruct((t * TOK_SUB, LANES), F32),
                   jax.ShapeDtypeStruct((LANES, t), F32)],
        compiler_params=pltpu.CompilerParams(dimension_semantics=("parallel",), vmem_limit_bytes=VMEM_LIMIT),
        name="out_proj",
    )(o_dn, o_gla, gates, x2d, wd, wg, wo, gn, wr_t, br)


ROUTE_SUB = 256


def _route_body(lg_ref, idx_ref, gate_ref, cnt_ref, carry):
    step = pl.program_id(0)

    @pl.when(step == 0)
    def _():
        carry[...] = jnp.zeros_like(carry)

    tt = lg_ref.shape[1]
    gl = lg_ref[0:N_GROUPS, :]
    gmax = jnp.max(gl, axis=0, keepdims=True)
    rid8 = lax.broadcasted_iota(jnp.int32, (N_GROUPS, tt), 0)
    gsel = jnp.min(jnp.where(gl == gmax, rid8, N_GROUPS), axis=0, keepdims=True)
    gw = 1.0 / jnp.sum(jnp.exp(gl - gmax), axis=0, keepdims=True)
    el = lg_ref[N_GROUPS:N_GROUPS + N_EXPERTS, :]
    rid = lax.broadcasted_iota(jnp.int32, (N_EXPERTS, tt), 0)
    neg = jnp.float32(-jnp.inf)
    ein = jnp.where((rid >> 3) == gsel, el, neg)
    t1 = jnp.max(ein, axis=0, keepdims=True)
    i1 = jnp.min(jnp.where(ein == t1, rid, N_EXPERTS), axis=0, keepdims=True)
    ein2 = jnp.where(rid == i1, neg, ein)
    t2 = jnp.max(ein2, axis=0, keepdims=True)
    i2 = jnp.min(jnp.where(ein2 == t2, rid, N_EXPERTS), axis=0, keepdims=True)
    e21 = jnp.exp(t2 - t1)
    den = 1.0 / (1.0 + e21)
    sel1 = rid == i1
    sel2 = rid == i2
    onehot = jnp.where(sel1 | sel2, 1.0, 0.0)

    ur, uc = _tri_incl(ROUTE_SUB)
    upper = (ur <= uc).astype(BF16)
    run = carry[...]
    r1_parts, r2_parts = [], []
    for s in range(tt // ROUTE_SUB):
        sl = slice(s * ROUTE_SUB, (s + 1) * ROUTE_SUB)
        oh = onehot[:, sl]
        incl = jnp.dot(oh.astype(BF16), upper, preferred_element_type=F32) + run
        excl = incl - oh
        r1_parts.append(jnp.sum(jnp.where(sel1[:, sl], excl, 0.0), axis=0, keepdims=True))
        r2_parts.append(jnp.sum(jnp.where(sel2[:, sl], excl, 0.0), axis=0, keepdims=True))
        run = jnp.broadcast_to(incl[:, ROUTE_SUB - 1:ROUTE_SUB], run.shape)
    carry[...] = run
    r1 = jnp.concatenate(r1_parts, axis=1) if len(r1_parts) > 1 else r1_parts[0]
    r2 = jnp.concatenate(r2_parts, axis=1) if len(r2_parts) > 1 else r2_parts[0]

    idx_ref[...] = jnp.zeros_like(idx_ref)
    idx_ref[0:1, :] = i1
    idx_ref[1:2, :] = i2
    idx_ref[2:3, :] = r1.astype(jnp.int32)
    idx_ref[3:4, :] = r2.astype(jnp.int32)
    gate_ref[...] = jnp.zeros_like(gate_ref)
    gate_ref[0:1, :] = den * gw
    gate_ref[1:2, :] = e21 * den * gw
    cnt_ref[...] = run[:, 0:LANES]


def _route(logits_t, lanes):
    t = logits_t.shape[1]
    return pl.pallas_call(
        _route_body, grid=(t // lanes,),
        in_specs=[pl.BlockSpec((LANES, lanes), lambda i: (0, i))],
        out_specs=[pl.BlockSpec((8, lanes), lambda i: (0, i)), pl.BlockSpec((8, lanes), lambda i: (0, i)),
                   pl.BlockSpec((N_EXPERTS, LANES), lambda i: (0, 0))],
        out_shape=[jax.ShapeDtypeStruct((8, t), jnp.int32), jax.ShapeDtypeStruct((8, t), F32),
                   jax.ShapeDtypeStruct((N_EXPERTS, LANES), F32)],
        scratch_shapes=[pltpu.VMEM((N_EXPERTS, ROUTE_SUB), F32)],
        compiler_params=pltpu.CompilerParams(dimension_semantics=("arbitrary",)),
        name="route",
    )(logits_t)


def _row_copy(src_ref, src_tok, dst_ref, dst_tok, sem):
    return pltpu.make_async_copy(src_ref.at[_tile_rows(src_tok)], dst_ref.at[_tile_rows(dst_tok)], sem)


ROW_UNROLL = 8


def _slots_body(pstart_ref, idx_ref, slot_ref):
    idx = idx_ref[...]
    base = jnp.zeros_like(idx)
    for e in range(N_EXPERTS):
        base = jnp.where(idx == e, pstart_ref[e], base)
    slot_ref[...] = jnp.zeros_like(idx)
    slot_ref[0:2, :] = base[0:2, :] + idx[2:4, :]


def _slots(pstart, idx, lanes):
    t = idx.shape[1]
    grid_spec = pltpu.PrefetchScalarGridSpec(
        num_scalar_prefetch=1, grid=(t // lanes,),
        in_specs=[pl.BlockSpec((8, lanes), lambda i, ps: (0, i))],
        out_specs=pl.BlockSpec((8, lanes), lambda i, ps: (0, i)),
    )
    return pl.pallas_call(
        _slots_body, grid_spec=grid_spec, out_shape=jax.ShapeDtypeStruct(idx.shape, jnp.int32),
        compiler_params=pltpu.CompilerParams(dimension_semantics=("parallel",)),
        name="slots",
    )(pstart, idx)


DISPATCH_BUFS = 3


def _dispatch_body(pstart_ref, size_ref, slot_ref, hn_hbm, xs_ref, zblk, hbuf, sem, load_sems, row_sems):
    step = pl.program_id(0)
    n_steps = pl.num_programs(0)
    n_tok = hbuf.shape[1] // TOK_SUB

    def load(tile):
        buf = tile % DISPATCH_BUFS
        return pltpu.make_async_copy(hn_hbm.at[_tile_rows(tile * n_tok, n_tok)], hbuf.at[buf], load_sems.at[buf])

    def wait_rows(parity):
        for k in range(2):
            pltpu.make_async_copy(hbuf.at[0], xs_ref.at[_tile_rows(0, n_tok)], row_sems.at[parity]).wait()

    @pl.when(step == 0)
    def _():
        load(0).start()

        @pl.when(n_steps > 1)
        def _():
            load(1).start()

        zblk[...] = jnp.zeros_like(zblk)
        parts = MOE_BLOCK // ZERO_ROWS

        def clear_copy(row0, part):
            return pltpu.make_async_copy(zblk, xs_ref.at[_tile_rows(row0 + part * ZERO_ROWS, ZERO_ROWS)], sem)

        def per_expert(e, n_rows):
            size = size_ref[e]
            full = size // MOE_BLOCK * MOE_BLOCK

            @pl.when(size != full)
            def _():
                for part in range(parts):
                    clear_copy(pstart_ref[e] + full, part).start()
                for part in range(parts):
                    clear_copy(pstart_ref[e] + full, part).wait()

            return n_rows + (size + MOE_BLOCK - 1) // MOE_BLOCK * MOE_BLOCK

        n_rows = lax.fori_loop(0, N_EXPERTS, per_expert, 0)

        def clear_unused(j, c):
            for part in range(parts):
                clear_copy(j * MOE_BLOCK, part).start()
            for part in range(parts):
                clear_copy(j * MOE_BLOCK, part).wait()
            return c

        lax.fori_loop(n_rows // MOE_BLOCK, xs_ref.shape[0] // (MOE_BLOCK * TOK_SUB), clear_unused, 0)

    load(step).wait()
    src = hbuf.at[step % DISPATCH_BUFS]
    row_sem = row_sems.at[step % 2]

    def issue(t, carry):
        for k in range(2):
            _row_copy(src, t, xs_ref, slot_ref[k, t], row_sem).start()
        return carry

    lax.fori_loop(0, n_tok, issue, 0, unroll=ROW_UNROLL)

    @pl.when(step > 0)
    def _():
        wait_rows((step - 1) % 2)

    @pl.when(step + 2 < n_steps)
    def _():
        load(step + 2).start()

    @pl.when(step == n_steps - 1)
    def _():
        wait_rows(step % 2)


def _dispatch(pstart, sizes, slots, hn, n_rows_out, tokens):
    t = hn.shape[0] // TOK_SUB
    grid_spec = pltpu.PrefetchScalarGridSpec(
        num_scalar_prefetch=2, grid=(t // tokens,),
        in_specs=[pl.BlockSpec((8, tokens), lambda i, ps, sz: (0, i), memory_space=pltpu.SMEM),
                  pl.BlockSpec(memory_space=pl.ANY)],
        out_specs=pl.BlockSpec(memory_space=pl.ANY),
        scratch_shapes=[pltpu.VMEM((ZERO_ROWS * TOK_SUB, LANES), F32),
                        pltpu.VMEM((DISPATCH_BUFS, tokens * TOK_SUB, LANES), F32),
                        pltpu.SemaphoreType.DMA(()), pltpu.SemaphoreType.DMA((DISPATCH_BUFS,)),
                        pltpu.SemaphoreType.DMA((2,))],
    )
    return pl.pallas_call(
        _dispatch_body, grid_spec=grid_spec,
        out_shape=jax.ShapeDtypeStruct((n_rows_out * TOK_SUB, LANES), F32),
        compiler_params=pltpu.CompilerParams(dimension_semantics=("arbitrary",)),
        name="dispatch",
    )(pstart, sizes, slots, hn)


def _experts_body(be_ref, nused_ref, par_ref, nxt_ref, nxt2_ref, xs_ref, w1_hbm, w3_hbm, w2_hbm, yb_ref,
                  w1f, w3f, w2f, w1b, w3b, w2b, sems):
    j = pl.program_id(0)
    used = j < nused_ref[0]
    expert = be_ref[j]
    slot = par_ref[j]
    new_expert = jnp.logical_or(j == 0, expert != be_ref[jnp.maximum(j - 1, 0)])

    def fetch(e, s):
        return [pltpu.make_async_copy(w_hbm.at[e], w_vmem.at[s], sems.at[s])
                for w_hbm, w_vmem in ((w1_hbm, w1f), (w3_hbm, w3f), (w2_hbm, w2f))]

    @pl.when(j == 0)
    def _():
        for copy in fetch(expert, slot):
            copy.start()

        @pl.when(nxt_ref[0] >= 0)
        def _():
            for copy in fetch(nxt_ref[0], 1 - slot):
                copy.start()

    @pl.when(jnp.logical_and(used, new_expert))
    def _():
        for copy in fetch(expert, slot):
            copy.wait()
        w1b[...] = w1f[slot].astype(BF16)
        w3b[...] = w3f[slot].astype(BF16)
        w2b[...] = w2f[slot].astype(BF16)

        @pl.when(nxt2_ref[j] >= 0)
        def _():
            for copy in fetch(nxt2_ref[j], slot):
                copy.start()

    @pl.when(used)
    def _():
        xb = _tiles_to_rows(xs_ref, MOE_BLOCK).astype(BF16)
        h1 = jnp.dot(xb, w1b[...], preferred_element_type=F32)
        h3 = jnp.dot(xb, w3b[...], preferred_element_type=F32)
        hid = (_silu(h1) * h3).astype(BF16)
        _rows_to_tiles(yb_ref, jnp.dot(hid, w2b[...], preferred_element_type=F32))

    @pl.when(jnp.logical_not(used))
    def _():
        yb_ref[...] = jnp.zeros_like(yb_ref)


def _experts(block_expert, n_used, sizes, xs, w1, w3, w2):
    p = xs.shape[0] // TOK_SUB
    nb = p // MOE_BLOCK
    owns = sizes > 0
    ordinal = jnp.cumsum(owns) - owns
    ids = jnp.arange(N_EXPERTS, dtype=jnp.int32)
    later = jnp.where((ids[None, :] > ids[:, None]) & owns[None, :], ids[None, :], N_EXPERTS)
    nxt_e = jnp.min(later, axis=1)
    nxt_e = jnp.where(nxt_e == N_EXPERTS, -1, nxt_e).astype(jnp.int32)

    def lookup(table, index):
        hit = index[:, None] == ids[None, :]
        return jnp.sum(jnp.where(hit, table[None, :] + 1, 0), axis=1).astype(jnp.int32) - 1

    nxt2_e = lookup(nxt_e, nxt_e)
    parity = lookup(ordinal.astype(jnp.int32), block_expert) % 2
    nxt, nxt2 = lookup(nxt_e, block_expert), lookup(nxt2_e, block_expert)

    row_map = lambda j, be, nu, pa, n1, n2: (jnp.minimum(j, nu[0] - 1), 0)
    grid_spec = pltpu.PrefetchScalarGridSpec(
        num_scalar_prefetch=5, grid=(nb,),
        in_specs=[pl.BlockSpec((MOE_BLOCK * TOK_SUB, LANES), row_map),
                  pl.BlockSpec(memory_space=pl.ANY), pl.BlockSpec(memory_space=pl.ANY),
                  pl.BlockSpec(memory_space=pl.ANY)],
        out_specs=pl.BlockSpec((MOE_BLOCK * TOK_SUB, LANES), lambda j, be, nu, pa, n1, n2: (j, 0)),
        scratch_shapes=[pltpu.VMEM((2, D_MODEL, D_EXPERT), F32), pltpu.VMEM((2, D_MODEL, D_EXPERT), F32),
                        pltpu.VMEM((2, D_EXPERT, D_MODEL), F32),
                        pltpu.VMEM((D_MODEL, D_EXPERT), BF16), pltpu.VMEM((D_MODEL, D_EXPERT), BF16),
                        pltpu.VMEM((D_EXPERT, D_MODEL), BF16), pltpu.SemaphoreType.DMA((2,))],
    )
    return pl.pallas_call(
        _experts_body, grid_spec=grid_spec,
        out_shape=jax.ShapeDtypeStruct((p * TOK_SUB, LANES), F32),
        compiler_params=pltpu.CompilerParams(dimension_semantics=("arbitrary",), vmem_limit_bytes=VMEM_LIMIT),
        name="experts",
    )(block_expert, n_used, parity, nxt, nxt2, xs, w1, w3, w2)


def _combine_body(slot_ref, slot_next_ref, gate_ref, h2_ref, gf_ref, yb_ref, o_ref, ybuf, sems):
    n_tok = h2_ref.shape[0]
    step = pl.program_id(0)
    cur = step % 2

    def gather(slots, buf_id):
        def issue(t, carry):
            for k in range(2):
                pltpu.make_async_copy(yb_ref.at[_tile_rows(slots[k, t])], ybuf.at[buf_id, k, _tile_rows(t)],
                                      sems.at[buf_id]).start()
            return carry

        lax.fori_loop(0, n_tok, issue, 0, unroll=ROW_UNROLL)

    @pl.when(step == 0)
    def _():
        gather(slot_ref, 0)

    @pl.when(step + 1 < pl.num_programs(0))
    def _():
        gather(slot_next_ref, 1 - cur)

    for k in range(2):
        pltpu.make_async_copy(yb_ref.at[_tile_rows(0, n_tok)], ybuf.at[cur, k], sems.at[cur]).wait()
    y0 = _tiles_to_rows(ybuf.at[cur, 0], n_tok)
    y1 = _tiles_to_rows(ybuf.at[cur, 1], n_tok)
    y = h2_ref[...] + gate_ref[:, 0:1] * y0 + gate_ref[:, 1:2] * y1
    o_ref[...] = _rms(y) * gf_ref[...]


def _combine(slots, gate_t, h2, gf, yb, tokens):
    t = h2.shape[0]
    last = t // tokens - 1
    return pl.pallas_call(
        _combine_body, grid=(t // tokens,),
        in_specs=[pl.BlockSpec((8, tokens), lambda i: (0, i), memory_space=pltpu.SMEM),
                  pl.BlockSpec((8, tokens), lambda i: (0, jnp.minimum(i + 1, last)), memory_space=pltpu.SMEM),
                  pl.BlockSpec((tokens, 8), lambda i: (i, 0)),
                  pl.BlockSpec((tokens, D_MODEL), lambda i: (i, 0)),
                  pl.BlockSpec((1, D_MODEL), lambda i: (0, 0)),
                  pl.BlockSpec(memory_space=pl.ANY)],
        out_specs=pl.BlockSpec((tokens, D_MODEL), lambda i: (i, 0)),
        scratch_shapes=[pltpu.VMEM((2, 2, tokens * TOK_SUB, LANES), F32), pltpu.SemaphoreType.DMA((2,))],
        out_shape=jax.ShapeDtypeStruct((t, D_MODEL), F32),
        compiler_params=pltpu.CompilerParams(dimension_semantics=("arbitrary",)),
        name="combine",
    )(slots, slots, gate_t, h2, gf, yb)


def _pad_lanes(v, width=LANES):
    v = v.reshape(1, -1).astype(F32)
    return jnp.pad(v, ((0, 0), (0, width - v.shape[1])))


def _largest_tile(n, cap):
    t = cap
    while n % t:
        t //= 2
    return t


def kernel(x, meta_tokens, norm_mix, w_in, conv_dn, a_log, dt_bias, norm_head_dn, w_proj_dn, w_alpha, b_alpha,
           norm_head_gla, w_proj_gla, w_out, norm_ffn, w_group, b_group, w_router, b_router, w1, w3, w2, norm_final):
    assert norm_mix.shape[0] == 1, "single-layer block"
    bsz, seq, d = x.shape
    assert d == D_MODEL and seq % CHUNK == 0
    t = bsz * seq

    wi = w_in[0]
    offs = np.cumsum([0, DN_QK, DN_QK, DN_V, DN_V, DN_HEADS, DN_HEADS, GLA_QK, GLA_QK, GLA_V, GLA_V, GLA_RANK,
                      D_MODEL, D_MODEL]).tolist()
    seg = lambda i, j: wi[:, offs[i]:offs[j]]
    n_small = 2 * DN_HEADS + GLA_RANK
    w_all = jnp.concatenate([seg(0, 4), seg(6, 10), seg(11, 13), seg(4, 6), seg(10, 11),
                             jnp.zeros((D_MODEL, SMALL_W - n_small), F32)], axis=1).astype(BF16)
    g_mix = norm_mix[0].reshape(1, D_MODEL)

    conv_w = jnp.pad(conv_dn[0], ((0, 8 - CONV_WIDTH), (0, 0)))
    alog = _pad_lanes(a_log[0])
    dtb = _pad_lanes(dt_bias[0])
    nh_dn = norm_head_dn[0].reshape(1, DN_HEAD_DIM)
    lr_off = 2 * DN_HEADS
    wa = jnp.zeros((SMALL_W, GLA_QK), F32).at[lr_off:lr_off + GLA_RANK].set(w_alpha[0])
    ba = b_alpha[0].reshape(1, GLA_QK)
    nh_gla = norm_head_gla[0].reshape(1, GLA_VAL_DIM)

    def mixers(tokens2d, nb, hist, s0_dn, s0_gla, rows):
        tiles_per_seq = tokens2d.shape[0] // nb // rows
        qkv, z, qk_g, v_g, r_g, gates, small, tail = _in_proj(tokens2d, g_mix, w_all, conv_w, hist, rows, tiles_per_seq)
        r3 = lambda a: a.reshape(nb, -1, a.shape[-1])
        o_dn, s_dn = _dn_chunk(r3(qkv), r3(small), r3(z), s0_dn, alog, dtb, nh_dn)
        o_gla, s_gla = _gla_chunk(r3(qk_g), r3(v_g), r3(r_g), r3(small), wa, ba, nh_gla, s0_gla)
        return tail, o_dn, o_gla, gates, s_dn, s_gla

    meta_rows = jnp.pad(meta_tokens.astype(F32), ((CHUNK - N_META, 0), (0, 0)))
    zero_hist = jnp.zeros((HIST_ROWS, 3 * DN_QK), F32)
    zero_dn = jnp.zeros((DN_HEADS, DN_HEAD_DIM, DN_HEAD_DIM), F32)
    zero_gla = jnp.zeros((GLA_HEADS, GLA_VAL_DIM, GLA_KEY_DIM), F32)
    hist, _, _, _, s_dn_m, s_gla_m = mixers(meta_rows, 1, zero_hist, zero_dn, zero_gla, CHUNK)

    x2d = x.reshape(t, d)
    rows = _largest_tile(seq, 512)
    _, o_dn, o_gla, gates, _, _ = mixers(x2d, bsz, hist, s_dn_m[0], s_gla_m[0], rows)

    wr_t = jnp.concatenate([w_group[0], w_router[0]], axis=1).T
    wr_t = jnp.pad(wr_t, ((0, LANES - wr_t.shape[0]), (0, 0)))
    br = jnp.pad(jnp.concatenate([b_group[0], b_router[0]]), (0, LANES - N_GROUPS - N_EXPERTS)).reshape(LANES, 1)
    h2, hn, logits_t = _out_proj(
        o_dn.reshape(t, DN_V), o_gla.reshape(t, GLA_V), gates, x2d,
        w_proj_dn[0].astype(BF16), w_proj_gla[0].astype(BF16), w_out[0].astype(BF16),
        norm_ffn[0].reshape(1, D_MODEL), wr_t, br, rows)

    idx, gate, cnt = _route(logits_t, _largest_tile(t, 2048))

    sizes = cnt[:, 0].astype(jnp.int32)
    padded = (sizes + MOE_BLOCK - 1) // MOE_BLOCK * MOE_BLOCK
    pends = jnp.cumsum(padded)
    pstart = (pends - padded).astype(jnp.int32)
    n_blocks = (2 * t) // MOE_BLOCK + N_EXPERTS
    n_used = (pends[-1:] // MOE_BLOCK).astype(jnp.int32)
    block_row0 = jnp.arange(n_blocks, dtype=jnp.int32) * MOE_BLOCK
    block_expert = jnp.minimum(jnp.sum(pends[None, :] <= block_row0[:, None], axis=1), N_EXPERTS - 1).astype(jnp.int32)

    tok_tile = _largest_tile(t, 256)
    slots = _slots(pstart, idx, _largest_tile(t, 2048))
    xs = _dispatch(pstart, sizes, slots, hn, n_blocks * MOE_BLOCK, tok_tile)
    yb = _experts(block_expert, n_used, sizes, xs, w1[0], w3[0], w2[0])
    out = _combine(slots, gate.T, h2, norm_final.reshape(1, D_MODEL), yb, tok_tile)
    return out.reshape(bsz, seq, d)
```

```python
import functools

import jax
import jax.numpy as jnp
import numpy as np
from jax import lax
from jax.experimental import pallas as pl
from jax.experimental.pallas import tpu as pltpu

F32 = jnp.float32
BF16 = jnp.bfloat16

D_MODEL = 1024
CHUNK = 64
N_META = 16
EPS = 1e-6
DN_HEADS = 8
DN_HEAD_DIM = 128
DN_QK = DN_HEADS * DN_HEAD_DIM
DN_V = DN_HEADS * DN_HEAD_DIM
CONV_WIDTH = 4
GLA_HEADS = 4
GLA_KEY_DIM = 128
GLA_VAL_DIM = 256
GLA_QK = GLA_HEADS * GLA_KEY_DIM
GLA_V = GLA_HEADS * GLA_VAL_DIM
GLA_RANK = 16
GLA_TAU = 16.0
N_GROUPS = 8
EXPERTS_PER_GROUP = 8
N_EXPERTS = N_GROUPS * EXPERTS_PER_GROUP
D_EXPERT = 512
MOE_BLOCK = 512
ZERO_ROWS = 256

LANES = 128
SUBLANES = 8
SMALL_W = LANES
HIST_ROWS = 8
VMEM_LIMIT = 56 * 1024 * 1024

_NT = (((1,), (1,)), ((), ()))
_TN = (((0,), (0,)), ((), ()))


def _bdot(a, b):
    return jnp.dot(a.astype(BF16), b.astype(BF16), preferred_element_type=F32)


def _bdot_nt(a, b):
    return lax.dot_general(a.astype(BF16), b.astype(BF16), _NT, preferred_element_type=F32)


def _bdot_tn(a, b):
    return lax.dot_general(a.astype(BF16), b.astype(BF16), _TN, preferred_element_type=F32)


def _sigmoid(x):
    return 0.5 + 0.5 * jnp.tanh(0.5 * x)


def _silu(x):
    half = 0.5 * x
    return half + half * jnp.tanh(half)


def _softplus(x):
    return jnp.maximum(x, 0.0) + jnp.log(1.0 + jnp.exp(-jnp.abs(x)))


def _rms(x, eps=EPS):
    return x * lax.rsqrt(jnp.mean(x * x, axis=-1, keepdims=True) + eps)


def _tri_incl(n):
    r = lax.broadcasted_iota(jnp.int32, (n, n), 0)
    c = lax.broadcasted_iota(jnp.int32, (n, n), 1)
    return r, c


def _bf16_part(x):
    bits = pltpu.bitcast(x, jnp.uint32) & jnp.uint32(0xFFFF0000)
    return pltpu.bitcast(bits, F32)


def _cumsum_rows(x):
    r, c = _tri_incl(x.shape[0])
    tri = (r >= c).astype(F32)
    hi = _bf16_part(x)
    r1 = x - hi
    mid = _bf16_part(r1)
    lo = r1 - mid
    return (jnp.dot(tri, hi, preferred_element_type=F32) + jnp.dot(tri, mid, preferred_element_type=F32)
            + jnp.dot(tri, lo, preferred_element_type=F32))


IN_GROUPS = ((3 * DN_QK, BF16), (DN_V, BF16), (2 * GLA_QK, BF16), (GLA_V, BF16), (GLA_V, BF16),
             (2 * D_MODEL, BF16), (SMALL_W, F32))


CONV_COLS = 256
PLAIN_COLS = 256


def _inproj_body(tiles_per_seq, x_ref, g_ref, w_ref, cw_ref, hist_ref, *refs):
    o_refs, tail_ref, pre, carry = refs[:-3], refs[-3], refs[-2], refs[-1]
    rows = x_ref.shape[0]

    @pl.when(pl.program_id(0) % tiles_per_seq == 0)
    def _():
        carry[...] = hist_ref[...]

    x = x_ref[...]
    xb = (_rms(x) * g_ref[...]).astype(BF16)

    qkv_ref = o_refs[0]

    def conv_pass(c0):
        cols = slice(c0, c0 + CONV_COLS)
        pre[0:HIST_ROWS, :] = carry[:, cols]
        pre[HIST_ROWS:HIST_ROWS + rows, :] = jnp.dot(xb, w_ref[:, cols], preferred_element_type=F32)
        carry[:, cols] = pre[rows:rows + HIST_ROWS, :]
        acc = None
        for j in range(CONV_WIDTH):
            lo = HIST_ROWS - (CONV_WIDTH - 1) + j
            term = pre[lo:lo + rows, :] * cw_ref[j:j + 1, cols]
            acc = term if acc is None else acc + term
        qkv_ref[:, cols] = _silu(acc).astype(qkv_ref.dtype)

    def plain_pass(o_ref, c0, width, wcol):
        o_ref[:, c0:c0 + width] = jnp.dot(xb, w_ref[:, wcol + c0:wcol + c0 + width],
                                          preferred_element_type=F32).astype(o_ref.dtype)

    plain_tasks = []
    wcol = qkv_ref.shape[1]
    for o_ref in o_refs[1:]:
        width = o_ref.shape[1]
        step = min(width, PLAIN_COLS)
        plain_tasks += [functools.partial(plain_pass, o_ref, c0, step, wcol) for c0 in range(0, width, step)]
        wcol += width

    conv_tasks = [functools.partial(conv_pass, c0) for c0 in range(0, qkv_ref.shape[1], CONV_COLS)]
    n_conv, n_plain = len(conv_tasks), len(plain_tasks)
    for i, conv_task in enumerate(conv_tasks):
        for task in plain_tasks[i * n_plain // n_conv:(i + 1) * n_plain // n_conv]:
            task()
        conv_task()
    tail_ref[...] = carry[...]


def _in_proj(x2d, gain, w_all, conv_w, hist, rows, tiles_per_seq):
    t = x2d.shape[0]
    const = lambda a: pl.BlockSpec(a.shape, lambda i: (0, 0))
    in_specs = [pl.BlockSpec((rows, D_MODEL), lambda i: (i, 0)), const(gain),
                pl.BlockSpec(w_all.shape, lambda i: (0, 0), pipeline_mode=pl.Buffered(1)),
                const(conv_w), const(hist)]
    out_specs = [pl.BlockSpec((rows, w), lambda i: (i, 0)) for w, _ in IN_GROUPS] + [const(hist)]
    out_shape = [jax.ShapeDtypeStruct((t, w), dt) for w, dt in IN_GROUPS] + [jax.ShapeDtypeStruct(hist.shape, F32)]
    return pl.pallas_call(
        functools.partial(_inproj_body, tiles_per_seq), grid=(t // rows,),
        in_specs=in_specs, out_specs=out_specs, out_shape=out_shape,
        scratch_shapes=[pltpu.VMEM((HIST_ROWS + rows, CONV_COLS), F32), pltpu.VMEM(hist.shape, F32)],
        compiler_params=pltpu.CompilerParams(dimension_semantics=("arbitrary",), vmem_limit_bytes=VMEM_LIMIT),
        name="in_proj",
    )(x2d, gain, w_all, conv_w, hist)


def _block_diag2(m2):
    n = m2.shape[0]
    lane = lax.broadcasted_iota(jnp.int32, m2.shape, 1)
    return jnp.concatenate([jnp.where(lane < n, m2, 0.0), jnp.where(lane >= n, m2, 0.0)], axis=0)


def _unit_lower_inverse(mats):
    n = mats[0].shape[0]
    r = lax.broadcasted_iota(jnp.int32, (n, 2 * n), 0)
    c = lax.broadcasted_iota(jnp.int32, (n, 2 * n), 1) & (n - 1)
    eye = (r == c).astype(F32)
    pair = (r >> 1) == (c >> 1)
    ts = [eye - jnp.where(pair, a, 0.0) for a in mats]
    m = 2
    while m < n:
        sh = m.bit_length() - 1
        keep = ((r >> (sh + 1)) == (c >> (sh + 1))) & ((r >> sh) != (c >> sh))
        ams = [_block_diag2(jnp.where(keep, a, 0.0)) for a in mats]
        xs = [jnp.dot(t, am, preferred_element_type=F32) for t, am in zip(ts, ams)]
        ys = [jnp.dot(x, _block_diag2(t), preferred_element_type=F32) for x, t in zip(xs, ts)]
        ts = [t - y for t, y in zip(ts, ys)]
        m *= 2
    return ts


def _l2_normalise(xs, ones_bd, scale):
    out = []
    for p in range(0, len(xs), 2):
        sq = jnp.concatenate([xs[p] * xs[p], xs[p + 1] * xs[p + 1]], axis=1)
        ss = jnp.dot(sq, ones_bd, preferred_element_type=F32)
        inv = lax.rsqrt(ss + EPS) * scale
        out.append(xs[p] * inv[:, :DN_HEAD_DIM])
        out.append(xs[p + 1] * inv[:, DN_HEAD_DIM:])
    return out


DN_WAVE = 32


def _dn_body(qkv_ref, sm_ref, z_ref, s0_ref, alog_ref, dtb_ref, nh_ref, ones_ref, o_ref, sfin_ref, s_scr):
    c_id = pl.program_id(1)
    n_rows = qkv_ref.shape[0]
    all_items = [(b, h) for b in range(n_rows) for h in range(DN_HEADS)]

    @pl.when(c_id == 0)
    def _():
        for b in range(n_rows):
            s_scr[b] = s0_ref[...]

    def head_cols(b, col):
        return qkv_ref[b, :, col:col + DN_HEAD_DIM].astype(F32)

    gc_all, gc_t, beta_all = [], [], []
    for b in range(n_rows):
        sm = sm_ref[b]
        g_b = -jnp.exp(alog_ref[...]) * _softplus(sm + dtb_ref[...])
        beta_all.append(_sigmoid(sm))
        gc_b = _cumsum_rows(g_b)
        gc_all.append(gc_b)
        gc_t.append(jnp.concatenate([gc_b, jnp.zeros_like(gc_b)], axis=0).T)

    r2 = lax.broadcasted_iota(jnp.int32, (CHUNK, 2 * CHUNK), 0)
    lane2 = lax.broadcasted_iota(jnp.int32, (CHUNK, 2 * CHUNK), 1)
    c2 = lane2 & (CHUNK - 1)
    first = lane2 < CHUNK
    causal2 = r2 >= c2
    strict2 = r2 > c2
    scale = DN_HEAD_DIM ** -0.5
    ones_bd = ones_ref[...]

    def wave(items):
        n_items = range(len(items))
        q = [head_cols(b, h * DN_HEAD_DIM) for b, h in items]
        k = [head_cols(b, DN_QK + h * DN_HEAD_DIM) for b, h in items]
        v = [head_cols(b, 2 * DN_QK + h * DN_HEAD_DIM) for b, h in items]
        q = _l2_normalise(q, ones_bd, scale)
        k = _l2_normalise(k, ones_bd, 1.0)
        gcol = [gc_all[b][:, h:h + 1] for b, h in items]
        bcol = [beta_all[b][:, DN_HEADS + h:DN_HEADS + h + 1] for b, h in items]
        glast = [gc_all[b][CHUNK - 1:CHUNK, h:h + 1] for b, h in items]
        eg = [jnp.exp(gcol[i]) for i in n_items]

        pairs = range(0, len(items), 2)
        n_pairs = range(len(pairs))

        def diag2(x0, x1):
            return jnp.concatenate([jnp.concatenate([x0, jnp.zeros_like(x1)], axis=1),
                                    jnp.concatenate([jnp.zeros_like(x0), x1], axis=1)], axis=0)

        def lanes2(x0, x1):
            return jnp.where(first, x0, x1)

        def head_row(i):
            b, h = items[i]
            return gc_t[b][h:h + 1, :]

        k_bd = [diag2(k[p], k[p + 1]) for p in pairs]
        k_cat = [jnp.concatenate([k[p], k[p + 1]], axis=1) for p in pairs]
        q_cat = [jnp.concatenate([q[p], q[p + 1]], axis=1) for p in pairs]
        kk2 = [lax.dot_general(k_cat[j], k_bd[j], _NT, preferred_element_type=F32) for j in n_pairs]
        qk2 = [lax.dot_general(q_cat[j], k_bd[j], _NT, preferred_element_type=F32) for j in n_pairs]
        grow2 = [head_row(p) + pltpu.roll(head_row(p + 1), CHUNK, axis=1) for p in pairs]
        gcol2 = [lanes2(gcol[p], gcol[p + 1]) for p in pairs]
        bcol2 = [lanes2(bcol[p], bcol[p + 1]) for p in pairs]
        decay2 = [jnp.where(causal2, jnp.exp(jnp.where(causal2, gcol2[j] - grow2[j], 0.0)), 0.0) for j in n_pairs]
        a_mat2 = [jnp.where(strict2, bcol2[j] * kk2[j] * decay2[j], 0.0) for j in n_pairs]
        qk2 = [jnp.where(causal2, qk2[j] * decay2[j], 0.0) for j in n_pairs]
        t_inv2 = _unit_lower_inverse(a_mat2)
        rhs = [jnp.concatenate([v[i] * bcol[i], k[i] * (bcol[i] * eg[i])], axis=1) for i in n_items]
        sol2 = [jnp.dot(t_inv2[j], diag2(rhs[p], rhs[p + 1]), preferred_element_type=F32)
                for j, p in enumerate(pairs)]
        sol = [sol2[i // 2][:, (i % 2) * 2 * DN_HEAD_DIM:(i % 2 + 1) * 2 * DN_HEAD_DIM] for i in n_items]
        s_old = [s_scr[b, h] for b, h in items]
        qe = [q[i] * eg[i] for i in n_items]
        k_dec = [k[i] * jnp.exp(glast[i] - gcol[i]) for i in n_items]
        ws = [jnp.dot(sol[i][:, DN_HEAD_DIM:], s_old[i], preferred_element_type=F32) for i in n_items]
        o_inter = [jnp.dot(qe[i], s_old[i], preferred_element_type=F32) for i in n_items]
        v_new = [sol[i][:, :DN_HEAD_DIM] - ws[i] for i in n_items]
        o_intra2 = [jnp.dot(qk2[j], diag2(v_new[p], v_new[p + 1]), preferred_element_type=F32)
                    for j, p in enumerate(pairs)]
        o_intra = [o_intra2[i // 2][:, (i % 2) * DN_HEAD_DIM:(i % 2 + 1) * DN_HEAD_DIM] for i in n_items]
        s_add = [lax.dot_general(k_dec[i], v_new[i], _TN, preferred_element_type=F32) for i in n_items]
        for i, (b, h) in enumerate(items):
            col = h * DN_HEAD_DIM
            s_scr[b, h] = s_old[i] * jnp.exp(glast[i]) + s_add[i]
            zed = z_ref[b, :, col:col + DN_HEAD_DIM].astype(F32)
            o = o_inter[i] + o_intra[i]
            o_ref[b, :, col:col + DN_HEAD_DIM] = (_rms(o) * nh_ref[...] * _silu(zed)).astype(o_ref.dtype)

    for w0 in range(0, len(all_items), DN_WAVE):
        wave(all_items[w0:w0 + DN_WAVE])

    @pl.when(c_id == pl.num_programs(1) - 1)
    def _():
        sfin_ref[...] = s_scr[...]


def _rows_per_step(b, want):
    while b % want:
        want //= 2
    return want


def _dn_chunk(qkv, small, z, s0, alog, dtb, nh):
    b, l, _ = qkv.shape
    lane_head = np.arange(2 * DN_HEAD_DIM) // DN_HEAD_DIM
    ones_bd = jnp.asarray(lane_head[:, None] == lane_head[None, :], F32)
    nc = l // CHUNK
    nr = _rows_per_step(b, 8)
    const2 = lambda bi, ci: (0, 0)
    blk = lambda w: pl.BlockSpec((nr, CHUNK, w), lambda bi, ci: (bi, ci, 0))
    return pl.pallas_call(
        _dn_body, grid=(b // nr, nc),
        in_specs=[
            blk(3 * DN_QK), blk(SMALL_W), blk(DN_V),
            pl.BlockSpec(s0.shape, lambda bi, ci: (0, 0, 0)),
            pl.BlockSpec(alog.shape, const2),
            pl.BlockSpec(dtb.shape, const2),
            pl.BlockSpec(nh.shape, const2),
            pl.BlockSpec(ones_bd.shape, const2),
        ],
        out_specs=[
            blk(DN_V),
            pl.BlockSpec((nr, DN_HEADS, DN_HEAD_DIM, DN_HEAD_DIM), lambda bi, ci: (bi, 0, 0, 0)),
        ],
        out_shape=[
            jax.ShapeDtypeStruct((b, l, DN_V), BF16),
            jax.ShapeDtypeStruct((b, DN_HEADS, DN_HEAD_DIM, DN_HEAD_DIM), F32),
        ],
        scratch_shapes=[pltpu.VMEM((nr, DN_HEADS, DN_HEAD_DIM, DN_HEAD_DIM), F32)],
        compiler_params=pltpu.CompilerParams(dimension_semantics=("parallel", "arbitrary"),
                                             vmem_limit_bytes=VMEM_LIMIT),
        name="dn_chunk",
    )(qkv, small, z, s0, alog, dtb, nh, ones_bd)


def _gla_body(qk_ref, v_ref, r_ref, sm_ref, wa_ref, ba_ref, nh_ref, s0_ref, o_ref, sfin_ref, s_scr):
    c_id = pl.program_id(1)
    n_rows = qk_ref.shape[0]
    items = [(b, h) for b in range(n_rows) for h in range(GLA_HEADS)]
    n_items = range(len(items))

    @pl.when(c_id == 0)
    def _():
        for b in range(n_rows):
            s_scr[b] = s0_ref[...]

    b_all = []
    for b in range(n_rows):
        la = jnp.dot(sm_ref[b], wa_ref[...], preferred_element_type=F32) + ba_ref[...]
        log_alpha = (jnp.minimum(la, 0.0) - jnp.log(1.0 + jnp.exp(-jnp.abs(la)))) * (1.0 / GLA_TAU)
        b_all.append(_cumsum_rows(log_alpha))

    r, c = _tri_incl(CHUNK)
    causal = r >= c
    scale = GLA_KEY_DIM ** -0.5
    mid = CHUNK // 2 - 1

    q = [qk_ref[b, :, h * GLA_KEY_DIM:(h + 1) * GLA_KEY_DIM].astype(F32) * scale for b, h in items]
    k = [qk_ref[b, :, GLA_QK + h * GLA_KEY_DIM:GLA_QK + (h + 1) * GLA_KEY_DIM].astype(F32) for b, h in items]
    v = [v_ref[b, :, h * GLA_VAL_DIM:(h + 1) * GLA_VAL_DIM].astype(F32) for b, h in items]
    bh = [b_all[b][:, h * GLA_KEY_DIM:(h + 1) * GLA_KEY_DIM] for b, h in items]
    bmid = [x[mid:mid + 1, :] for x in bh]
    blast = [x[CHUNK - 1:CHUNK, :] for x in bh]
    qs = [q[i] * jnp.exp(bh[i] - bmid[i]) for i in n_items]
    ks = [k[i] * jnp.exp(bmid[i] - bh[i]) for i in n_items]
    qd = [q[i] * jnp.exp(bh[i]) for i in n_items]
    kd = [k[i] * jnp.exp(blast[i] - bh[i]) for i in n_items]
    st = [s_scr[b, h] for b, h in items]
    att = [lax.dot_general(qs[i], ks[i], _NT, preferred_element_type=F32) for i in n_items]
    o_inter = [lax.dot_general(qd[i], st[i], _NT, preferred_element_type=F32) for i in n_items]
    s_add = [lax.dot_general(v[i], kd[i], _TN, preferred_element_type=F32) for i in n_items]
    att = [jnp.where(causal, a, 0.0) for a in att]
    o_intra = [jnp.dot(att[i], v[i], preferred_element_type=F32) for i in n_items]
    for i, (b, h) in enumerate(items):
        vc = h * GLA_VAL_DIM
        s_scr[b, h] = st[i] * jnp.exp(blast[i]) + s_add[i]
        gate = _silu(r_ref[b, :, vc:vc + GLA_VAL_DIM].astype(F32))
        o = o_inter[i] + o_intra[i]
        o_ref[b, :, vc:vc + GLA_VAL_DIM] = (_rms(o) * nh_ref[...] * gate).astype(o_ref.dtype)

    @pl.when(c_id == pl.num_programs(1) - 1)
    def _():
        sfin_ref[...] = s_scr[...]


def _gla_chunk(qk, v, rr, small, wa, ba, nh, s0):
    b, l, _ = qk.shape
    nc = l // CHUNK
    nr = _rows_per_step(b, 8)
    const2 = lambda bi, ci: (0, 0)
    blk = lambda w: pl.BlockSpec((nr, CHUNK, w), lambda bi, ci: (bi, ci, 0))
    return pl.pallas_call(
        _gla_body, grid=(b // nr, nc),
        in_specs=[
            blk(2 * GLA_QK), blk(GLA_V), blk(GLA_V), blk(SMALL_W),
            pl.BlockSpec(wa.shape, const2), pl.BlockSpec(ba.shape, const2), pl.BlockSpec(nh.shape, const2),
            pl.BlockSpec(s0.shape, lambda bi, ci: (0, 0, 0)),
        ],
        out_specs=[
            blk(GLA_V),
            pl.BlockSpec((nr, GLA_HEADS, GLA_VAL_DIM, GLA_KEY_DIM), lambda bi, ci: (bi, 0, 0, 0)),
        ],
        out_shape=[
            jax.ShapeDtypeStruct((b, l, GLA_V), BF16),
            jax.ShapeDtypeStruct((b, GLA_HEADS, GLA_VAL_DIM, GLA_KEY_DIM), F32),
        ],
        scratch_shapes=[pltpu.VMEM((nr, GLA_HEADS, GLA_VAL_DIM, GLA_KEY_DIM), F32)],
        compiler_params=pltpu.CompilerParams(dimension_semantics=("parallel", "arbitrary"),
                                             vmem_limit_bytes=VMEM_LIMIT),
        name="gla_chunk",
    )(qk, v, rr, small, wa, ba, nh, s0)


TOK_SUB = D_MODEL // LANES
assert TOK_SUB == SUBLANES


def _tiles_to_rows(ref, n):
    return jnp.concatenate([ref[pl.ds(s, n, stride=TOK_SUB), :] for s in range(TOK_SUB)], axis=1)


def _rows_to_tiles(ref, value):
    n = value.shape[0]
    for s in range(TOK_SUB):
        ref[pl.ds(s, n, stride=TOK_SUB), :] = value[:, s * LANES:(s + 1) * LANES]


def _tile_rows(tok, count=1):
    return pl.ds(pl.multiple_of(tok * TOK_SUB, TOK_SUB), count * TOK_SUB)


def _outproj_body(odn_ref, ogla_ref, gates_ref, x_ref, wd_ref, wg_ref, wo_ref, gn_ref, wr_ref, br_ref,
                  h2_ref, hn_ref, lg_ref):
    y_dn = jnp.dot(odn_ref[...], wd_ref[...], preferred_element_type=F32)
    y_gla = jnp.dot(ogla_ref[...], wg_ref[...], preferred_element_type=F32)
    gd = _sigmoid(gates_ref[:, 0:D_MODEL].astype(F32))
    gg = _sigmoid(gates_ref[:, D_MODEL:2 * D_MODEL].astype(F32))
    merged = gd * y_dn + gg * y_gla
    h2 = x_ref[...] + _bdot(merged, wo_ref[...])
    h2_ref[...] = h2
    hn = _rms(h2) * gn_ref[...]
    _rows_to_tiles(hn_ref, hn)
    wr = wr_ref[...]
    wr_hi, hn_hi = _bf16_part(wr), _bf16_part(hn)
    nt = lambda a, b: lax.dot_general(a, b, _NT, preferred_element_type=F32)
    lg_ref[...] = nt(wr_hi, hn_hi) + nt(wr_hi, hn - hn_hi) + nt(wr - wr_hi, hn_hi) + br_ref[...]


def _out_proj(o_dn, o_gla, gates, x2d, wd, wg, wo, gn, wr_t, br, rows):
    t = x2d.shape[0]
    row_blk = lambda w: pl.BlockSpec((rows, w), lambda i: (i, 0))
    const = lambda a: pl.BlockSpec(a.shape, lambda i: (0, 0))
    return pl.pallas_call(
        _outproj_body, grid=(t // rows,),
        in_specs=[row_blk(DN_V), row_blk(GLA_V), row_blk(2 * D_MODEL), row_blk(D_MODEL),
                  const(wd), const(wg), const(wo), const(gn), const(wr_t), const(br)],
        out_specs=[row_blk(D_MODEL), pl.BlockSpec((rows * TOK_SUB, LANES), lambda i: (i, 0)),
                   pl.BlockSpec((LANES, rows), lambda i: (0, i))],
        out_shape=[jax.ShapeDtypeStruct((t, D_MODEL), F32), jax.ShapeDtypeStruct((t * TOK_SUB, LANES), F32),
                   jax.ShapeDtypeStruct((LANES, t), F32)],
        compiler_params=pltpu.CompilerParams(dimension_semantics=("parallel",), vmem_limit_bytes=VMEM_LIMIT),
        name="out_proj",
    )(o_dn, o_gla, gates, x2d, wd, wg, wo, gn, wr_t, br)


ROUTE_SUB = 256


def _route_body(lg_ref, idx_ref, gate_ref, cnt_ref, carry):
    step = pl.program_id(0)

    @pl.when(step == 0)
    def _():
        carry[...] = jnp.zeros_like(carry)

    tt = lg_ref.shape[1]
    gl = lg_ref[0:N_GROUPS, :]
    gmax = jnp.max(gl, axis=0, keepdims=True)
    rid8 = lax.broadcasted_iota(jnp.int32, (N_GROUPS, tt), 0)
    gsel = jnp.min(jnp.where(gl == gmax, rid8, N_GROUPS), axis=0, keepdims=True)
    gw = 1.0 / jnp.sum(jnp.exp(gl - gmax), axis=0, keepdims=True)
    el = lg_ref[N_GROUPS:N_GROUPS + N_EXPERTS, :]
    rid = lax.broadcasted_iota(jnp.int32, (N_EXPERTS, tt), 0)
    neg = jnp.float32(-jnp.inf)
    ein = jnp.where((rid >> 3) == gsel, el, neg)
    t1 = jnp.max(ein, axis=0, keepdims=True)
    i1 = jnp.min(jnp.where(ein == t1, rid, N_EXPERTS), axis=0, keepdims=True)
    ein2 = jnp.where(rid == i1, neg, ein)
    t2 = jnp.max(ein2, axis=0, keepdims=True)
    i2 = jnp.min(jnp.where(ein2 == t2, rid, N_EXPERTS), axis=0, keepdims=True)
    e21 = jnp.exp(t2 - t1)
    den = 1.0 / (1.0 + e21)
    sel1 = rid == i1
    sel2 = rid == i2
    onehot = jnp.where(sel1 | sel2, 1.0, 0.0)

    ur, uc = _tri_incl(ROUTE_SUB)
    upper = (ur <= uc).astype(BF16)
    run = carry[...]
    r1_parts, r2_parts = [], []
    for s in range(tt // ROUTE_SUB):
        sl = slice(s * ROUTE_SUB, (s + 1) * ROUTE_SUB)
        oh = onehot[:, sl]
        incl = jnp.dot(oh.astype(BF16), upper, preferred_element_type=F32) + run
        excl = incl - oh
        r1_parts.append(jnp.sum(jnp.where(sel1[:, sl], excl, 0.0), axis=0, keepdims=True))
        r2_parts.append(jnp.sum(jnp.where(sel2[:, sl], excl, 0.0), axis=0, keepdims=True))
        run = jnp.broadcast_to(incl[:, ROUTE_SUB - 1:ROUTE_SUB], run.shape)
    carry[...] = run
    r1 = jnp.concatenate(r1_parts, axis=1) if len(r1_parts) > 1 else r1_parts[0]
    r2 = jnp.concatenate(r2_parts, axis=1) if len(r2_parts) > 1 else r2_parts[0]

    idx_ref[...] = jnp.zeros_like(idx_ref)
    idx_ref[0:1, :] = i1
    idx_ref[1:2, :] = i2
    idx_ref[2:3, :] = r1.astype(jnp.int32)
    idx_ref[3:4, :] = r2.astype(jnp.int32)
    gate_ref[...] = jnp.zeros_like(gate_ref)
    gate_ref[0:1, :] = den * gw
    gate_ref[1:2, :] = e21 * den * gw
    cnt_ref[...] = run[:, 0:LANES]


def _route(logits_t, lanes):
    t = logits_t.shape[1]
    return pl.pallas_call(
        _route_body, grid=(t // lanes,),
        in_specs=[pl.BlockSpec((LANES, lanes), lambda i: (0, i))],
        out_specs=[pl.BlockSpec((8, lanes), lambda i: (0, i)), pl.BlockSpec((8, lanes), lambda i: (0, i)),
                   pl.BlockSpec((N_EXPERTS, LANES), lambda i: (0, 0))],
        out_shape=[jax.ShapeDtypeStruct((8, t), jnp.int32), jax.ShapeDtypeStruct((8, t), F32),
                   jax.ShapeDtypeStruct((N_EXPERTS, LANES), F32)],
        scratch_shapes=[pltpu.VMEM((N_EXPERTS, ROUTE_SUB), F32)],
        compiler_params=pltpu.CompilerParams(dimension_semantics=("arbitrary",)),
        name="route",
    )(logits_t)


def _row_copy(src_ref, src_tok, dst_ref, dst_tok, sem):
    return pltpu.make_async_copy(src_ref.at[_tile_rows(src_tok)], dst_ref.at[_tile_rows(dst_tok)], sem)


ROW_UNROLL = 8


def _slots_body(pstart_ref, idx_ref, slot_ref):
    idx = idx_ref[...]
    base = jnp.zeros_like(idx)
    for e in range(N_EXPERTS):
        base = jnp.where(idx == e, pstart_ref[e], base)
    slot_ref[...] = jnp.zeros_like(idx)
    slot_ref[0:2, :] = base[0:2, :] + idx[2:4, :]


def _slots(pstart, idx, lanes):
    t = idx.shape[1]
    grid_spec = pltpu.PrefetchScalarGridSpec(
        num_scalar_prefetch=1, grid=(t // lanes,),
        in_specs=[pl.BlockSpec((8, lanes), lambda i, ps: (0, i))],
        out_specs=pl.BlockSpec((8, lanes), lambda i, ps: (0, i)),
    )
    return pl.pallas_call(
        _slots_body, grid_spec=grid_spec, out_shape=jax.ShapeDtypeStruct(idx.shape, jnp.int32),
        compiler_params=pltpu.CompilerParams(dimension_semantics=("parallel",)),
        name="slots",
    )(pstart, idx)


DISPATCH_BUFS = 3


def _dispatch_body(pstart_ref, size_ref, slot_ref, hn_hbm, xs_ref, zblk, hbuf, sem, load_sems, row_sems):
    step = pl.program_id(0)
    n_steps = pl.num_programs(0)
    n_tok = hbuf.shape[1] // TOK_SUB

    def load(tile):
        buf = tile % DISPATCH_BUFS
        return pltpu.make_async_copy(hn_hbm.at[_tile_rows(tile * n_tok, n_tok)], hbuf.at[buf], load_sems.at[buf])

    def wait_rows(parity):
        for k in range(2):
            pltpu.make_async_copy(hbuf.at[0], xs_ref.at[_tile_rows(0, n_tok)], row_sems.at[parity]).wait()

    @pl.when(step == 0)
    def _():
        load(0).start()

        @pl.when(n_steps > 1)
        def _():
            load(1).start()

        zblk[...] = jnp.zeros_like(zblk)
        parts = MOE_BLOCK // ZERO_ROWS

        def clear_copy(row0, part):
            return pltpu.make_async_copy(zblk, xs_ref.at[_tile_rows(row0 + part * ZERO_ROWS, ZERO_ROWS)], sem)

        def per_expert(e, n_rows):
            size = size_ref[e]
            full = size // MOE_BLOCK * MOE_BLOCK

            @pl.when(size != full)
            def _():
                for part in range(parts):
                    clear_copy(pstart_ref[e] + full, part).start()
                for part in range(parts):
                    clear_copy(pstart_ref[e] + full, part).wait()

            return n_rows + (size + MOE_BLOCK - 1) // MOE_BLOCK * MOE_BLOCK

        n_rows = lax.fori_loop(0, N_EXPERTS, per_expert, 0)

        def clear_unused(j, c):
            for part in range(parts):
                clear_copy(j * MOE_BLOCK, part).start()
            for part in range(parts):
                clear_copy(j * MOE_BLOCK, part).wait()
            return c

        lax.fori_loop(n_rows // MOE_BLOCK, xs_ref.shape[0] // (MOE_BLOCK * TOK_SUB), clear_unused, 0)

    load(step).wait()
    src = hbuf.at[step % DISPATCH_BUFS]
    row_sem = row_sems.at[step % 2]

    def issue(t, carry):
        for k in range(2):
            _row_copy(src, t, xs_ref, slot_ref[k, t], row_sem).start()
        return carry

    lax.fori_loop(0, n_tok, issue, 0, unroll=ROW_UNROLL)

    @pl.when(step > 0)
    def _():
        wait_rows((step - 1) % 2)

    @pl.when(step + 2 < n_steps)
    def _():
        load(step + 2).start()

    @pl.when(step == n_steps - 1)
    def _():
        wait_rows(step % 2)


def _dispatch(pstart, sizes, slots, hn, n_rows_out, tokens):
    t = hn.shape[0] // TOK_SUB
    grid_spec = pltpu.PrefetchScalarGridSpec(
        num_scalar_prefetch=2, grid=(t // tokens,),
        in_specs=[pl.BlockSpec((8, tokens), lambda i, ps, sz: (0, i), memory_space=pltpu.SMEM),
                  pl.BlockSpec(memory_space=pl.ANY)],
        out_specs=pl.BlockSpec(memory_space=pl.ANY),
        scratch_shapes=[pltpu.VMEM((ZERO_ROWS * TOK_SUB, LANES), F32),
                        pltpu.VMEM((DISPATCH_BUFS, tokens * TOK_SUB, LANES), F32),
                        pltpu.SemaphoreType.DMA(()), pltpu.SemaphoreType.DMA((DISPATCH_BUFS,)),
                        pltpu.SemaphoreType.DMA((2,))],
    )
    return pl.pallas_call(
        _dispatch_body, grid_spec=grid_spec,
        out_shape=jax.ShapeDtypeStruct((n_rows_out * TOK_SUB, LANES), F32),
        compiler_params=pltpu.CompilerParams(dimension_semantics=("arbitrary",)),
        name="dispatch",
    )(pstart, sizes, slots, hn)


def _experts_body(be_ref, nused_ref, par_ref, nxt_ref, nxt2_ref, xs_ref, w1_hbm, w3_hbm, w2_hbm, yb_ref,
                  w1f, w3f, w2f, w1b, w3b, w2b, sems):
    j = pl.program_id(0)
    used = j < nused_ref[0]
    expert = be_ref[j]
    slot = par_ref[j]
    new_expert = jnp.logical_or(j == 0, expert != be_ref[jnp.maximum(j - 1, 0)])

    def fetch(e, s):
        return [pltpu.make_async_copy(w_hbm.at[e], w_vmem.at[s], sems.at[s])
                for w_hbm, w_vmem in ((w1_hbm, w1f), (w3_hbm, w3f), (w2_hbm, w2f))]

    @pl.when(j == 0)
    def _():
        for copy in fetch(expert, slot):
            copy.start()

        @pl.when(nxt_ref[0] >= 0)
        def _():
            for copy in fetch(nxt_ref[0], 1 - slot):
                copy.start()

    @pl.when(jnp.logical_and(used, new_expert))
    def _():
        for copy in fetch(expert, slot):
            copy.wait()
        w1b[...] = w1f[slot].astype(BF16)
        w3b[...] = w3f[slot].astype(BF16)
        w2b[...] = w2f[slot].astype(BF16)

        @pl.when(nxt2_ref[j] >= 0)
        def _():
            for copy in fetch(nxt2_ref[j], slot):
                copy.start()

    @pl.when(used)
    def _():
        xb = _tiles_to_rows(xs_ref, MOE_BLOCK).astype(BF16)
        h1 = jnp.dot(xb, w1b[...], preferred_element_type=F32)
        h3 = jnp.dot(xb, w3b[...], preferred_element_type=F32)
        hid = (_silu(h1) * h3).astype(BF16)
        _rows_to_tiles(yb_ref, jnp.dot(hid, w2b[...], preferred_element_type=F32))

    @pl.when(jnp.logical_not(used))
    def _():
        yb_ref[...] = jnp.zeros_like(yb_ref)


def _experts(block_expert, n_used, sizes, xs, w1, w3, w2):
    p = xs.shape[0] // TOK_SUB
    nb = p // MOE_BLOCK
    owns = sizes > 0
    ordinal = jnp.cumsum(owns) - owns
    ids = jnp.arange(N_EXPERTS, dtype=jnp.int32)
    later = jnp.where((ids[None, :] > ids[:, None]) & owns[None, :], ids[None, :], N_EXPERTS)
    nxt_e = jnp.min(later, axis=1)
    nxt_e = jnp.where(nxt_e == N_EXPERTS, -1, nxt_e).astype(jnp.int32)

    def lookup(table, index):
        hit = index[:, None] == ids[None, :]
        return jnp.sum(jnp.where(hit, table[None, :] + 1, 0), axis=1).astype(jnp.int32) - 1

    nxt2_e = lookup(nxt_e, nxt_e)
    parity = lookup(ordinal.astype(jnp.int32), block_expert) % 2
    nxt, nxt2 = lookup(nxt_e, block_expert), lookup(nxt2_e, block_expert)

    row_map = lambda j, be, nu, pa, n1, n2: (jnp.minimum(j, nu[0] - 1), 0)
    grid_spec = pltpu.PrefetchScalarGridSpec(
        num_scalar_prefetch=5, grid=(nb,),
        in_specs=[pl.BlockSpec((MOE_BLOCK * TOK_SUB, LANES), row_map),
                  pl.BlockSpec(memory_space=pl.ANY), pl.BlockSpec(memory_space=pl.ANY),
                  pl.BlockSpec(memory_space=pl.ANY)],
        out_specs=pl.BlockSpec((MOE_BLOCK * TOK_SUB, LANES), lambda j, be, nu, pa, n1, n2: (j, 0)),
        scratch_shapes=[pltpu.VMEM((2, D_MODEL, D_EXPERT), F32), pltpu.VMEM((2, D_MODEL, D_EXPERT), F32),
                        pltpu.VMEM((2, D_EXPERT, D_MODEL), F32),
                        pltpu.VMEM((D_MODEL, D_EXPERT), BF16), pltpu.VMEM((D_MODEL, D_EXPERT), BF16),
                        pltpu.VMEM((D_EXPERT, D_MODEL), BF16), pltpu.SemaphoreType.DMA((2,))],
    )
    return pl.pallas_call(
        _experts_body, grid_spec=grid_spec,
        out_shape=jax.ShapeDtypeStruct((p * TOK_SUB, LANES), F32),
        compiler_params=pltpu.CompilerParams(dimension_semantics=("arbitrary",), vmem_limit_bytes=VMEM_LIMIT),
        name="experts",
    )(block_expert, n_used, parity, nxt, nxt2, xs, w1, w3, w2)


def _combine_body(slot_ref, slot_next_ref, gate_ref, h2_ref, gf_ref, yb_ref, o_ref, ybuf, sems):
    n_tok = h2_ref.shape[0]
    step = pl.program_id(0)
    cur = step % 2

    def gather(slots, buf_id):
        def issue(t, carry):
            for k in range(2):
                pltpu.make_async_copy(yb_ref.at[_tile_rows(slots[k, t])], ybuf.at[buf_id, k, _tile_rows(t)],
                                      sems.at[buf_id]).start()
            return carry

        lax.fori_loop(0, n_tok, issue, 0, unroll=ROW_UNROLL)

    @pl.when(step == 0)
    def _():
        gather(slot_ref, 0)

    @pl.when(step + 1 < pl.num_programs(0))
    def _():
        gather(slot_next_ref, 1 - cur)

    for k in range(2):
        pltpu.make_async_copy(yb_ref.at[_tile_rows(0, n_tok)], ybuf.at[cur, k], sems.at[cur]).wait()
    y0 = _tiles_to_rows(ybuf.at[cur, 0], n_tok)
    y1 = _tiles_to_rows(ybuf.at[cur, 1], n_tok)
    y = h2_ref[...] + gate_ref[:, 0:1] * y0 + gate_ref[:, 1:2] * y1
    o_ref[...] = _rms(y) * gf_ref[...]


def _combine(slots, gate_t, h2, gf, yb, tokens):
    t = h2.shape[0]
    last = t // tokens - 1
    return pl.pallas_call(
        _combine_body, grid=(t // tokens,),
        in_specs=[pl.BlockSpec((8, tokens), lambda i: (0, i), memory_space=pltpu.SMEM),
                  pl.BlockSpec((8, tokens), lambda i: (0, jnp.minimum(i + 1, last)), memory_space=pltpu.SMEM),
                  pl.BlockSpec((tokens, 8), lambda i: (i, 0)),
                  pl.BlockSpec((tokens, D_MODEL), lambda i: (i, 0)),
                  pl.BlockSpec((1, D_MODEL), lambda i: (0, 0)),
                  pl.BlockSpec(memory_space=pl.ANY)],
        out_specs=pl.BlockSpec((tokens, D_MODEL), lambda i: (i, 0)),
        scratch_shapes=[pltpu.VMEM((2, 2, tokens * TOK_SUB, LANES), F32), pltpu.SemaphoreType.DMA((2,))],
        out_shape=jax.ShapeDtypeStruct((t, D_MODEL), F32),
        compiler_params=pltpu.CompilerParams(dimension_semantics=("arbitrary",)),
        name="combine",
    )(slots, slots, gate_t, h2, gf, yb)


def _pad_lanes(v, width=LANES):
    v = v.reshape(1, -1).astype(F32)
    return jnp.pad(v, ((0, 0), (0, width - v.shape[1])))


def _largest_tile(n, cap):
    t = cap
    while n % t:
        t //= 2
    return t


def kernel(x, meta_tokens, norm_mix, w_in, conv_dn, a_log, dt_bias, norm_head_dn, w_proj_dn, w_alpha, b_alpha,
           norm_head_gla, w_proj_gla, w_out, norm_ffn, w_group, b_group, w_router, b_router, w1, w3, w2, norm_final):
    assert norm_mix.shape[0] == 1, "single-layer block"
    bsz, seq, d = x.shape
    assert d == D_MODEL and seq % CHUNK == 0
    t = bsz * seq

    wi = w_in[0]
    offs = np.cumsum([0, DN_QK, DN_QK, DN_V, DN_V, DN_HEADS, DN_HEADS, GLA_QK, GLA_QK, GLA_V, GLA_V, GLA_RANK,
                      D_MODEL, D_MODEL]).tolist()
    seg = lambda i, j: wi[:, offs[i]:offs[j]]
    n_small = 2 * DN_HEADS + GLA_RANK
    w_all = jnp.concatenate([seg(0, 4), seg(6, 10), seg(11, 13), seg(4, 6), seg(10, 11),
                             jnp.zeros((D_MODEL, SMALL_W - n_small), F32)], axis=1).astype(BF16)
    g_mix = norm_mix[0].reshape(1, D_MODEL)

    conv_w = jnp.pad(conv_dn[0], ((0, 8 - CONV_WIDTH), (0, 0)))
    alog = _pad_lanes(a_log[0])
    dtb = _pad_lanes(dt_bias[0])
    nh_dn = norm_head_dn[0].reshape(1, DN_HEAD_DIM)
    lr_off = 2 * DN_HEADS
    wa = jnp.zeros((SMALL_W, GLA_QK), F32).at[lr_off:lr_off + GLA_RANK].set(w_alpha[0])
    ba = b_alpha[0].reshape(1, GLA_QK)
    nh_gla = norm_head_gla[0].reshape(1, GLA_VAL_DIM)

    def mixers(tokens2d, nb, hist, s0_dn, s0_gla, rows):
        tiles_per_seq = tokens2d.shape[0] // nb // rows
        qkv, z, qk_g, v_g, r_g, gates, small, tail = _in_proj(tokens2d, g_mix, w_all, conv_w, hist, rows, tiles_per_seq)
        r3 = lambda a: a.reshape(nb, -1, a.shape[-1])
        o_dn, s_dn = _dn_chunk(r3(qkv), r3(small), r3(z), s0_dn, alog, dtb, nh_dn)
        o_gla, s_gla = _gla_chunk(r3(qk_g), r3(v_g), r3(r_g), r3(small), wa, ba, nh_gla, s0_gla)
        return tail, o_dn, o_gla, gates, s_dn, s_gla

    meta_rows = jnp.pad(meta_tokens.astype(F32), ((CHUNK - N_META, 0), (0, 0)))
    zero_hist = jnp.zeros((HIST_ROWS, 3 * DN_QK), F32)
    zero_dn = jnp.zeros((DN_HEADS, DN_HEAD_DIM, DN_HEAD_DIM), F32)
    zero_gla = jnp.zeros((GLA_HEADS, GLA_VAL_DIM, GLA_KEY_DIM), F32)
    hist, _, _, _, s_dn_m, s_gla_m = mixers(meta_rows, 1, zero_hist, zero_dn, zero_gla, CHUNK)

    x2d = x.reshape(t, d)
    rows = _largest_tile(seq, 512)
    _, o_dn, o_gla, gates, _, _ = mixers(x2d, bsz, hist, s_dn_m[0], s_gla_m[0], rows)

    wr_t = jnp.concatenate([w_group[0], w_router[0]], axis=1).T
    wr_t = jnp.pad(wr_t, ((0, LANES - wr_t.shape[0]), (0, 0)))
    br = jnp.pad(jnp.concatenate([b_group[0], b_router[0]]), (0, LANES - N_GROUPS - N_EXPERTS)).reshape(LANES, 1)
    h2, hn, logits_t = _out_proj(
        o_dn.reshape(t, DN_V), o_gla.reshape(t, GLA_V), gates, x2d,
        w_proj_dn[0].astype(BF16), w_proj_gla[0].astype(BF16), w_out[0].astype(BF16),
        norm_ffn[0].reshape(1, D_MODEL), wr_t, br, rows)

    idx, gate, cnt = _route(logits_t, _largest_tile(t, 2048))

    sizes = cnt[:, 0].astype(jnp.int32)
    padded = (sizes + MOE_BLOCK - 1) // MOE_BLOCK * MOE_BLOCK
    pends = jnp.cumsum(padded)
    pstart = (pends - padded).astype(jnp.int32)
    n_blocks = (2 * t) // MOE_BLOCK + N_EXPERTS
    n_used = (pends[-1:] // MOE_BLOCK).astype(jnp.int32)
    block_row0 = jnp.arange(n_blocks, dtype=jnp.int32) * MOE_BLOCK
    block_expert = jnp.minimum(jnp.sum(pends[None, :] <= block_row0[:, None], axis=1), N_EXPERTS - 1).astype(jnp.int32)

    tok_tile = _largest_tile(t, 256)
    slots = _slots(pstart, idx, _largest_tile(t, 2048))
    xs = _dispatch(pstart, sizes, slots, hn, n_blocks * MOE_BLOCK, tok_tile)
    yb = _experts(block_expert, n_used, sizes, xs, w1[0], w3[0], w2[0])
    out = _combine(slots, gate.T, h2, norm_final.reshape(1, D_MODEL), yb, tok_tile)
    return out.reshape(bsz, seq, d)
```

```python
import functools

import jax
import jax.numpy as jnp
import numpy as np
from jax import lax
from jax.experimental import pallas as pl
from jax.experimental.pallas import tpu as pltpu

F32 = jnp.float32
BF16 = jnp.bfloat16

D_MODEL = 1024
CHUNK = 64
N_META = 16
EPS = 1e-6
DN_HEADS = 8
DN_HEAD_DIM = 128
DN_QK = DN_HEADS * DN_HEAD_DIM
DN_V = DN_HEADS * DN_HEAD_DIM
CONV_WIDTH = 4
GLA_HEADS = 4
GLA_KEY_DIM = 128
GLA_VAL_DIM = 256
GLA_QK = GLA_HEADS * GLA_KEY_DIM
GLA_V = GLA_HEADS * GLA_VAL_DIM
GLA_RANK = 16
GLA_TAU = 16.0
N_GROUPS = 8
EXPERTS_PER_GROUP = 8
N_EXPERTS = N_GROUPS * EXPERTS_PER_GROUP
D_EXPERT = 512
MOE_BLOCK = 512
ZERO_ROWS = 256

LANES = 128
SUBLANES = 8
SMALL_W = LANES
HIST_ROWS = 8
VMEM_LIMIT = 56 * 1024 * 1024

_NT = (((1,), (1,)), ((), ()))
_TN = (((0,), (0,)), ((), ()))


def _bdot(a, b):
    return jnp.dot(a.astype(BF16), b.astype(BF16), preferred_element_type=F32)


def _bdot_nt(a, b):
    return lax.dot_general(a.astype(BF16), b.astype(BF16), _NT, preferred_element_type=F32)


def _bdot_tn(a, b):
    return lax.dot_general(a.astype(BF16), b.astype(BF16), _TN, preferred_element_type=F32)


def _sigmoid(x):
    return 0.5 + 0.5 * jnp.tanh(0.5 * x)


def _silu(x):
    half = 0.5 * x
    return half + half * jnp.tanh(half)


def _softplus(x):
    return jnp.maximum(x, 0.0) + jnp.log(1.0 + jnp.exp(-jnp.abs(x)))


def _rms(x, eps=EPS):
    return x * lax.rsqrt(jnp.mean(x * x, axis=-1, keepdims=True) + eps)


def _tri_incl(n):
    r = lax.broadcasted_iota(jnp.int32, (n, n), 0)
    c = lax.broadcasted_iota(jnp.int32, (n, n), 1)
    return r, c


def _bf16_part(x):
    bits = pltpu.bitcast(x, jnp.uint32) & jnp.uint32(0xFFFF0000)
    return pltpu.bitcast(bits, F32)


def _cumsum_rows(x):
    r, c = _tri_incl(x.shape[0])
    tri = (r >= c).astype(F32)
    hi = _bf16_part(x)
    r1 = x - hi
    mid = _bf16_part(r1)
    lo = r1 - mid
    return (jnp.dot(tri, hi, preferred_element_type=F32) + jnp.dot(tri, mid, preferred_element_type=F32)
            + jnp.dot(tri, lo, preferred_element_type=F32))


IN_GROUPS = ((3 * DN_QK, BF16), (DN_V, BF16), (2 * GLA_QK, BF16), (GLA_V, BF16), (GLA_V, BF16),
             (2 * D_MODEL, BF16), (SMALL_W, F32))


CONV_COLS = 256
PLAIN_COLS = 256


def _inproj_body(tiles_per_seq, x_ref, g_ref, w_ref, cw_ref, hist_ref, *refs):
    o_refs, tail_ref, pre, carry = refs[:-3], refs[-3], refs[-2], refs[-1]
    rows = x_ref.shape[0]

    @pl.when(pl.program_id(0) % tiles_per_seq == 0)
    def _():
        carry[...] = hist_ref[...]

    x = x_ref[...]
    xb = (_rms(x) * g_ref[...]).astype(BF16)

    qkv_ref = o_refs[0]

    def conv_pass(c0):
        cols = slice(c0, c0 + CONV_COLS)
        pre[0:HIST_ROWS, :] = carry[:, cols]
        pre[HIST_ROWS:HIST_ROWS + rows, :] = jnp.dot(xb, w_ref[:, cols], preferred_element_type=F32)
        carry[:, cols] = pre[rows:rows + HIST_ROWS, :]
        acc = None
        for j in range(CONV_WIDTH):
            lo = HIST_ROWS - (CONV_WIDTH - 1) + j
            term = pre[lo:lo + rows, :] * cw_ref[j:j + 1, cols]
            acc = term if acc is None else acc + term
        qkv_ref[:, cols] = _silu(acc).astype(qkv_ref.dtype)

    def plain_pass(o_ref, c0, width, wcol):
        o_ref[:, c0:c0 + width] = jnp.dot(xb, w_ref[:, wcol + c0:wcol + c0 + width],
                                          preferred_element_type=F32).astype(o_ref.dtype)

    plain_tasks = []
    wcol = qkv_ref.shape[1]
    for o_ref in o_refs[1:]:
        width = o_ref.shape[1]
        step = min(width, PLAIN_COLS)
        plain_tasks += [functools.partial(plain_pass, o_ref, c0, step, wcol) for c0 in range(0, width, step)]
        wcol += width

    conv_tasks = [functools.partial(conv_pass, c0) for c0 in range(0, qkv_ref.shape[1], CONV_COLS)]
    n_conv, n_plain = len(conv_tasks), len(plain_tasks)
    for i, conv_task in enumerate(conv_tasks):
        for task in plain_tasks[i * n_plain // n_conv:(i + 1) * n_plain // n_conv]:
            task()
        conv_task()
    tail_ref[...] = carry[...]


def _in_proj(x2d, gain, w_all, conv_w, hist, rows, tiles_per_seq):
    t = x2d.shape[0]
    const = lambda a: pl.BlockSpec(a.shape, lambda i: (0, 0))
    in_specs = [pl.BlockSpec((rows, D_MODEL), lambda i: (i, 0)), const(gain),
                pl.BlockSpec(w_all.shape, lambda i: (0, 0), pipeline_mode=pl.Buffered(1)),
                const(conv_w), const(hist)]
    out_specs = [pl.BlockSpec((rows, w), lambda i: (i, 0)) for w, _ in IN_GROUPS] + [const(hist)]
    out_shape = [jax.ShapeDtypeStruct((t, w), dt) for w, dt in IN_GROUPS] + [jax.ShapeDtypeStruct(hist.shape, F32)]
    return pl.pallas_call(
        functools.partial(_inproj_body, tiles_per_seq), grid=(t // rows,),
        in_specs=in_specs, out_specs=out_specs, out_shape=out_shape,
        scratch_shapes=[pltpu.VMEM((HIST_ROWS + rows, CONV_COLS), F32), pltpu.VMEM(hist.shape, F32)],
        compiler_params=pltpu.CompilerParams(dimension_semantics=("arbitrary",), vmem_limit_bytes=VMEM_LIMIT),
        name="in_proj",
    )(x2d, gain, w_all, conv_w, hist)


def _block_diag2(m2):
    n = m2.shape[0]
    lane = lax.broadcasted_iota(jnp.int32, m2.shape, 1)
    return jnp.concatenate([jnp.where(lane < n, m2, 0.0), jnp.where(lane >= n, m2, 0.0)], axis=0)


def _unit_lower_inverse(mats):
    n = mats[0].shape[0]
    r = lax.broadcasted_iota(jnp.int32, (n, 2 * n), 0)
    c = lax.broadcasted_iota(jnp.int32, (n, 2 * n), 1) & (n - 1)
    eye = (r == c).astype(F32)
    pair = (r >> 1) == (c >> 1)
    ts = [eye - jnp.where(pair, a, 0.0) for a in mats]
    m = 2
    while m < n:
        sh = m.bit_length() - 1
        keep = ((r >> (sh + 1)) == (c >> (sh + 1))) & ((r >> sh) != (c >> sh))
        ams = [_block_diag2(jnp.where(keep, a, 0.0)) for a in mats]
        xs = [jnp.dot(t, am, preferred_element_type=F32) for t, am in zip(ts, ams)]
        ys = [jnp.dot(x, _block_diag2(t), preferred_element_type=F32) for x, t in zip(xs, ts)]
        ts = [t - y for t, y in zip(ts, ys)]
        m *= 2
    return ts


def _l2_normalise(xs, ones_bd, scale):
    out = []
    for p in range(0, len(xs), 2):
        sq = jnp.concatenate([xs[p] * xs[p], xs[p + 1] * xs[p + 1]], axis=1)
        ss = jnp.dot(sq, ones_bd, preferred_element_type=F32)
        inv = lax.rsqrt(ss + EPS) * scale
        out.append(xs[p] * inv[:, :DN_HEAD_DIM])
        out.append(xs[p + 1] * inv[:, DN_HEAD_DIM:])
    return out


DN_WAVE = 32


def _dn_body(qkv_ref, sm_ref, z_ref, s0_ref, alog_ref, dtb_ref, nh_ref, ones_ref, o_ref, sfin_ref, s_scr):
    c_id = pl.program_id(1)
    n_rows = qkv_ref.shape[0]
    all_items = [(b, h) for b in range(n_rows) for h in range(DN_HEADS)]

    @pl.when(c_id == 0)
    def _():
        for b in range(n_rows):
            s_scr[b] = s0_ref[...]

    def head_cols(b, col):
        return qkv_ref[b, :, col:col + DN_HEAD_DIM].astype(F32)

    gc_all, gc_t, beta_all = [], [], []
    for b in range(n_rows):
        sm = sm_ref[b]
        g_b = -jnp.exp(alog_ref[...]) * _softplus(sm + dtb_ref[...])
        beta_all.append(_sigmoid(sm))
        gc_b = _cumsum_rows(g_b)
        gc_all.append(gc_b)
        gc_t.append(jnp.concatenate([gc_b, jnp.zeros_like(gc_b)], axis=0).T)

    r2 = lax.broadcasted_iota(jnp.int32, (CHUNK, 2 * CHUNK), 0)
    lane2 = lax.broadcasted_iota(jnp.int32, (CHUNK, 2 * CHUNK), 1)
    c2 = lane2 & (CHUNK - 1)
    first = lane2 < CHUNK
    causal2 = r2 >= c2
    strict2 = r2 > c2
    scale = DN_HEAD_DIM ** -0.5
    ones_bd = ones_ref[...]

    def wave(items):
        n_items = range(len(items))
        q = [head_cols(b, h * DN_HEAD_DIM) for b, h in items]
        k = [head_cols(b, DN_QK + h * DN_HEAD_DIM) for b, h in items]
        v = [head_cols(b, 2 * DN_QK + h * DN_HEAD_DIM) for b, h in items]
        q = _l2_normalise(q, ones_bd, scale)
        k = _l2_normalise(k, ones_bd, 1.0)
        gcol = [gc_all[b][:, h:h + 1] for b, h in items]
        bcol = [beta_all[b][:, DN_HEADS + h:DN_HEADS + h + 1] for b, h in items]
        glast = [gc_all[b][CHUNK - 1:CHUNK, h:h + 1] for b, h in items]
        eg = [jnp.exp(gcol[i]) for i in n_items]

        pairs = range(0, len(items), 2)
        n_pairs = range(len(pairs))

        def diag2(x0, x1):
            return jnp.concatenate([jnp.concatenate([x0, jnp.zeros_like(x1)], axis=1),
                                    jnp.concatenate([jnp.zeros_like(x0), x1], axis=1)], axis=0)

        def lanes2(x0, x1):
            return jnp.where(first, x0, x1)

        def head_row(i):
            b, h = items[i]
            return gc_t[b][h:h + 1, :]

        k_bd = [diag2(k[p], k[p + 1]) for p in pairs]
        k_cat = [jnp.concatenate([k[p], k[p + 1]], axis=1) for p in pairs]
        q_cat = [jnp.concatenate([q[p], q[p + 1]], axis=1) for p in pairs]
        kk2 = [lax.dot_general(k_cat[j], k_bd[j], _NT, preferred_element_type=F32) for j in n_pairs]
        qk2 = [lax.dot_general(q_cat[j], k_bd[j], _NT, preferred_element_type=F32) for j in n_pairs]
        grow2 = [head_row(p) + pltpu.roll(head_row(p + 1), CHUNK, axis=1) for p in pairs]
        gcol2 = [lanes2(gcol[p], gcol[p + 1]) for p in pairs]
        bcol2 = [lanes2(bcol[p], bcol[p + 1]) for p in pairs]
        decay2 = [jnp.where(causal2, jnp.exp(jnp.where(causal2, gcol2[j] - grow2[j], 0.0)), 0.0) for j in n_pairs]
        a_mat2 = [jnp.where(strict2, bcol2[j] * kk2[j] * decay2[j], 0.0) for j in n_pairs]
        qk2 = [jnp.where(causal2, qk2[j] * decay2[j], 0.0) for j in n_pairs]
        t_inv2 = _unit_lower_inverse(a_mat2)
        rhs = [jnp.concatenate([v[i] * bcol[i], k[i] * (bcol[i] * eg[i])], axis=1) for i in n_items]
        sol2 = [jnp.dot(t_inv2[j], diag2(rhs[p], rhs[p + 1]), preferred_element_type=F32)
                for j, p in enumerate(pairs)]
        sol = [sol2[i // 2][:, (i % 2) * 2 * DN_HEAD_DIM:(i % 2 + 1) * 2 * DN_HEAD_DIM] for i in n_items]
        s_old = [s_scr[b, h] for b, h in items]
        qe = [q[i] * eg[i] for i in n_items]
        k_dec = [k[i] * jnp.exp(glast[i] - gcol[i]) for i in n_items]
        ws = [jnp.dot(sol[i][:, DN_HEAD_DIM:], s_old[i], preferred_element_type=F32) for i in n_items]
        o_inter = [jnp.dot(qe[i], s_old[i], preferred_element_type=F32) for i in n_items]
        v_new = [sol[i][:, :DN_HEAD_DIM] - ws[i] for i in n_items]
        o_intra2 = [jnp.dot(qk2[j], diag2(v_new[p], v_new[p + 1]), preferred_element_type=F32)
                    for j, p in enumerate(pairs)]
        o_intra = [o_intra2[i // 2][:, (i % 2) * DN_HEAD_DIM:(i % 2 + 1) * DN_HEAD_DIM] for i in n_items]
        s_add = [lax.dot_general(k_dec[i], v_new[i], _TN, preferred_element_type=F32) for i in n_items]
        for i, (b, h) in enumerate(items):
            col = h * DN_HEAD_DIM
            s_scr[b, h] = s_old[i] * jnp.exp(glast[i]) + s_add[i]
            zed = z_ref[b, :, col:col + DN_HEAD_DIM].astype(F32)
            o = o_inter[i] + o_intra[i]
            o_ref[b, :, col:col + DN_HEAD_DIM] = (_rms(o) * nh_ref[...] * _silu(zed)).astype(o_ref.dtype)

    for w0 in range(0, len(all_items), DN_WAVE):
        wave(all_items[w0:w0 + DN_WAVE])

    @pl.when(c_id == pl.num_programs(1) - 1)
    def _():
        sfin_ref[...] = s_scr[...]


def _rows_per_step(b, want):
    while b % want:
        want //= 2
    return want


def _dn_chunk(qkv, small, z, s0, alog, dtb, nh):
    b, l, _ = qkv.shape
    lane_head = np.arange(2 * DN_HEAD_DIM) // DN_HEAD_DIM
    ones_bd = jnp.asarray(lane_head[:, None] == lane_head[None, :], F32)
    nc = l // CHUNK
    nr = _rows_per_step(b, 8)
    const2 = lambda bi, ci: (0, 0)
    blk = lambda w: pl.BlockSpec((nr, CHUNK, w), lambda bi, ci: (bi, ci, 0))
    return pl.pallas_call(
        _dn_body, grid=(b // nr, nc),
        in_specs=[
            blk(3 * DN_QK), blk(SMALL_W), blk(DN_V),
            pl.BlockSpec(s0.shape, lambda bi, ci: (0, 0, 0)),
            pl.BlockSpec(alog.shape, const2),
            pl.BlockSpec(dtb.shape, const2),
            pl.BlockSpec(nh.shape, const2),
            pl.BlockSpec(ones_bd.shape, const2),
        ],
        out_specs=[
            blk(DN_V),
            pl.BlockSpec((nr, DN_HEADS, DN_HEAD_DIM, DN_HEAD_DIM), lambda bi, ci: (bi, 0, 0, 0)),
        ],
        out_shape=[
            jax.ShapeDtypeStruct((b, l, DN_V), BF16),
            jax.ShapeDtypeStruct((b, DN_HEADS, DN_HEAD_DIM, DN_HEAD_DIM), F32),
        ],
        scratch_shapes=[pltpu.VMEM((nr, DN_HEADS, DN_HEAD_DIM, DN_HEAD_DIM), F32)],
        compiler_params=pltpu.CompilerParams(dimension_semantics=("parallel", "arbitrary"),
                                             vmem_limit_bytes=VMEM_LIMIT),
        name="dn_chunk",
    )(qkv, small, z, s0, alog, dtb, nh, ones_bd)


def _gla_body(qk_ref, v_ref, r_ref, sm_ref, wa_ref, ba_ref, nh_ref, s0_ref, o_ref, sfin_ref, s_scr):
    c_id = pl.program_id(1)
    n_rows = qk_ref.shape[0]
    items = [(b, h) for b in range(n_rows) for h in range(GLA_HEADS)]
    n_items = range(len(items))

    @pl.when(c_id == 0)
    def _():
        for b in range(n_rows):
            s_scr[b] = s0_ref[...]

    b_all = []
    for b in range(n_rows):
        la = jnp.dot(sm_ref[b], wa_ref[...], preferred_element_type=F32) + ba_ref[...]
        log_alpha = (jnp.minimum(la, 0.0) - jnp.log(1.0 + jnp.exp(-jnp.abs(la)))) * (1.0 / GLA_TAU)
        b_all.append(_cumsum_rows(log_alpha))

    r, c = _tri_incl(CHUNK)
    causal = r >= c
    scale = GLA_KEY_DIM ** -0.5
    mid = CHUNK // 2 - 1

    q = [qk_ref[b, :, h * GLA_KEY_DIM:(h + 1) * GLA_KEY_DIM].astype(F32) * scale for b, h in items]
    k = [qk_ref[b, :, GLA_QK + h * GLA_KEY_DIM:GLA_QK + (h + 1) * GLA_KEY_DIM].astype(F32) for b, h in items]
    v = [v_ref[b, :, h * GLA_VAL_DIM:(h + 1) * GLA_VAL_DIM].astype(F32) for b, h in items]
    bh = [b_all[b][:, h * GLA_KEY_DIM:(h + 1) * GLA_KEY_DIM] for b, h in items]
    bmid = [x[mid:mid + 1, :] for x in bh]
    blast = [x[CHUNK - 1:CHUNK, :] for x in bh]
    qs = [q[i] * jnp.exp(bh[i] - bmid[i]) for i in n_items]
    ks = [k[i] * jnp.exp(bmid[i] - bh[i]) for i in n_items]
    qd = [q[i] * jnp.exp(bh[i]) for i in n_items]
    kd = [k[i] * jnp.exp(blast[i] - bh[i]) for i in n_items]
    st = [s_scr[b, h] for b, h in items]
    att = [lax.dot_general(qs[i], ks[i], _NT, preferred_element_type=F32) for i in n_items]
    o_inter = [lax.dot_general(qd[i], st[i], _NT, preferred_element_type=F32) for i in n_items]
    s_add = [lax.dot_general(v[i], kd[i], _TN, preferred_element_type=F32) for i in n_items]
    att = [jnp.where(causal, a, 0.0) for a in att]
    o_intra = [jnp.dot(att[i], v[i], preferred_element_type=F32) for i in n_items]
    for i, (b, h) in enumerate(items):
        vc = h * GLA_VAL_DIM
        s_scr[b, h] = st[i] * jnp.exp(blast[i]) + s_add[i]
        gate = _silu(r_ref[b, :, vc:vc + GLA_VAL_DIM].astype(F32))
        o = o_inter[i] + o_intra[i]
        o_ref[b, :, vc:vc + GLA_VAL_DIM] = (_rms(o) * nh_ref[...] * gate).astype(o_ref.dtype)

    @pl.when(c_id == pl.num_programs(1) - 1)
    def _():
        sfin_ref[...] = s_scr[...]


def _gla_chunk(qk, v, rr, small, wa, ba, nh, s0):
    b, l, _ = qk.shape
    nc = l // CHUNK
    nr = _rows_per_step(b, 8)
    const2 = lambda bi, ci: (0, 0)
    blk = lambda w: pl.BlockSpec((nr, CHUNK, w), lambda bi, ci: (bi, ci, 0))
    return pl.pallas_call(
        _gla_body, grid=(b // nr, nc),
        in_specs=[
            blk(2 * GLA_QK), blk(GLA_V), blk(GLA_V), blk(SMALL_W),
            pl.BlockSpec(wa.shape, const2), pl.BlockSpec(ba.shape, const2), pl.BlockSpec(nh.shape, const2),
            pl.BlockSpec(s0.shape, lambda bi, ci: (0, 0, 0)),
        ],
        out_specs=[
            blk(GLA_V),
            pl.BlockSpec((nr, GLA_HEADS, GLA_VAL_DIM, GLA_KEY_DIM), lambda bi, ci: (bi, 0, 0, 0)),
        ],
        out_shape=[
            jax.ShapeDtypeStruct((b, l, GLA_V), BF16),
            jax.ShapeDtypeStruct((b, GLA_HEADS, GLA_VAL_DIM, GLA_KEY_DIM), F32),
        ],
        scratch_shapes=[pltpu.VMEM((nr, GLA_HEADS, GLA_VAL_DIM, GLA_KEY_DIM), F32)],
        compiler_params=pltpu.CompilerParams(dimension_semantics=("parallel", "arbitrary"),
                                             vmem_limit_bytes=VMEM_LIMIT),
        name="gla_chunk",
    )(qk, v, rr, small, wa, ba, nh, s0)


TOK_SUB = D_MODEL // LANES
assert TOK_SUB == SUBLANES


def _tiles_to_rows(ref, n):
    return jnp.concatenate([ref[pl.ds(s, n, stride=TOK_SUB), :] for s in range(TOK_SUB)], axis=1)


def _rows_to_tiles(ref, value):
    n = value.shape[0]
    for s in range(TOK_SUB):
        ref[pl.ds(s, n, stride=TOK_SUB), :] = value[:, s * LANES:(s + 1) * LANES]


def _tile_rows(tok, count=1):
    return pl.ds(pl.multiple_of(tok * TOK_SUB, TOK_SUB), count * TOK_SUB)


def _outproj_body(odn_ref, ogla_ref, gates_ref, x_ref, wd_ref, wg_ref, wo_ref, gn_ref, wr_ref, br_ref,
                  h2_ref, hn_ref, lg_ref):
    y_dn = jnp.dot(odn_ref[...], wd_ref[...], preferred_element_type=F32)
    y_gla = jnp.dot(ogla_ref[...], wg_ref[...], preferred_element_type=F32)
    gd = _sigmoid(gates_ref[:, 0:D_MODEL].astype(F32))
    gg = _sigmoid(gates_ref[:, D_MODEL:2 * D_MODEL].astype(F32))
    merged = gd * y_dn + gg * y_gla
    h2 = x_ref[...] + _bdot(merged, wo_ref[...])
    h2_ref[...] = h2
    hn = _rms(h2) * gn_ref[...]
    _rows_to_tiles(hn_ref, hn)
    wr = wr_ref[...]
    wr_hi, hn_hi = _bf16_part(wr), _bf16_part(hn)
    nt = lambda a, b: lax.dot_general(a, b, _NT, preferred_element_type=F32)
    lg_ref[...] = nt(wr_hi, hn_hi) + nt(wr_hi, hn - hn_hi) + nt(wr - wr_hi, hn_hi) + br_ref[...]


def _out_proj(o_dn, o_gla, gates, x2d, wd, wg, wo, gn, wr_t, br, rows):
    t = x2d.shape[0]
    row_blk = lambda w: pl.BlockSpec((rows, w), lambda i: (i, 0))
    const = lambda a: pl.BlockSpec(a.shape, lambda i: (0, 0))
    return pl.pallas_call(
        _outproj_body, grid=(t // rows,),
        in_specs=[row_blk(DN_V), row_blk(GLA_V), row_blk(2 * D_MODEL), row_blk(D_MODEL),
                  const(wd), const(wg), const(wo), const(gn), const(wr_t), const(br)],
        out_specs=[row_blk(D_MODEL), pl.BlockSpec((rows * TOK_SUB, LANES), lambda i: (i, 0)),
                   pl.BlockSpec((LANES, rows), lambda i: (0, i))],
        out_shape=[jax.ShapeDtypeStruct((t, D_MODEL), F32), jax.ShapeDtypeStruct((t * TOK_SUB, LANES), F32),
                   jax.ShapeDtypeStruct((LANES, t), F32)],
        compiler_params=pltpu.CompilerParams(dimension_semantics=("parallel",), vmem_limit_bytes=VMEM_LIMIT),
        name="out_proj",
    )(o_dn, o_gla, gates, x2d, wd, wg, wo, gn, wr_t, br)


ROUTE_SUB = 256


def _route_body(lg_ref, idx_ref, gate_ref, cnt_ref, carry):
    step = pl.program_id(0)

    @pl.when(step == 0)
    def _():
        carry[...] = jnp.zeros_like(carry)

    tt = lg_ref.shape[1]
    gl = lg_ref[0:N_GROUPS, :]
    gmax = jnp.max(gl, axis=0, keepdims=True)
    rid8 = lax.broadcasted_iota(jnp.int32, (N_GROUPS, tt), 0)
    gsel = jnp.min(jnp.where(gl == gmax, rid8, N_GROUPS), axis=0, keepdims=True)
    gw = 1.0 / jnp.sum(jnp.exp(gl - gmax), axis=0, keepdims=True)
    el = lg_ref[N_GROUPS:N_GROUPS + N_EXPERTS, :]
    rid = lax.broadcasted_iota(jnp.int32, (N_EXPERTS, tt), 0)
    neg = jnp.float32(-jnp.inf)
    ein = jnp.where((rid >> 3) == gsel, el, neg)
    t1 = jnp.max(ein, axis=0, keepdims=True)
    i1 = jnp.min(jnp.where(ein == t1, rid, N_EXPERTS), axis=0, keepdims=True)
    ein2 = jnp.where(rid == i1, neg, ein)
    t2 = jnp.max(ein2, axis=0, keepdims=True)
    i2 = jnp.min(jnp.where(ein2 == t2, rid, N_EXPERTS), axis=0, keepdims=True)
    e21 = jnp.exp(t2 - t1)
    den = 1.0 / (1.0 + e21)
    sel1 = rid == i1
    sel2 = rid == i2
    onehot = jnp.where(sel1 | sel2, 1.0, 0.0)

    ur, uc = _tri_incl(ROUTE_SUB)
    upper = (ur <= uc).astype(BF16)
    run = carry[...]
    r1_parts, r2_parts = [], []
    for s in range(tt // ROUTE_SUB):
        sl = slice(s * ROUTE_SUB, (s + 1) * ROUTE_SUB)
        oh = onehot[:, sl]
        incl = jnp.dot(oh.astype(BF16), upper, preferred_element_type=F32) + run
        excl = incl - oh
        r1_parts.append(jnp.sum(jnp.where(sel1[:, sl], excl, 0.0), axis=0, keepdims=True))
        r2_parts.append(jnp.sum(jnp.where(sel2[:, sl], excl, 0.0), axis=0, keepdims=True))
        run = jnp.broadcast_to(incl[:, ROUTE_SUB - 1:ROUTE_SUB], run.shape)
    carry[...] = run
    r1 = jnp.concatenate(r1_parts, axis=1) if len(r1_parts) > 1 else r1_parts[0]
    r2 = jnp.concatenate(r2_parts, axis=1) if len(r2_parts) > 1 else r2_parts[0]

    idx_ref[...] = jnp.zeros_like(idx_ref)
    idx_ref[0:1, :] = i1
    idx_ref[1:2, :] = i2
    idx_ref[2:3, :] = r1.astype(jnp.int32)
    idx_ref[3:4, :] = r2.astype(jnp.int32)
    gate_ref[...] = jnp.zeros_like(gate_ref)
    gate_ref[0:1, :] = den * gw
    gate_ref[1:2, :] = e21 * den * gw
    cnt_ref[...] = run[:, 0:LANES]


def _route(logits_t, lanes):
    t = logits_t.shape[1]
    return pl.pallas_call(
        _route_body, grid=(t // lanes,),
        in_specs=[pl.BlockSpec((LANES, lanes), lambda i: (0, i))],
        out_specs=[pl.BlockSpec((8, lanes), lambda i: (0, i)), pl.BlockSpec((8, lanes), lambda i: (0, i)),
                   pl.BlockSpec((N_EXPERTS, LANES), lambda i: (0, 0))],
        out_shape=[jax.ShapeDtypeStruct((8, t), jnp.int32), jax.ShapeDtypeStruct((8, t), F32),
                   jax.ShapeDtypeStruct((N_EXPERTS, LANES), F32)],
        scratch_shapes=[pltpu.VMEM((N_EXPERTS, ROUTE_SUB), F32)],
        compiler_params=pltpu.CompilerParams(dimension_semantics=("arbitrary",)),
        name="route",
    )(logits_t)


def _row_copy(src_ref, src_tok, dst_ref, dst_tok, sem):
    return pltpu.make_async_copy(src_ref.at[_tile_rows(src_tok)], dst_ref.at[_tile_rows(dst_tok)], sem)


ROW_UNROLL = 8


def _slots_body(pstart_ref, idx_ref, slot_ref):
    idx = idx_ref[...]
    base = jnp.zeros_like(idx)
    for e in range(N_EXPERTS):
        base = jnp.where(idx == e, pstart_ref[e], base)
    slot_ref[...] = jnp.zeros_like(idx)
    slot_ref[0:2, :] = base[0:2, :] + idx[2:4, :]


def _slots(pstart, idx, lanes):
    t = idx.shape[1]
    grid_spec = pltpu.PrefetchScalarGridSpec(
        num_scalar_prefetch=1, grid=(t // lanes,),
        in_specs=[pl.BlockSpec((8, lanes), lambda i, ps: (0, i))],
        out_specs=pl.BlockSpec((8, lanes), lambda i, ps: (0, i)),
    )
    return pl.pallas_call(
        _slots_body, grid_spec=grid_spec, out_shape=jax.ShapeDtypeStruct(idx.shape, jnp.int32),
        compiler_params=pltpu.CompilerParams(dimension_semantics=("parallel",)),
        name="slots",
    )(pstart, idx)


DISPATCH_BUFS = 3


def _dispatch_body(pstart_ref, size_ref, slot_ref, hn_hbm, xs_ref, zblk, hbuf, sem, load_sems, row_sems):
    step = pl.program_id(0)
    n_steps = pl.num_programs(0)
    n_tok = hbuf.shape[1] // TOK_SUB

    def load(tile):
        buf = tile % DISPATCH_BUFS
        return pltpu.make_async_copy(hn_hbm.at[_tile_rows(tile * n_tok, n_tok)], hbuf.at[buf], load_sems.at[buf])

    def wait_rows(parity):
        for k in range(2):
            pltpu.make_async_copy(hbuf.at[0], xs_ref.at[_tile_rows(0, n_tok)], row_sems.at[parity]).wait()

    @pl.when(step == 0)
    def _():
        load(0).start()

        @pl.when(n_steps > 1)
        def _():
            load(1).start()

        zblk[...] = jnp.zeros_like(zblk)
        parts = MOE_BLOCK // ZERO_ROWS

        def clear_copy(row0, part):
            return pltpu.make_async_copy(zblk, xs_ref.at[_tile_rows(row0 + part * ZERO_ROWS, ZERO_ROWS)], sem)

        def per_expert(e, n_rows):
            size = size_ref[e]
            full = size // MOE_BLOCK * MOE_BLOCK

            @pl.when(size != full)
            def _():
                for part in range(parts):
                    clear_copy(pstart_ref[e] + full, part).start()
                for part in range(parts):
                    clear_copy(pstart_ref[e] + full, part).wait()

            return n_rows + (size + MOE_BLOCK - 1) // MOE_BLOCK * MOE_BLOCK

        n_rows = lax.fori_loop(0, N_EXPERTS, per_expert, 0)

        def clear_unused(j, c):
            for part in range(parts):
                clear_copy(j * MOE_BLOCK, part).start()
            for part in range(parts):
                clear_copy(j * MOE_BLOCK, part).wait()
            return c

        lax.fori_loop(n_rows // MOE_BLOCK, xs_ref.shape[0] // (MOE_BLOCK * TOK_SUB), clear_unused, 0)

    load(step).wait()
    src = hbuf.at[step % DISPATCH_BUFS]
    row_sem = row_sems.at[step % 2]

    def issue(t, carry):
        for k in range(2):
            _row_copy(src, t, xs_ref, slot_ref[k, t], row_sem).start(priority=k)
        return carry

    lax.fori_loop(0, n_tok, issue, 0, unroll=ROW_UNROLL)

    @pl.when(step > 0)
    def _():
        wait_rows((step - 1) % 2)

    @pl.when(step + 2 < n_steps)
    def _():
        load(step + 2).start()

    @pl.when(step == n_steps - 1)
    def _():
        wait_rows(step % 2)


def _dispatch(pstart, sizes, slots, hn, n_rows_out, tokens):
    t = hn.shape[0] // TOK_SUB
    grid_spec = pltpu.PrefetchScalarGridSpec(
        num_scalar_prefetch=2, grid=(t // tokens,),
        in_specs=[pl.BlockSpec((8, tokens), lambda i, ps, sz: (0, i), memory_space=pltpu.SMEM),
                  pl.BlockSpec(memory_space=pl.ANY)],
        out_specs=pl.BlockSpec(memory_space=pl.ANY),
        scratch_shapes=[pltpu.VMEM((ZERO_ROWS * TOK_SUB, LANES), F32),
                        pltpu.VMEM((DISPATCH_BUFS, tokens * TOK_SUB, LANES), F32),
                        pltpu.SemaphoreType.DMA(()), pltpu.SemaphoreType.DMA((DISPATCH_BUFS,)),
                        pltpu.SemaphoreType.DMA((2,))],
    )
    return pl.pallas_call(
        _dispatch_body, grid_spec=grid_spec,
        out_shape=jax.ShapeDtypeStruct((n_rows_out * TOK_SUB, LANES), F32),
        compiler_params=pltpu.CompilerParams(dimension_semantics=("arbitrary",)),
        name="dispatch",
    )(pstart, sizes, slots, hn)


def _experts_body(be_ref, nused_ref, par_ref, nxt_ref, nxt2_ref, xs_ref, w1_hbm, w3_hbm, w2_hbm, yb_ref,
                  w1f, w3f, w2f, w1b, w3b, w2b, sems):
    j = pl.program_id(0)
    used = j < nused_ref[0]
    expert = be_ref[j]
    slot = par_ref[j]
    new_expert = jnp.logical_or(j == 0, expert != be_ref[jnp.maximum(j - 1, 0)])

    def fetch(e, s):
        return [pltpu.make_async_copy(w_hbm.at[e], w_vmem.at[s], sems.at[s])
                for w_hbm, w_vmem in ((w1_hbm, w1f), (w3_hbm, w3f), (w2_hbm, w2f))]

    @pl.when(j == 0)
    def _():
        for copy in fetch(expert, slot):
            copy.start()

        @pl.when(nxt_ref[0] >= 0)
        def _():
            for copy in fetch(nxt_ref[0], 1 - slot):
                copy.start()

    @pl.when(jnp.logical_and(used, new_expert))
    def _():
        for copy in fetch(expert, slot):
            copy.wait()
        w1b[...] = w1f[slot].astype(BF16)
        w3b[...] = w3f[slot].astype(BF16)
        w2b[...] = w2f[slot].astype(BF16)

        @pl.when(nxt2_ref[j] >= 0)
        def _():
            for copy in fetch(nxt2_ref[j], slot):
                copy.start()

    @pl.when(used)
    def _():
        xb = _tiles_to_rows(xs_ref, MOE_BLOCK).astype(BF16)
        h1 = jnp.dot(xb, w1b[...], preferred_element_type=F32)
        h3 = jnp.dot(xb, w3b[...], preferred_element_type=F32)
        hid = (_silu(h1) * h3).astype(BF16)
        _rows_to_tiles(yb_ref, jnp.dot(hid, w2b[...], preferred_element_type=F32))

    @pl.when(jnp.logical_not(used))
    def _():
        yb_ref[...] = jnp.zeros_like(yb_ref)


def _experts(block_expert, n_used, sizes, xs, w1, w3, w2):
    p = xs.shape[0] // TOK_SUB
    nb = p // MOE_BLOCK
    owns = sizes > 0
    ordinal = jnp.cumsum(owns) - owns
    ids = jnp.arange(N_EXPERTS, dtype=jnp.int32)
    later = jnp.where((ids[None, :] > ids[:, None]) & owns[None, :], ids[None, :], N_EXPERTS)
    nxt_e = jnp.min(later, axis=1)
    nxt_e = jnp.where(nxt_e == N_EXPERTS, -1, nxt_e).astype(jnp.int32)

    def lookup(table, index):
        hit = index[:, None] == ids[None, :]
        return jnp.sum(jnp.where(hit, table[None, :] + 1, 0), axis=1).astype(jnp.int32) - 1

    nxt2_e = lookup(nxt_e, nxt_e)
    parity = lookup(ordinal.astype(jnp.int32), block_expert) % 2
    nxt, nxt2 = lookup(nxt_e, block_expert), lookup(nxt2_e, block_expert)

    row_map = lambda j, be, nu, pa, n1, n2: (jnp.minimum(j, nu[0] - 1), 0)
    grid_spec = pltpu.PrefetchScalarGridSpec(
        num_scalar_prefetch=5, grid=(nb,),
        in_specs=[pl.BlockSpec((MOE_BLOCK * TOK_SUB, LANES), row_map),
                  pl.BlockSpec(memory_space=pl.ANY), pl.BlockSpec(memory_space=pl.ANY),
                  pl.BlockSpec(memory_space=pl.ANY)],
        out_specs=pl.BlockSpec((MOE_BLOCK * TOK_SUB, LANES), lambda j, be, nu, pa, n1, n2: (j, 0)),
        scratch_shapes=[pltpu.VMEM((2, D_MODEL, D_EXPERT), F32), pltpu.VMEM((2, D_MODEL, D_EXPERT), F32),
                        pltpu.VMEM((2, D_EXPERT, D_MODEL), F32),
                        pltpu.VMEM((D_MODEL, D_EXPERT), BF16), pltpu.VMEM((D_MODEL, D_EXPERT), BF16),
                        pltpu.VMEM((D_EXPERT, D_MODEL), BF16), pltpu.SemaphoreType.DMA((2,))],
    )
    return pl.pallas_call(
        _experts_body, grid_spec=grid_spec,
        out_shape=jax.ShapeDtypeStruct((p * TOK_SUB, LANES), F32),
        compiler_params=pltpu.CompilerParams(dimension_semantics=("arbitrary",), vmem_limit_bytes=VMEM_LIMIT),
        name="experts",
    )(block_expert, n_used, parity, nxt, nxt2, xs, w1, w3, w2)


def _combine_body(slot_ref, slot_next_ref, gate_ref, h2_ref, gf_ref, yb_ref, o_ref, ybuf, sems):
    n_tok = h2_ref.shape[0]
    step = pl.program_id(0)
    cur = step % 2

    def gather(slots, buf_id):
        def issue(t, carry):
            for k in range(2):
                pltpu.make_async_copy(yb_ref.at[_tile_rows(slots[k, t])], ybuf.at[buf_id, k, _tile_rows(t)],
                                      sems.at[buf_id]).start(priority=k)
            return carry

        lax.fori_loop(0, n_tok, issue, 0, unroll=ROW_UNROLL)

    @pl.when(step == 0)
    def _():
        gather(slot_ref, 0)

    @pl.when(step + 1 < pl.num_programs(0))
    def _():
        gather(slot_next_ref, 1 - cur)

    for k in range(2):
        pltpu.make_async_copy(yb_ref.at[_tile_rows(0, n_tok)], ybuf.at[cur, k], sems.at[cur]).wait()
    y0 = _tiles_to_rows(ybuf.at[cur, 0], n_tok)
    y1 = _tiles_to_rows(ybuf.at[cur, 1], n_tok)
    y = h2_ref[...] + gate_ref[:, 0:1] * y0 + gate_ref[:, 1:2] * y1
    o_ref[...] = _rms(y) * gf_ref[...]


def _combine(slots, gate_t, h2, gf, yb, tokens):
    t = h2.shape[0]
    last = t // tokens - 1
    return pl.pallas_call(
        _combine_body, grid=(t // tokens,),
        in_specs=[pl.BlockSpec((8, tokens), lambda i: (0, i), memory_space=pltpu.SMEM),
                  pl.BlockSpec((8, tokens), lambda i: (0, jnp.minimum(i + 1, last)), memory_space=pltpu.SMEM),
                  pl.BlockSpec((tokens, 8), lambda i: (i, 0)),
                  pl.BlockSpec((tokens, D_MODEL), lambda i: (i, 0)),
                  pl.BlockSpec((1, D_MODEL), lambda i: (0, 0)),
                  pl.BlockSpec(memory_space=pl.ANY)],
        out_specs=pl.BlockSpec((tokens, D_MODEL), lambda i: (i, 0)),
        scratch_shapes=[pltpu.VMEM((2, 2, tokens * TOK_SUB, LANES), F32), pltpu.SemaphoreType.DMA((2,))],
        out_shape=jax.ShapeDtypeStruct((t, D_MODEL), F32),
        compiler_params=pltpu.CompilerParams(dimension_semantics=("arbitrary",)),
        name="combine",
    )(slots, slots, gate_t, h2, gf, yb)


def _pad_lanes(v, width=LANES):
    v = v.reshape(1, -1).astype(F32)
    return jnp.pad(v, ((0, 0), (0, width - v.shape[1])))


def _largest_tile(n, cap):
    t = cap
    while n % t:
        t //= 2
    return t


def kernel(x, meta_tokens, norm_mix, w_in, conv_dn, a_log, dt_bias, norm_head_dn, w_proj_dn, w_alpha, b_alpha,
           norm_head_gla, w_proj_gla, w_out, norm_ffn, w_group, b_group, w_router, b_router, w1, w3, w2, norm_final):
    assert norm_mix.shape[0] == 1, "single-layer block"
    bsz, seq, d = x.shape
    assert d == D_MODEL and seq % CHUNK == 0
    t = bsz * seq

    wi = w_in[0]
    offs = np.cumsum([0, DN_QK, DN_QK, DN_V, DN_V, DN_HEADS, DN_HEADS, GLA_QK, GLA_QK, GLA_V, GLA_V, GLA_RANK,
                      D_MODEL, D_MODEL]).tolist()
    seg = lambda i, j: wi[:, offs[i]:offs[j]]
    n_small = 2 * DN_HEADS + GLA_RANK
    w_all = jnp.concatenate([seg(0, 4), seg(6, 10), seg(11, 13), seg(4, 6), seg(10, 11),
                             jnp.zeros((D_MODEL, SMALL_W - n_small), F32)], axis=1).astype(BF16)
    g_mix = norm_mix[0].reshape(1, D_MODEL)

    conv_w = jnp.pad(conv_dn[0], ((0, 8 - CONV_WIDTH), (0, 0)))
    alog = _pad_lanes(a_log[0])
    dtb = _pad_lanes(dt_bias[0])
    nh_dn = norm_head_dn[0].reshape(1, DN_HEAD_DIM)
    lr_off = 2 * DN_HEADS
    wa = jnp.zeros((SMALL_W, GLA_QK), F32).at[lr_off:lr_off + GLA_RANK].set(w_alpha[0])
    ba = b_alpha[0].reshape(1, GLA_QK)
    nh_gla = norm_head_gla[0].reshape(1, GLA_VAL_DIM)

    def mixers(tokens2d, nb, hist, s0_dn, s0_gla, rows):
        tiles_per_seq = tokens2d.shape[0] // nb // rows
        qkv, z, qk_g, v_g, r_g, gates, small, tail = _in_proj(tokens2d, g_mix, w_all, conv_w, hist, rows, tiles_per_seq)
        r3 = lambda a: a.reshape(nb, -1, a.shape[-1])
        o_dn, s_dn = _dn_chunk(r3(qkv), r3(small), r3(z), s0_dn, alog, dtb, nh_dn)
        o_gla, s_gla = _gla_chunk(r3(qk_g), r3(v_g), r3(r_g), r3(small), wa, ba, nh_gla, s0_gla)
        return tail, o_dn, o_gla, gates, s_dn, s_gla

    meta_rows = jnp.pad(meta_tokens.astype(F32), ((CHUNK - N_META, 0), (0, 0)))
    zero_hist = jnp.zeros((HIST_ROWS, 3 * DN_QK), F32)
    zero_dn = jnp.zeros((DN_HEADS, DN_HEAD_DIM, DN_HEAD_DIM), F32)
    zero_gla = jnp.zeros((GLA_HEADS, GLA_VAL_DIM, GLA_KEY_DIM), F32)
    hist, _, _, _, s_dn_m, s_gla_m = mixers(meta_rows, 1, zero_hist, zero_dn, zero_gla, CHUNK)

    x2d = x.reshape(t, d)
    rows = _largest_tile(seq, 512)
    _, o_dn, o_gla, gates, _, _ = mixers(x2d, bsz, hist, s_dn_m[0], s_gla_m[0], rows)

    wr_t = jnp.concatenate([w_group[0], w_router[0]], axis=1).T
    wr_t = jnp.pad(wr_t, ((0, LANES - wr_t.shape[0]), (0, 0)))
    br = jnp.pad(jnp.concatenate([b_group[0], b_router[0]]), (0, LANES - N_GROUPS - N_EXPERTS)).reshape(LANES, 1)
    h2, hn, logits_t = _out_proj(
        o_dn.reshape(t, DN_V), o_gla.reshape(t, GLA_V), gates, x2d,
        w_proj_dn[0].astype(BF16), w_proj_gla[0].astype(BF16), w_out[0].astype(BF16),
        norm_ffn[0].reshape(1, D_MODEL), wr_t, br, rows)

    idx, gate, cnt = _route(logits_t, _largest_tile(t, 2048))

    sizes = cnt[:, 0].astype(jnp.int32)
    padded = (sizes + MOE_BLOCK - 1) // MOE_BLOCK * MOE_BLOCK
    pends = jnp.cumsum(padded)
    pstart = (pends - padded).astype(jnp.int32)
    n_blocks = (2 * t) // MOE_BLOCK + N_EXPERTS
    n_used = (pends[-1:] // MOE_BLOCK).astype(jnp.int32)
    block_row0 = jnp.arange(n_blocks, dtype=jnp.int32) * MOE_BLOCK
    block_expert = jnp.minimum(jnp.sum(pends[None, :] <= block_row0[:, None], axis=1), N_EXPERTS - 1).astype(jnp.int32)

    tok_tile = _largest_tile(t, 256)
    slots = _slots(pstart, idx, _largest_tile(t, 2048))
    xs = _dispatch(pstart, sizes, slots, hn, n_blocks * MOE_BLOCK, tok_tile)
    yb = _experts(block_expert, n_used, sizes, xs, w1[0], w3[0], w2[0])
    out = _combine(slots, gate.T, h2, norm_final.reshape(1, D_MODEL), yb, tok_tile)
    return out.reshape(bsz, seq, d)
```
